```python
import jax, jax.numpy as jnp
from jax import lax
import numpy as np

D_MODEL = 2048
BATCH = 2
SEQ = 4096
DEPTH = 2

GRID_W = 64
CTX_LEN = 256
N_MIXERS = 2
N_HEADS = 16
HEAD_DIM = D_MODEL // N_HEADS
WIN_ROWS = 8
WIN_COLS = 16
CONV_WIDTH = 31
N_EXPERTS = 64
TOP_K = 8
EXPERT_DIM = D_MODEL // 4
SHARED_DIM = D_MODEL // 4
ROUTED_SCALE = 2.5
EXPERT_BLOCK = 128
LN_EPS = 1e-6
NEG_INF = -1e30
DEEPNORM_ALPHA = (2 * DEPTH) ** 0.25
DEEPNORM_BETA = (8 * DEPTH) ** -0.25
N_ATTN_LAYERS = (DEPTH + 1) // 2
N_CONV_LAYERS = DEPTH // 2

kernel_name = 'hybrid_natten_conformer_moe_dit'


def layer_norm(x, g, b):
    xf = x.astype(jnp.float32)
    mu = jnp.mean(xf, axis=-1, keepdims=True)
    var = jnp.mean(jnp.square(xf - mu), axis=-1, keepdims=True)
    y = (xf - mu) * lax.rsqrt(var + LN_EPS)
    return (y * g.astype(jnp.float32) + b.astype(jnp.float32)).astype(x.dtype)


def na_attend(q, k, v, kc, vc, rpb):
    B, N, H, hd = q.shape
    rows = N // GRID_W
    wr = min(WIN_ROWS, rows)
    wc = min(WIN_COLS, GRID_W)
    scale = hd ** -0.5
    qg = q.reshape(B, rows, GRID_W, H, hd)
    kg = k.reshape(B, rows, GRID_W, H, hd)
    vg = v.reshape(B, rows, GRID_W, H, hd)
    col = jnp.arange(GRID_W)
    cstart = jnp.clip(col - wc // 2, 0, GRID_W - wc)
    col_ok = (col[None, :] >= cstart[:, None]) & (col[None, :] < cstart[:, None] + wc)
    cidx = jnp.clip(col[None, :] - col[:, None] + WIN_COLS - 1, 0, 2 * WIN_COLS - 2)
    mask = jnp.broadcast_to(col_ok[:, None, :], (GRID_W, wr, GRID_W)).reshape(GRID_W, wr * GRID_W)
    n_loc = wr * GRID_W

    def one_row(r):
        rs = jnp.clip(r - wr // 2, 0, rows - wr)
        q_r = lax.dynamic_index_in_dim(qg, r, axis=1, keepdims=False)
        k_s = lax.dynamic_slice_in_dim(kg, rs, wr, axis=1).reshape(B, n_loc, H, hd)
        v_s = lax.dynamic_slice_in_dim(vg, rs, wr, axis=1).reshape(B, n_loc, H, hd)
        ridx = rs + jnp.arange(wr) - r + WIN_ROWS - 1
        bias = rpb[:, ridx[None, :, None], cidx[:, None, :]].reshape(H, GRID_W, n_loc)
        s_loc = jnp.einsum('bqhd,bkhd->bhqk', q_r, k_s).astype(jnp.float32) * scale + bias.astype(jnp.float32)
        s_loc = jnp.where(mask, s_loc, NEG_INF)
        s_ctx = jnp.einsum('bqhd,bkhd->bhqk', q_r, kc).astype(jnp.float32) * scale
        p = jax.nn.softmax(jnp.concatenate([s_loc, s_ctx], axis=-1), axis=-1).astype(v.dtype)
        return (jnp.einsum('bhqk,bkhd->bqhd', p[..., :n_loc], v_s)
                + jnp.einsum('bhqk,bkhd->bqhd', p[..., n_loc:], vc))

    out = lax.map(one_row, jnp.arange(rows))
    return jnp.moveaxis(out, 0, 1).reshape(B, N, H * hd)


def context_attend(qc, kc, vc):
    B, Lc, H, hd = qc.shape
    s = jnp.einsum('bqhd,bkhd->bhqk', qc, kc).astype(jnp.float32) * (hd ** -0.5)
    p = jax.nn.softmax(s, axis=-1).astype(vc.dtype)
    return jnp.einsum('bhqk,bkhd->bqhd', p, vc).reshape(B, Lc, H * hd)


def neighbourhood_mixer(u, uc, w_qkv, rpb, w_o, ctx_out):
    B, N, D = u.shape
    Lc = uc.shape[1]
    qkv = jnp.dot(u, w_qkv).reshape(B, N, 3, N_HEADS, HEAD_DIM)
    q, k, v = qkv[:, :, 0], qkv[:, :, 1], qkv[:, :, 2]
    if ctx_out:
        qkv_c = jnp.dot(uc, w_qkv).reshape(B, Lc, 3, N_HEADS, HEAD_DIM)
        qc, kc, vc = qkv_c[:, :, 0], qkv_c[:, :, 1], qkv_c[:, :, 2]
    else:
        kv_c = jnp.dot(uc, w_qkv[:, D:]).reshape(B, Lc, 2, N_HEADS, HEAD_DIM)
        kc, vc = kv_c[:, :, 0], kv_c[:, :, 1]
    m = jnp.dot(na_attend(q, k, v, kc, vc, rpb), w_o)
    mc = jnp.dot(context_attend(qc, kc, vc), w_o) if ctx_out else None
    return m, mc


def conv_module(u, w_pw1, b_pw1, w_dw, b_dw, ln_g, ln_b, w_pw2, b_pw2):
    D = u.shape[-1]
    a, g = jnp.split(jnp.dot(u, w_pw1) + b_pw1, 2, axis=-1)
    h = a * jax.nn.sigmoid(g)
    h = lax.conv_general_dilated(h, w_dw[:, None, :], window_strides=(1,),
                                 padding=[(CONV_WIDTH // 2, CONV_WIDTH // 2)],
                                 dimension_numbers=('NWC', 'WIO', 'NWC'),
                                 feature_group_count=D) + b_dw
    h = jax.nn.silu(layer_norm(h, ln_g, ln_b))
    return jnp.dot(h, w_pw2) + b_pw2


def moe_ffn(h, w_router, router_bias, we_gate, we_up, we_down, ws_gate, ws_up, ws_down):
    T, D = h.shape
    E = we_gate.shape[0]
    scores = jax.nn.sigmoid(jnp.dot(h, w_router).astype(jnp.float32))
    _, idx = lax.top_k(scores + router_bias.astype(jnp.float32), TOP_K)
    gate = jnp.take_along_axis(scores, idx, axis=-1)
    gate = gate / jnp.sum(gate, axis=-1, keepdims=True) * ROUTED_SCALE
    n_assign = T * TOP_K
    flat_e = idx.reshape(-1).astype(jnp.int32)
    flat_tok = jnp.arange(n_assign, dtype=jnp.int32) // TOP_K
    flat_g = gate.reshape(-1)
    order = jnp.argsort(flat_e)
    se = flat_e[order]
    counts = jnp.bincount(flat_e, length=E)
    padded = (counts + EXPERT_BLOCK - 1) // EXPERT_BLOCK * EXPERT_BLOCK
    starts = jnp.cumsum(counts) - counts
    pends = jnp.cumsum(padded)
    pstarts = pends - padded
    dest = pstarts[se] + (jnp.arange(n_assign, dtype=jnp.int32) - starts[se])
    n_rows = (n_assign + E * (EXPERT_BLOCK - 1) + EXPERT_BLOCK - 1) // EXPERT_BLOCK * EXPERT_BLOCK
    n_blocks = n_rows // EXPERT_BLOCK
    row_tok = jnp.full((n_rows,), T, jnp.int32).at[dest].set(flat_tok[order])
    row_g = jnp.zeros((n_rows,), jnp.float32).at[dest].set(flat_g[order])
    blk_e = jnp.minimum(jnp.searchsorted(pends, jnp.arange(n_blocks) * EXPERT_BLOCK, side='right'), E - 1)
    h_pad = jnp.concatenate([h, jnp.zeros((1, D), h.dtype)], axis=0)

    def run_block(args):
        e, tok, g = args
        xb = h_pad[tok]
        yb = jnp.dot(jax.nn.silu(jnp.dot(xb, we_gate[e])) * jnp.dot(xb, we_up[e]), we_down[e])
        return yb * g[:, None].astype(yb.dtype)

    y_rows = lax.map(run_block, (blk_e, row_tok.reshape(n_blocks, EXPERT_BLOCK),
                                 row_g.reshape(n_blocks, EXPERT_BLOCK)))
    routed = jax.ops.segment_sum(y_rows.reshape(n_rows, D), row_tok, num_segments=T + 1)[:T]
    shared = jnp.dot(jax.nn.silu(jnp.dot(h, ws_gate)) * jnp.dot(h, ws_up), ws_down)
    return routed + shared


def setup_inputs(seed: int = 0) -> dict:
    key = jax.random.key(seed)
    ks = jax.random.split(key, 32)
    D = D_MODEL

    def nrm(k, shape, scale):
        return jax.random.normal(k, shape, jnp.float32) * scale

    return {
        'x': nrm(ks[0], (BATCH, SEQ, D), 1.0),
        'c': nrm(ks[1], (BATCH, D), 1.0),
        'ctx': nrm(ks[2], (BATCH, CTX_LEN, D), 1.0),
        'c_ctx': nrm(ks[3], (D,), 1.0),
        'mod_w': nrm(ks[4], (DEPTH, D, 6 * D), 0.5 * D ** -0.5),
        'mod_b': nrm(ks[5], (DEPTH, 6 * D), 0.02),
        'ln_mix_g': 1.0 + nrm(ks[6], (DEPTH, D), 0.02),
        'ln_mix_b': nrm(ks[7], (DEPTH, D), 0.02),
        'ln_ffn_g': 1.0 + nrm(ks[8], (DEPTH, D), 0.02),
        'ln_ffn_b': nrm(ks[9], (DEPTH, D), 0.02),
        'na_w_qkv': nrm(ks[10], (N_ATTN_LAYERS, D, 3 * D), D ** -0.5),
        'na_rpb': nrm(ks[11], (N_ATTN_LAYERS, N_HEADS, 2 * WIN_ROWS - 1, 2 * WIN_COLS - 1), 0.02),
        'na_w_o': nrm(ks[12], (N_ATTN_LAYERS, D, D), DEEPNORM_BETA * D ** -0.5),
        'cv_w_pw1': nrm(ks[13], (N_CONV_LAYERS, D, 2 * D), D ** -0.5),
        'cv_b_pw1': nrm(ks[14], (N_CONV_LAYERS, 2 * D), 0.02),
        'cv_w_dw': nrm(ks[15], (N_CONV_LAYERS, CONV_WIDTH, D), CONV_WIDTH ** -0.5),
        'cv_b_dw': nrm(ks[16], (N_CONV_LAYERS, D), 0.02),
        'cv_ln_g': 1.0 + nrm(ks[17], (N_CONV_LAYERS, D), 0.02),
        'cv_ln_b': nrm(ks[18], (N_CONV_LAYERS, D), 0.02),
        'cv_w_pw2': nrm(ks[19], (N_CONV_LAYERS, D, D), DEEPNORM_BETA * D ** -0.5),
        'cv_b_pw2': nrm(ks[20], (N_CONV_LAYERS, D), 0.02),
        'moe_w_router': nrm(ks[21], (DEPTH, D, N_EXPERTS), D ** -0.5),
        'moe_router_bias': nrm(ks[22], (DEPTH, N_EXPERTS), 0.01),
        'moe_w_gate': nrm(ks[23], (DEPTH, N_EXPERTS, D, EXPERT_DIM), D ** -0.5),
        'moe_w_up': nrm(ks[24], (DEPTH, N_EXPERTS, D, EXPERT_DIM), D ** -0.5),
        'moe_w_down': nrm(ks[25], (DEPTH, N_EXPERTS, EXPERT_DIM, D), DEEPNORM_BETA * EXPERT_DIM ** -0.5),
        'sh_w_gate': nrm(ks[26], (DEPTH, D, SHARED_DIM), D ** -0.5),
        'sh_w_up': nrm(ks[27], (DEPTH, D, SHARED_DIM), D ** -0.5),
        'sh_w_down': nrm(ks[28], (DEPTH, SHARED_DIM, D), DEEPNORM_BETA * SHARED_DIM ** -0.5),
    }


def reference(x, c, ctx, c_ctx, mod_w, mod_b, ln_mix_g, ln_mix_b, ln_ffn_g, ln_ffn_b,
              na_w_qkv, na_rpb, na_w_o, cv_w_pw1, cv_b_pw1, cv_w_dw, cv_b_dw, cv_ln_g, cv_ln_b,
              cv_w_pw2, cv_b_pw2, moe_w_router, moe_router_bias, moe_w_gate, moe_w_up, moe_w_down,
              sh_w_gate, sh_w_up, sh_w_down):
    B, N, D = x.shape
    Lc = ctx.shape[1]
    alpha = DEEPNORM_ALPHA
    h, hc = x, ctx
    for i in range(DEPTH):
        last = i == DEPTH - 1
        j = i // N_MIXERS
        mod = jnp.dot(jax.nn.silu(c), mod_w[i]) + mod_b[i]
        mod_c = jnp.dot(jax.nn.silu(c_ctx), mod_w[i]) + mod_b[i]
        sh_a, sc_a, g_a, sh_f, sc_f, g_f = jnp.split(mod[:, None, :], 6, axis=-1)
        csh_a, csc_a, cg_a, csh_f, csc_f, cg_f = jnp.split(mod_c, 6, axis=-1)
        u = h * (1.0 + sc_a) + sh_a
        uc = hc * (1.0 + csc_a) + csh_a
        if i % N_MIXERS == 0:
            m, mc = neighbourhood_mixer(u, uc, na_w_qkv[j], na_rpb[j], na_w_o[j], not last)
        else:
            m = conv_module(u, cv_w_pw1[j], cv_b_pw1[j], cv_w_dw[j], cv_b_dw[j],
                            cv_ln_g[j], cv_ln_b[j], cv_w_pw2[j], cv_b_pw2[j])
            mc = None if last else conv_module(uc, cv_w_pw1[j], cv_b_pw1[j], cv_w_dw[j], cv_b_dw[j],
                                               cv_ln_g[j], cv_ln_b[j], cv_w_pw2[j], cv_b_pw2[j])
        h = layer_norm(alpha * h + g_a * m, ln_mix_g[i], ln_mix_b[i])
        f_in = h * (1.0 + sc_f) + sh_f
        moe_args = (moe_w_router[i], moe_router_bias[i], moe_w_gate[i], moe_w_up[i], moe_w_down[i],
                    sh_w_gate[i], sh_w_up[i], sh_w_down[i])
        if last:
            f = moe_ffn(f_in.reshape(B * N, D), *moe_args).reshape(B, N, D)
        else:
            hc = layer_norm(alpha * hc + cg_a * mc, ln_mix_g[i], ln_mix_b[i])
            fc_in = hc * (1.0 + csc_f) + csh_f
            tokens = jnp.concatenate([f_in.reshape(B * N, D), fc_in.reshape(B * Lc, D)], axis=0)
            out = moe_ffn(tokens, *moe_args)
            f = out[:B * N].reshape(B, N, D)
            fc = out[B * N:].reshape(B, Lc, D)
            hc = layer_norm(alpha * hc + cg_f * fc, ln_ffn_g[i], ln_ffn_b[i])
        h = layer_norm(alpha * h + g_f * f, ln_ffn_g[i], ln_ffn_b[i])
    return h
```

```python
import functools

import jax
import jax.numpy as jnp
from jax import lax
from jax.experimental import pallas as pl
from jax.experimental.pallas import tpu as pltpu

D_MODEL = 2048
GRID_W = 64
N_HEADS = 16
HEAD_DIM = D_MODEL // N_HEADS
WIN_ROWS = 8
WIN_COLS = 16
CONV_WIDTH = 31
N_EXPERTS = 64
TOP_K = 8
ROUTED_SCALE = 2.5
LN_EPS = 1e-6
NEG_INF = -1e30
DEPTH = 2
DEEPNORM_ALPHA = (2 * DEPTH) ** 0.25

F32 = jnp.float32
BF16 = jnp.bfloat16

VMEM_LIMIT_BYTES = 56 * 1024 * 1024
ROW_TILE = 256
MM_ROW_TILE = 512
EXPERT_TILE = 256
ROWS_PER_STEP = 8
CONV_CH = 256
CONV_TT = 64


def _params(sem):
    return pltpu.CompilerParams(dimension_semantics=sem, vmem_limit_bytes=VMEM_LIMIT_BYTES)


def _layer_norm(z, g, b):
    mu = jnp.mean(z, axis=-1, keepdims=True)
    zc = z - mu
    var = jnp.mean(zc * zc, axis=-1, keepdims=True)
    return zc * lax.rsqrt(var + LN_EPS) * g + b


def _silu(x):
    return x * jax.nn.sigmoid(x)


def _mod_kernel(c_ref, w_ref, b_ref, o_ref):
    a = _silu(c_ref[...]).astype(BF16)
    o_ref[...] = jnp.dot(a, w_ref[...].astype(BF16), preferred_element_type=F32) + b_ref[...]


def _mod_matmul(cvec, w, b):
    m, k = cvec.shape
    n = w.shape[1]
    tn = 1024
    return pl.pallas_call(
        _mod_kernel,
        grid=(n // tn,),
        in_specs=[pl.BlockSpec((m, k), lambda j: (0, 0)),
                  pl.BlockSpec((k, tn), lambda j: (0, j)),
                  pl.BlockSpec((1, tn), lambda j: (0, j))],
        out_specs=pl.BlockSpec((m, tn), lambda j: (0, j)),
        out_shape=jax.ShapeDtypeStruct((m, n), F32),
        compiler_params=_params(("arbitrary",)),
        name="mod_matmul",
    )(cvec, w, b.reshape(1, n))


def _modmm_kernel(h_ref, sc_ref, sh_ref, w_ref, o_ref, a_scr):
    @pl.when(pl.program_id(1) == 0)
    def _():
        a_scr[...] = (h_ref[...] * (1.0 + sc_ref[0]) + sh_ref[0]).astype(BF16)

    o_ref[...] = jnp.dot(a_scr[...], w_ref[...], preferred_element_type=F32).astype(o_ref.dtype)


def _modglu_kernel(h_ref, sc_ref, sh_ref, wa_ref, wg_ref, ba_ref, bg_ref, o_ref, a_scr):
    @pl.when(pl.program_id(1) == 0)
    def _():
        a_scr[...] = (h_ref[...] * (1.0 + sc_ref[0]) + sh_ref[0]).astype(BF16)

    a = a_scr[...]
    va = jnp.dot(a, wa_ref[...], preferred_element_type=F32) + ba_ref[...]
    vg = jnp.dot(a, wg_ref[...], preferred_element_type=F32) + bg_ref[...]
    o_ref[...] = va * jax.nn.sigmoid(vg)


def _group_of_tile(tile_rows, seq):
    return lambda i: (i * tile_rows) // seq


def _mod_matmul_qkv(h, sc, sh, w, n_rows, seq):
    k = h.shape[1]
    n = w.shape[1]
    tm, tn = MM_ROW_TILE, 2048
    grp = _group_of_tile(tm, seq)
    return pl.pallas_call(
        _modmm_kernel,
        grid=(n_rows // tm, n // tn),
        in_specs=[pl.BlockSpec((tm, k), lambda i, j: (i, 0)),
                  pl.BlockSpec((1, 1, k), lambda i, j: (grp(i), 0, 0)),
                  pl.BlockSpec((1, 1, k), lambda i, j: (grp(i), 0, 0)),
                  pl.BlockSpec((k, tn), lambda i, j: (0, j))],
        out_specs=pl.BlockSpec((tm, tn), lambda i, j: (i, j)),
        out_shape=jax.ShapeDtypeStruct((n_rows, n), BF16),
        scratch_shapes=[pltpu.VMEM((tm, k), BF16)],
        compiler_params=_params(("parallel", "arbitrary")),
        name="mod_qkv",
    )(h, sc, sh, w)


def _mod_matmul_glu(h, sc, sh, w, b, n_rows, seq):
    k = h.shape[1]
    n = w.shape[1] // 2
    tm, tn = MM_ROW_TILE, 1024
    grp = _group_of_tile(tm, seq)
    nj = n // tn
    b2 = b.reshape(1, 2 * n)
    return pl.pallas_call(
        _modglu_kernel,
        grid=(n_rows // tm, nj),
        in_specs=[pl.BlockSpec((tm, k), lambda i, j: (i, 0)),
                  pl.BlockSpec((1, 1, k), lambda i, j: (grp(i), 0, 0)),
                  pl.BlockSpec((1, 1, k), lambda i, j: (grp(i), 0, 0)),
                  pl.BlockSpec((k, tn), lambda i, j: (0, j)),
                  pl.BlockSpec((k, tn), lambda i, j: (0, j + nj)),
                  pl.BlockSpec((1, tn), lambda i, j: (0, j)),
                  pl.BlockSpec((1, tn), lambda i, j: (0, j + nj))],
        out_specs=pl.BlockSpec((tm, tn), lambda i, j: (i, j)),
        out_shape=jax.ShapeDtypeStruct((n_rows, n), F32),
        scratch_shapes=[pltpu.VMEM((tm, k), BF16)],
        compiler_params=_params(("parallel", "arbitrary")),
        name="mod_pw1_glu",
    )(h, sc, sh, w, w, b2, b2)


def _na_kernel(q_ref, k_ref, v_ref, kc_ref, vc_ref, bias_ref, o_ref, *, rows):
    g = pl.program_id(2)
    scale = HEAD_DIM ** -0.5
    kc = kc_ref[...]
    vc = vc_ref[...]
    n_loc = WIN_ROWS * GRID_W
    contract_last = (((1,), (1,)), ((), ()))

    def body(rr, carry):
        r = g * ROWS_PER_STEP + rr
        rs = jnp.clip(r - WIN_ROWS // 2, 0, rows - WIN_ROWS)
        off = r - rs
        q = q_ref[pl.ds(pl.multiple_of(rr * GRID_W, GRID_W), GRID_W), :]
        k_start = pl.multiple_of(rs * GRID_W, GRID_W)
        ks = k_ref[pl.ds(k_start, n_loc), :]
        vs = v_ref[pl.ds(k_start, n_loc), :]
        s_loc = lax.dot_general(q, ks, contract_last, preferred_element_type=F32) * scale + bias_ref[0, off]
        s_ctx = lax.dot_general(q, kc, contract_last, preferred_element_type=F32) * scale
        m = jnp.maximum(jnp.max(s_loc, axis=-1, keepdims=True), jnp.max(s_ctx, axis=-1, keepdims=True))
        p_loc = jnp.exp(s_loc - m)
        p_ctx = jnp.exp(s_ctx - m)
        denom = jnp.sum(p_loc, axis=-1, keepdims=True) + jnp.sum(p_ctx, axis=-1, keepdims=True)
        o = (jnp.dot(p_loc.astype(BF16), vs, preferred_element_type=F32)
             + jnp.dot(p_ctx.astype(BF16), vc, preferred_element_type=F32))
        o_ref[pl.ds(pl.multiple_of(rr * GRID_W, GRID_W), GRID_W), :] = (o / denom).astype(o_ref.dtype)
        return carry

    lax.fori_loop(0, ROWS_PER_STEP, body, 0)


def _na_bias_table(rpb):
    col = jnp.arange(GRID_W)
    cstart = jnp.clip(col - WIN_COLS // 2, 0, GRID_W - WIN_COLS)
    col_ok = (col[None, :] >= cstart[:, None]) & (col[None, :] < cstart[:, None] + WIN_COLS)
    cidx = jnp.clip(col[None, :] - col[:, None] + WIN_COLS - 1, 0, 2 * WIN_COLS - 2)
    off = jnp.arange(WIN_ROWS)
    ridx = jnp.arange(WIN_ROWS)[None, :] - off[:, None] + WIN_ROWS - 1
    t = rpb[:, ridx[:, None, :, None], cidx[None, :, None, :]]
    t = jnp.where(col_ok[None, None, :, None, :], t.astype(F32), NEG_INF)
    return t.reshape(rpb.shape[0], WIN_ROWS, GRID_W, WIN_ROWS * GRID_W)


def _na_attention(qkv, bias_table, n_batch, seq, ctx_len, n_rows_out):
    rows = seq // GRID_W
    groups = rows // ROWS_PER_STEP
    tq = ROWS_PER_STEP * GRID_W
    ctx_blk0 = (n_batch * seq) // ctx_len
    h_ = N_HEADS
    return pl.pallas_call(
        functools.partial(_na_kernel, rows=rows),
        grid=(n_batch, h_, groups),
        in_specs=[pl.BlockSpec((tq, HEAD_DIM), lambda b, h, g: (b * groups + g, h)),
                  pl.BlockSpec((seq, HEAD_DIM), lambda b, h, g: (b, h_ + h)),
                  pl.BlockSpec((seq, HEAD_DIM), lambda b, h, g: (b, 2 * h_ + h)),
                  pl.BlockSpec((ctx_len, HEAD_DIM), lambda b, h, g: (ctx_blk0 + b, h_ + h)),
                  pl.BlockSpec((ctx_len, HEAD_DIM), lambda b, h, g: (ctx_blk0 + b, 2 * h_ + h)),
                  pl.BlockSpec((1, WIN_ROWS, GRID_W, WIN_ROWS * GRID_W), lambda b, h, g: (h, 0, 0, 0))],
        out_specs=pl.BlockSpec((tq, HEAD_DIM), lambda b, h, g: (b * groups + g, h)),
        out_shape=jax.ShapeDtypeStruct((n_rows_out, D_MODEL), BF16),
        compiler_params=_params(("parallel", "parallel", "arbitrary")),
        name="na_attention",
    )(qkv, qkv, qkv, qkv, qkv, bias_table)


def _ctx_attn_kernel(q_ref, k_ref, v_ref, prev_ref, o_ref):
    del prev_ref
    scale = HEAD_DIM ** -0.5
    s = lax.dot_general(q_ref[...], k_ref[...], (((1,), (1,)), ((), ())), preferred_element_type=F32) * scale
    m = jnp.max(s, axis=-1, keepdims=True)
    p = jnp.exp(s - m)
    denom = jnp.sum(p, axis=-1, keepdims=True)
    o = jnp.dot(p.astype(BF16), v_ref[...], preferred_element_type=F32)
    o_ref[...] = (o / denom).astype(o_ref.dtype)


def _ctx_attention(qkv, attn_out, n_batch, seq, ctx_len):
    ctx_blk0 = (n_batch * seq) // ctx_len
    h_ = N_HEADS
    return pl.pallas_call(
        _ctx_attn_kernel,
        grid=(n_batch, h_),
        in_specs=[pl.BlockSpec((ctx_len, HEAD_DIM), lambda b, h: (ctx_blk0 + b, h)),
                  pl.BlockSpec((ctx_len, HEAD_DIM), lambda b, h: (ctx_blk0 + b, h_ + h)),
                  pl.BlockSpec((ctx_len, HEAD_DIM), lambda b, h: (ctx_blk0 + b, 2 * h_ + h)),
                  pl.BlockSpec(memory_space=pl.ANY)],
        out_specs=pl.BlockSpec((ctx_len, HEAD_DIM), lambda b, h: (ctx_blk0 + b, h)),
        out_shape=jax.ShapeDtypeStruct(attn_out.shape, attn_out.dtype),
        input_output_aliases={3: 0},
        compiler_params=_params(("parallel", "parallel")),
        name="ctx_attention",
    )(qkv, qkv, qkv, attn_out)


def _dwconv_kernel(x_ref, w_ref, b_ref, o_ref, xpad, *, seq):
    half = CONV_WIDTH // 2
    lead = 16
    ch = x_ref.shape[1]
    xpad[pl.ds(0, lead), :] = jnp.zeros((lead, ch), F32)
    xpad[pl.ds(lead + seq, lead), :] = jnp.zeros((lead, ch), F32)
    xpad[pl.ds(lead, seq), :] = x_ref[...]
    w = w_ref[...]
    bias = b_ref[...]

    win_rows = CONV_TT + 2 * lead
    sub = 8

    def body(c, carry):
        t0 = pl.multiple_of(c * CONV_TT, CONV_TT)
        win = xpad[pl.ds(t0, win_rows), :]
        acc = jnp.broadcast_to(bias, (CONV_TT, ch))
        for p in range(sub):
            shifted = win if p == 0 else pltpu.roll(win, win_rows - p, axis=0)
            for k in range(CONV_WIDTH):
                o = lead - half + k
                if o % sub == p:
                    acc = acc + shifted[o - p:o - p + CONV_TT, :] * w[k:k + 1, :]
        o_ref[pl.ds(t0, CONV_TT), :] = acc
        return carry

    lax.fori_loop(0, seq // CONV_TT, body, 0)


def _dwconv(x, w_dw, b_dw, n_batch, seq):
    d = x.shape[1]
    nc = d // CONV_CH
    return pl.pallas_call(
        functools.partial(_dwconv_kernel, seq=seq),
        grid=(n_batch, nc),
        in_specs=[pl.BlockSpec((seq, CONV_CH), lambda b, c: (b, c)),
                  pl.BlockSpec((CONV_WIDTH, CONV_CH), lambda b, c: (0, c)),
                  pl.BlockSpec((1, CONV_CH), lambda b, c: (0, c))],
        out_specs=pl.BlockSpec((seq, CONV_CH), lambda b, c: (b, c)),
        out_shape=jax.ShapeDtypeStruct((n_batch * seq, d), F32),
        scratch_shapes=[pltpu.VMEM((seq + 32, CONV_CH), F32)],
        compiler_params=_params(("parallel", "parallel")),
        name="dwconv",
    )(x, w_dw, b_dw.reshape(1, d))


def _proj_ln_kernel(*refs, pre_ln, has_bias):
    it = iter(refs)
    a_ref = next(it)
    pre_g = next(it) if pre_ln else None
    pre_b = next(it) if pre_ln else None
    w_ref = next(it)
    bias_ref = next(it) if has_bias else None
    h_ref, gate_ref, lng_ref, lnb_ref, scf_ref, shf_ref, wr_ref = (next(it) for _ in range(7))
    hnew_ref, fin_ref, score_ref = (next(it) for _ in range(3))

    a = a_ref[...]
    if pre_ln:
        a = _silu(_layer_norm(a, pre_g[...], pre_b[...]))
    m = jnp.dot(a.astype(BF16), w_ref[...], preferred_element_type=F32)
    if has_bias:
        m = m + bias_ref[...]
    hn = _layer_norm(DEEPNORM_ALPHA * h_ref[...] + gate_ref[0] * m, lng_ref[...], lnb_ref[...])
    hnew_ref[...] = hn
    fb = (hn * (1.0 + scf_ref[0]) + shf_ref[0]).astype(BF16)
    fin_ref[...] = fb
    score_ref[...] = jax.nn.sigmoid(jnp.dot(fb, wr_ref[...], preferred_element_type=F32))


def _proj_ln(a, w, bias, h, gate, ln_g, ln_b, sc_f, sh_f, w_router, n_rows, seq, pre_ln=None):
    d = D_MODEL
    tm = ROW_TILE
    grp = _group_of_tile(tm, seq)
    row = lambda i: (i, 0)
    const = lambda i: (0, 0)
    grp3 = lambda i: (grp(i), 0, 0)
    args, specs = [a], [pl.BlockSpec((tm, d), row)]
    if pre_ln is not None:
        args += [pre_ln[0].reshape(1, d), pre_ln[1].reshape(1, d)]
        specs += [pl.BlockSpec((1, d), const)] * 2
    args.append(w)
    specs.append(pl.BlockSpec((d, d), const))
    if bias is not None:
        args.append(bias.reshape(1, d))
        specs.append(pl.BlockSpec((1, d), const))
    args += [h, gate, ln_g.reshape(1, d), ln_b.reshape(1, d), sc_f, sh_f, w_router]
    specs += [pl.BlockSpec((tm, d), row), pl.BlockSpec((1, 1, d), grp3),
              pl.BlockSpec((1, d), const), pl.BlockSpec((1, d), const),
              pl.BlockSpec((1, 1, d), grp3), pl.BlockSpec((1, 1, d), grp3),
              pl.BlockSpec((d, N_EXPERTS), const)]
    return pl.pallas_call(
        functools.partial(_proj_ln_kernel, pre_ln=pre_ln is not None, has_bias=bias is not None),
        grid=(n_rows // tm,),
        in_specs=specs,
        out_specs=[pl.BlockSpec((tm, d), row), pl.BlockSpec((tm, d), row),
                   pl.BlockSpec((tm, N_EXPERTS), row)],
        out_shape=[jax.ShapeDtypeStruct((n_rows, d), F32),
                   jax.ShapeDtypeStruct((n_rows, d), BF16),
                   jax.ShapeDtypeStruct((n_rows, N_EXPERTS), F32)],
        compiler_params=_params(("parallel",)),
        name="proj_ln",
    )(*args)


def _expert_kernel(be_ref, nv_ref, x_ref, wg_ref, wu_ref, wd_ref, y_ref, wgb, wub, wdb):
    i = pl.program_id(0)
    valid = i < nv_ref[0]
    changed = jnp.logical_or(i == 0, be_ref[i] != be_ref[jnp.maximum(i - 1, 0)])
    d, de = wg_ref.shape[1], wg_ref.shape[2]
    n_chunks = 8

    @pl.when(jnp.logical_and(valid, changed))
    def _():
        def cast_chunk(c, carry):
            r0 = pl.multiple_of(c * (d // n_chunks), d // n_chunks)
            wgb[pl.ds(r0, d // n_chunks), :] = wg_ref[0, pl.ds(r0, d // n_chunks), :].astype(BF16)
            wub[pl.ds(r0, d // n_chunks), :] = wu_ref[0, pl.ds(r0, d // n_chunks), :].astype(BF16)
            r1 = pl.multiple_of(c * (de // n_chunks), de // n_chunks)
            wdb[pl.ds(r1, de // n_chunks), :] = wd_ref[0, pl.ds(r1, de // n_chunks), :].astype(BF16)
            return carry

        lax.fori_loop(0, n_chunks, cast_chunk, 0)

    @pl.when(valid)
    def _():
        x = x_ref[...]
        gv = jnp.dot(x, wgb[...], preferred_element_type=F32)
        uv = jnp.dot(x, wub[...], preferred_element_type=F32)
        act = (_silu(gv) * uv).astype(BF16)
        y_ref[...] = jnp.dot(act, wdb[...], preferred_element_type=F32)


def _expert_ffn(xs, blk_e, n_valid, w_gate, w_up, w_down):
    n_rows, d = xs.shape
    de = w_gate.shape[2]
    tm = EXPERT_TILE
    n_blocks = n_rows // tm

    def xmap(i, be, nv):
        return (jnp.minimum(i, nv[0] - 1), 0)

    def wmap(i, be, nv):
        return (be[jnp.minimum(i, nv[0] - 1)], 0, 0)

    grid_spec = pltpu.PrefetchScalarGridSpec(
        num_scalar_prefetch=2,
        grid=(n_blocks,),
        in_specs=[pl.BlockSpec((tm, d), xmap),
                  pl.BlockSpec((1, d, de), wmap),
                  pl.BlockSpec((1, d, de), wmap),
                  pl.BlockSpec((1, de, d), wmap)],
        out_specs=pl.BlockSpec((tm, d), xmap),
        scratch_shapes=[pltpu.VMEM((d, de), BF16), pltpu.VMEM((d, de), BF16), pltpu.VMEM((de, d), BF16)],
    )
    return pl.pallas_call(
        _expert_kernel,
        grid_spec=grid_spec,
        out_shape=jax.ShapeDtypeStruct((n_rows, d), F32),
        compiler_params=_params(("arbitrary",)),
        name="expert_ffn",
    )(blk_e, n_valid, xs, w_gate, w_up, w_down)


def _route(scores, router_bias, tm_e):
    t = scores.shape[0]
    e = N_EXPERTS
    _, idx = lax.top_k(scores + router_bias.astype(F32), TOP_K)
    gate = jnp.take_along_axis(scores, idx, axis=-1)
    gate = gate / jnp.sum(gate, axis=-1, keepdims=True) * ROUTED_SCALE
    n_assign = t * TOP_K
    flat_e = idx.reshape(-1).astype(jnp.int32)
    order = jnp.argsort(flat_e).astype(jnp.int32)
    se = flat_e[order]
    counts = jnp.bincount(flat_e, length=e).astype(jnp.int32)
    padded = (counts + tm_e - 1) // tm_e * tm_e
    starts = jnp.cumsum(counts) - counts
    pends = jnp.cumsum(padded)
    pstarts = pends - padded
    dest_sorted = (pstarts[se] + (jnp.arange(n_assign, dtype=jnp.int32) - starts[se])).astype(jnp.int32)
    n_blocks = (n_assign + e * (tm_e - 1) + tm_e - 1) // tm_e
    n_rows = n_blocks * tm_e
    row_tok = jnp.zeros((n_rows,), jnp.int32).at[dest_sorted].set(order // TOP_K)
    dest = jnp.zeros((n_assign,), jnp.int32).at[order].set(dest_sorted).reshape(t, TOP_K)
    blk_e = jnp.minimum(jnp.searchsorted(pends, jnp.arange(n_blocks, dtype=jnp.int32) * tm_e, side='right'),
                        e - 1).astype(jnp.int32)
    n_valid = (pends[-1] // tm_e).astype(jnp.int32).reshape(1)
    return gate, dest, row_tok, blk_e, n_valid


def _ffn_out_kernel(f_ref, r_ref, wg_ref, wu_ref, wd_ref, h_ref, gate_ref, lng_ref, lnb_ref, o_ref):
    x = f_ref[...]
    gv = jnp.dot(x, wg_ref[...], preferred_element_type=F32)
    uv = jnp.dot(x, wu_ref[...], preferred_element_type=F32)
    act = (_silu(gv) * uv).astype(BF16)
    f = r_ref[...] + jnp.dot(act, wd_ref[...], preferred_element_type=F32)
    o_ref[...] = _layer_norm(DEEPNORM_ALPHA * h_ref[...] + gate_ref[0] * f, lng_ref[...], lnb_ref[...])


def _ffn_out(f_in, routed, ws_gate, ws_up, ws_down, h, gate, ln_g, ln_b, n_rows, seq):
    d = D_MODEL
    ds_ = ws_gate.shape[1]
    tm = ROW_TILE
    grp = _group_of_tile(tm, seq)
    row = lambda i: (i, 0)
    const = lambda i: (0, 0)
    return pl.pallas_call(
        _ffn_out_kernel,
        grid=(n_rows // tm,),
        in_specs=[pl.BlockSpec((tm, d), row), pl.BlockSpec((tm, d), row),
                  pl.BlockSpec((d, ds_), const), pl.BlockSpec((d, ds_), const), pl.BlockSpec((ds_, d), const),
                  pl.BlockSpec((tm, d), row), pl.BlockSpec((1, 1, d), lambda i: (grp(i), 0, 0)),
                  pl.BlockSpec((1, d), const), pl.BlockSpec((1, d), const)],
        out_specs=pl.BlockSpec((tm, d), row),
        out_shape=jax.ShapeDtypeStruct((n_rows, d), F32),
        compiler_params=_params(("parallel",)),
        name="ffn_out",
    )(f_in, routed, ws_gate, ws_up, ws_down, h, gate, ln_g.reshape(1, d), ln_b.reshape(1, d))


def _moe_and_norm(h, f_in, scores, gate_f, router_bias, we_gate, we_up, we_down, ws_gate, ws_up, ws_down,
                  ln_g, ln_b, n_rows, seq):
    gate, dest, row_tok, blk_e, n_valid = _route(scores[:n_rows], router_bias, EXPERT_TILE)
    xs = jnp.take(f_in, row_tok, axis=0)
    y = _expert_ffn(xs, blk_e, n_valid, we_gate, we_up, we_down)
    routed = jnp.sum(jnp.take(y, dest, axis=0) * gate[..., None], axis=1)
    return _ffn_out(f_in, routed, ws_gate.astype(BF16), ws_up.astype(BF16), ws_down.astype(BF16),
                    h, gate_f, ln_g, ln_b, n_rows, seq)


def kernel(x, c, ctx, c_ctx, mod_w, mod_b, ln_mix_g, ln_mix_b, ln_ffn_g, ln_ffn_b, na_w_qkv, na_rpb, na_w_o,
           cv_w_pw1, cv_b_pw1, cv_w_dw, cv_b_dw, cv_ln_g, cv_ln_b, cv_w_pw2, cv_b_pw2, moe_w_router,
           moe_router_bias, moe_w_gate, moe_w_up, moe_w_down, sh_w_gate, sh_w_up, sh_w_down):
    n_batch, seq, d = x.shape
    ctx_len = ctx.shape[1]
    n_lat = n_batch * seq
    n_tok = n_lat + n_batch * ctx_len

    cvec = jnp.zeros((8, d), F32).at[:n_batch].set(c).at[n_batch].set(c_ctx)
    mods = []
    for i in range(DEPTH):
        m = _mod_matmul(cvec, mod_w[i], mod_b[i]).reshape(8, 6, d)[:n_batch + 1]
        mods.append([m[:, p, :].reshape(n_batch + 1, 1, d) for p in range(6)])

    h = jnp.concatenate([x.reshape(n_lat, d), ctx.reshape(n_batch * ctx_len, d)], axis=0)

    sh_a, sc_a, g_a, sh_f, sc_f, g_f = mods[0]
    qkv = _mod_matmul_qkv(h, sc_a, sh_a, na_w_qkv[0].astype(BF16), n_tok, seq)
    attn = _na_attention(qkv, _na_bias_table(na_rpb[0]), n_batch, seq, ctx_len, n_tok)
    attn = _ctx_attention(qkv, attn, n_batch, seq, ctx_len)
    h, f_in, scores = _proj_ln(attn, na_w_o[0].astype(BF16), None, h, g_a, ln_mix_g[0], ln_mix_b[0],
                               sc_f, sh_f, moe_w_router[0].astype(BF16), n_tok, seq)
    h = _moe_and_norm(h, f_in, scores, g_f, moe_router_bias[0], moe_w_gate[0], moe_w_up[0], moe_w_down[0],
                      sh_w_gate[0], sh_w_up[0], sh_w_down[0], ln_ffn_g[0], ln_ffn_b[0], n_tok, seq)

    sh_a, sc_a, g_a, sh_f, sc_f, g_f = mods[1]
    glu = _mod_matmul_glu(h, sc_a, sh_a, cv_w_pw1[0].astype(BF16), cv_b_pw1[0], n_lat, seq)
    conv = _dwconv(glu, cv_w_dw[0], cv_b_dw[0], n_batch, seq)
    h, f_in, scores = _proj_ln(conv, cv_w_pw2[0].astype(BF16), cv_b_pw2[0], h, g_a, ln_mix_g[1], ln_mix_b[1],
                               sc_f, sh_f, moe_w_router[1].astype(BF16), n_lat, seq,
                               pre_ln=(cv_ln_g[0], cv_ln_b[0]))
    h = _moe_and_norm(h, f_in, scores, g_f, moe_router_bias[1], moe_w_gate[1], moe_w_up[1], moe_w_down[1],
                      sh_w_gate[1], sh_w_up[1], sh_w_down[1], ln_ffn_g[1], ln_ffn_b[1], n_lat, seq)
    return h.reshape(n_batch, seq, d)
```

```python
import functools

import jax
import jax.numpy as jnp
from jax import lax
from jax.experimental import pallas as pl
from jax.experimental.pallas import tpu as pltpu

D_MODEL = 2048
GRID_W = 64
N_HEADS = 16
HEAD_DIM = D_MODEL // N_HEADS
WIN_ROWS = 8
WIN_COLS = 16
CONV_WIDTH = 31
N_EXPERTS = 64
TOP_K = 8
ROUTED_SCALE = 2.5
LN_EPS = 1e-6
SUBLANES = 8
NEG_INF = -1e30
DEPTH = 2
DEEPNORM_ALPHA = (2 * DEPTH) ** 0.25

F32 = jnp.float32
BF16 = jnp.bfloat16

VMEM_LIMIT_BYTES = 56 * 1024 * 1024
ROW_TILE = 256
MM_ROW_TILE = 512
EXPERT_TILE = 256
COMBINE_TILE = 128
ROWS_PER_STEP = 8
CONV_CH = 256
CONV_TT = 64


def _params(sem):
    return pltpu.CompilerParams(dimension_semantics=sem, vmem_limit_bytes=VMEM_LIMIT_BYTES)


def _layer_norm(z, g, b):
    mu = jnp.mean(z, axis=-1, keepdims=True)
    zc = z - mu
    var = jnp.mean(zc * zc, axis=-1, keepdims=True)
    return zc * lax.rsqrt(var + LN_EPS) * g + b


def _silu(x):
    return x * jax.nn.sigmoid(x)


def _mod_kernel(c_ref, w_ref, b_ref, o_ref):
    a = _silu(c_ref[...]).astype(BF16)
    o_ref[0] = jnp.dot(a, w_ref[0].astype(BF16), preferred_element_type=F32) + b_ref[0]


def _mod_matmul(cvec, w, b):
    m, k = cvec.shape
    layers, _, n = w.shape
    tn = 1024
    return pl.pallas_call(
        _mod_kernel,
        grid=(layers, n // tn),
        in_specs=[pl.BlockSpec((m, k), lambda l, j: (0, 0)),
                  pl.BlockSpec((1, k, tn), lambda l, j: (l, 0, j)),
                  pl.BlockSpec((1, 1, tn), lambda l, j: (l, 0, j))],
        out_specs=pl.BlockSpec((1, m, tn), lambda l, j: (l, 0, j)),
        out_shape=jax.ShapeDtypeStruct((layers, m, n), F32),
        compiler_params=_params(("parallel", "arbitrary")),
        name="mod_matmul",
    )(cvec, w, b.reshape(layers, 1, n))


def _modmm_kernel(h_ref, sc_ref, sh_ref, w_ref, o_ref, a_scr):
    @pl.when(pl.program_id(1) == 0)
    def _():
        a_scr[...] = (h_ref[...] * (1.0 + sc_ref[0]) + sh_ref[0]).astype(BF16)

    o_ref[...] = jnp.dot(a_scr[...], w_ref[...], preferred_element_type=F32).astype(o_ref.dtype)


def _modglu_kernel(h_ref, sc_ref, sh_ref, wa_ref, wg_ref, ba_ref, bg_ref, o_ref, a_scr):
    @pl.when(pl.program_id(1) == 0)
    def _():
        a_scr[...] = (h_ref[...] * (1.0 + sc_ref[0]) + sh_ref[0]).astype(BF16)

    a = a_scr[...]
    va = jnp.dot(a, wa_ref[...], preferred_element_type=F32) + ba_ref[...]
    vg = jnp.dot(a, wg_ref[...], preferred_element_type=F32) + bg_ref[...]
    o_ref[...] = va * jax.nn.sigmoid(vg)


def _group_of_tile(tile_rows, seq):
    return lambda i: (i * tile_rows) // seq


def _mod_matmul_qkv(h, sc, sh, w, n_rows, seq):
    k = h.shape[1]
    n = w.shape[1]
    tm, tn = MM_ROW_TILE, 2048
    grp = _group_of_tile(tm, seq)
    return pl.pallas_call(
        _modmm_kernel,
        grid=(n_rows // tm, n // tn),
        in_specs=[pl.BlockSpec((tm, k), lambda i, j: (i, 0)),
                  pl.BlockSpec((1, 1, k), lambda i, j: (grp(i), 0, 0)),
                  pl.BlockSpec((1, 1, k), lambda i, j: (grp(i), 0, 0)),
                  pl.BlockSpec((k, tn), lambda i, j: (0, j))],
        out_specs=pl.BlockSpec((tm, tn), lambda i, j: (i, j)),
        out_shape=jax.ShapeDtypeStruct((n_rows, n), BF16),
        scratch_shapes=[pltpu.VMEM((tm, k), BF16)],
        compiler_params=_params(("parallel", "arbitrary")),
        name="mod_qkv",
    )(h, sc, sh, w)


def _mod_matmul_glu(h, sc, sh, w, b, n_rows, seq):
    k = h.shape[1]
    n = w.shape[1] // 2
    tm, tn = MM_ROW_TILE, 1024
    grp = _group_of_tile(tm, seq)
    nj = n // tn
    b2 = b.reshape(1, 2 * n)
    return pl.pallas_call(
        _modglu_kernel,
        grid=(n_rows // tm, nj),
        in_specs=[pl.BlockSpec((tm, k), lambda i, j: (i, 0)),
                  pl.BlockSpec((1, 1, k), lambda i, j: (grp(i), 0, 0)),
                  pl.BlockSpec((1, 1, k), lambda i, j: (grp(i), 0, 0)),
                  pl.BlockSpec((k, tn), lambda i, j: (0, j)),
                  pl.BlockSpec((k, tn), lambda i, j: (0, j + nj)),
                  pl.BlockSpec((1, tn), lambda i, j: (0, j)),
                  pl.BlockSpec((1, tn), lambda i, j: (0, j + nj))],
        out_specs=pl.BlockSpec((tm, tn), lambda i, j: (i, j)),
        out_shape=jax.ShapeDtypeStruct((n_rows, n), F32),
        scratch_shapes=[pltpu.VMEM((tm, k), BF16)],
        compiler_params=_params(("parallel", "arbitrary")),
        name="mod_pw1_glu",
    )(h, sc, sh, w, w, b2, b2)


def _na_kernel(q_ref, k_ref, v_ref, kc_ref, vc_ref, bias_ref, o_ref, *, rows):
    g = pl.program_id(2)
    scale = HEAD_DIM ** -0.5
    kc = kc_ref[...]
    vc = vc_ref[...]
    n_loc = WIN_ROWS * GRID_W
    contract_last = (((1,), (1,)), ((), ()))

    def body(rr, carry):
        r = g * ROWS_PER_STEP + rr
        rs = jnp.clip(r - WIN_ROWS // 2, 0, rows - WIN_ROWS)
        off = r - rs
        q = q_ref[pl.ds(pl.multiple_of(rr * GRID_W, GRID_W), GRID_W), :]
        k_start = pl.multiple_of(rs * GRID_W, GRID_W)
        ks = k_ref[pl.ds(k_start, n_loc), :]
        vs = v_ref[pl.ds(k_start, n_loc), :]
        s_loc = lax.dot_general(q, ks, contract_last, preferred_element_type=F32) * scale + bias_ref[0, off]
        s_ctx = lax.dot_general(q, kc, contract_last, preferred_element_type=F32) * scale
        m = jnp.maximum(jnp.max(s_loc, axis=-1, keepdims=True), jnp.max(s_ctx, axis=-1, keepdims=True))
        p_loc = jnp.exp(s_loc - m)
        p_ctx = jnp.exp(s_ctx - m)
        denom = jnp.sum(p_loc, axis=-1, keepdims=True) + jnp.sum(p_ctx, axis=-1, keepdims=True)
        o = (jnp.dot(p_loc.astype(BF16), vs, preferred_element_type=F32)
             + jnp.dot(p_ctx.astype(BF16), vc, preferred_element_type=F32))
        o_ref[pl.ds(pl.multiple_of(rr * GRID_W, GRID_W), GRID_W), :] = (o / denom).astype(o_ref.dtype)
        return carry

    lax.fori_loop(0, ROWS_PER_STEP, body, 0)


def _na_bias_table(rpb):
    col = jnp.arange(GRID_W)
    cstart = jnp.clip(col - WIN_COLS // 2, 0, GRID_W - WIN_COLS)
    col_ok = (col[None, :] >= cstart[:, None]) & (col[None, :] < cstart[:, None] + WIN_COLS)
    cidx = jnp.clip(col[None, :] - col[:, None] + WIN_COLS - 1, 0, 2 * WIN_COLS - 2)
    off = jnp.arange(WIN_ROWS)
    ridx = jnp.arange(WIN_ROWS)[None, :] - off[:, None] + WIN_ROWS - 1
    t = rpb[:, ridx[:, None, :, None], cidx[None, :, None, :]]
    t = jnp.where(col_ok[None, None, :, None, :], t.astype(F32), NEG_INF)
    return t.reshape(rpb.shape[0], WIN_ROWS, GRID_W, WIN_ROWS * GRID_W)


def _na_attention(qkv, bias_table, n_batch, seq, ctx_len, n_rows_out):
    rows = seq // GRID_W
    groups = rows // ROWS_PER_STEP
    tq = ROWS_PER_STEP * GRID_W
    ctx_blk0 = (n_batch * seq) // ctx_len
    h_ = N_HEADS
    return pl.pallas_call(
        functools.partial(_na_kernel, rows=rows),
        grid=(n_batch, h_, groups),
        in_specs=[pl.BlockSpec((tq, HEAD_DIM), lambda b, h, g: (b * groups + g, h)),
                  pl.BlockSpec((seq, HEAD_DIM), lambda b, h, g: (b, h_ + h)),
                  pl.BlockSpec((seq, HEAD_DIM), lambda b, h, g: (b, 2 * h_ + h)),
                  pl.BlockSpec((ctx_len, HEAD_DIM), lambda b, h, g: (ctx_blk0 + b, h_ + h)),
                  pl.BlockSpec((ctx_len, HEAD_DIM), lambda b, h, g: (ctx_blk0 + b, 2 * h_ + h)),
                  pl.BlockSpec((1, WIN_ROWS, GRID_W, WIN_ROWS * GRID_W), lambda b, h, g: (h, 0, 0, 0))],
        out_specs=pl.BlockSpec((tq, HEAD_DIM), lambda b, h, g: (b * groups + g, h)),
        out_shape=jax.ShapeDtypeStruct((n_rows_out, D_MODEL), BF16),
        compiler_params=_params(("parallel", "parallel", "arbitrary")),
        name="na_attention",
    )(qkv, qkv, qkv, qkv, qkv, bias_table)


def _ctx_attn_kernel(q_ref, k_ref, v_ref, prev_ref, o_ref):
    del prev_ref
    scale = HEAD_DIM ** -0.5
    s = lax.dot_general(q_ref[...], k_ref[...], (((1,), (1,)), ((), ())), preferred_element_type=F32) * scale
    m = jnp.max(s, axis=-1, keepdims=True)
    p = jnp.exp(s - m)
    denom = jnp.sum(p, axis=-1, keepdims=True)
    o = jnp.dot(p.astype(BF16), v_ref[...], preferred_element_type=F32)
    o_ref[...] = (o / denom).astype(o_ref.dtype)


def _ctx_attention(qkv, attn_out, n_batch, seq, ctx_len):
    ctx_blk0 = (n_batch * seq) // ctx_len
    h_ = N_HEADS
    return pl.pallas_call(
        _ctx_attn_kernel,
        grid=(n_batch, h_),
        in_specs=[pl.BlockSpec((ctx_len, HEAD_DIM), lambda b, h: (ctx_blk0 + b, h)),
                  pl.BlockSpec((ctx_len, HEAD_DIM), lambda b, h: (ctx_blk0 + b, h_ + h)),
                  pl.BlockSpec((ctx_len, HEAD_DIM), lambda b, h: (ctx_blk0 + b, 2 * h_ + h)),
                  pl.BlockSpec(memory_space=pl.ANY)],
        out_specs=pl.BlockSpec((ctx_len, HEAD_DIM), lambda b, h: (ctx_blk0 + b, h)),
        out_shape=jax.ShapeDtypeStruct(attn_out.shape, attn_out.dtype),
        input_output_aliases={3: 0},
        compiler_params=_params(("parallel", "parallel")),
        name="ctx_attention",
    )(qkv, qkv, qkv, attn_out)


def _dwconv_kernel(x_ref, w_ref, b_ref, o_ref, xpad, *, seq):
    half = CONV_WIDTH // 2
    lead = 2 * SUBLANES
    ch = x_ref.shape[1]
    xpad[pl.ds(0, lead), :] = jnp.zeros((lead, ch), F32)
    xpad[pl.ds(lead + seq, lead), :] = jnp.zeros((lead, ch), F32)
    xpad[pl.ds(lead, seq), :] = x_ref[...]
    w = w_ref[...]
    bias = b_ref[...]
    win_rows = CONV_TT + 2 * lead

    def body(c, carry):
        t0 = pl.multiple_of(c * CONV_TT, CONV_TT)
        win = xpad[pl.ds(t0, win_rows), :]
        acc = jnp.broadcast_to(bias, (CONV_TT, ch))
        for p in range(SUBLANES):
            shifted = win if p == 0 else pltpu.roll(win, win_rows - p, axis=0)
            for k in range(CONV_WIDTH):
                o = lead - half + k
                if o % SUBLANES == p:
                    acc = acc + shifted[o - p:o - p + CONV_TT, :] * w[k:k + 1, :]
        o_ref[pl.ds(t0, CONV_TT), :] = acc
        return carry

    lax.fori_loop(0, seq // CONV_TT, body, 0)


def _dwconv(x, w_dw, b_dw, n_batch, seq):
    d = x.shape[1]
    nc = d // CONV_CH
    return pl.pallas_call(
        functools.partial(_dwconv_kernel, seq=seq),
        grid=(n_batch, nc),
        in_specs=[pl.BlockSpec((seq, CONV_CH), lambda b, c: (b, c)),
                  pl.BlockSpec((CONV_WIDTH, CONV_CH), lambda b, c: (0, c)),
                  pl.BlockSpec((1, CONV_CH), lambda b, c: (0, c))],
        out_specs=pl.BlockSpec((seq, CONV_CH), lambda b, c: (b, c)),
        out_shape=jax.ShapeDtypeStruct((n_batch * seq, d), F32),
        scratch_shapes=[pltpu.VMEM((seq + 4 * SUBLANES, CONV_CH), F32)],
        compiler_params=_params(("parallel", "parallel")),
        name="dwconv",
    )(x, w_dw, b_dw.reshape(1, d))


def _proj_ln_kernel(*refs, pre_ln, has_bias):
    it = iter(refs)
    a_ref = next(it)
    pre_g = next(it) if pre_ln else None
    pre_b = next(it) if pre_ln else None
    w_ref = next(it)
    bias_ref = next(it) if has_bias else None
    h_ref, gate_ref, lng_ref, lnb_ref, scf_ref, shf_ref, wr_ref = (next(it) for _ in range(7))
    hnew_ref, fin_ref, score_ref = (next(it) for _ in range(3))

    a = a_ref[...]
    if pre_ln:
        a = _silu(_layer_norm(a, pre_g[...], pre_b[...]))
    m = jnp.dot(a.astype(BF16), w_ref[...], preferred_element_type=F32)
    if has_bias:
        m = m + bias_ref[...]
    hn = _layer_norm(DEEPNORM_ALPHA * h_ref[...] + gate_ref[0] * m, lng_ref[...], lnb_ref[...])
    hnew_ref[...] = hn
    f = hn * (1.0 + scf_ref[0]) + shf_ref[0]
    fin_ref[...] = f
    score_ref[...] = jax.nn.sigmoid(jnp.dot(f.astype(BF16), wr_ref[...], preferred_element_type=F32))


def _proj_ln(a, w, bias, h, gate, ln_g, ln_b, sc_f, sh_f, w_router, n_rows, seq, pre_ln=None):
    d = D_MODEL
    tm = ROW_TILE
    grp = _group_of_tile(tm, seq)
    row = lambda i: (i, 0)
    const = lambda i: (0, 0)
    grp3 = lambda i: (grp(i), 0, 0)
    args, specs = [a], [pl.BlockSpec((tm, d), row)]
    if pre_ln is not None:
        args += [pre_ln[0].reshape(1, d), pre_ln[1].reshape(1, d)]
        specs += [pl.BlockSpec((1, d), const)] * 2
    args.append(w)
    specs.append(pl.BlockSpec((d, d), const))
    if bias is not None:
        args.append(bias.reshape(1, d))
        specs.append(pl.BlockSpec((1, d), const))
    args += [h, gate, ln_g.reshape(1, d), ln_b.reshape(1, d), sc_f, sh_f, w_router]
    specs += [pl.BlockSpec((tm, d), row), pl.BlockSpec((1, 1, d), grp3),
              pl.BlockSpec((1, d), const), pl.BlockSpec((1, d), const),
              pl.BlockSpec((1, 1, d), grp3), pl.BlockSpec((1, 1, d), grp3),
              pl.BlockSpec((d, N_EXPERTS), const)]
    return pl.pallas_call(
        functools.partial(_proj_ln_kernel, pre_ln=pre_ln is not None, has_bias=bias is not None),
        grid=(n_rows // tm,),
        in_specs=specs,
        out_specs=[pl.BlockSpec((tm, d), row), pl.BlockSpec((tm, d), row),
                   pl.BlockSpec((tm, N_EXPERTS), row)],
        out_shape=[jax.ShapeDtypeStruct((n_rows, d), F32),
                   jax.ShapeDtypeStruct((n_rows, d), F32),
                   jax.ShapeDtypeStruct((n_rows, N_EXPERTS), F32)],
        compiler_params=_params(("parallel",)),
        name="proj_ln",
    )(*args)


def _router_kernel(s_ref, b_ref, ek_ref, rk_ref, gk_ref, cnt_ref, carry):
    tm, e = s_ref.shape

    @pl.when(pl.program_id(0) == 0)
    def _():
        carry[...] = jnp.zeros_like(carry)

    s = s_ref[...]
    lane = lax.broadcasted_iota(jnp.int32, (tm, e), 1).astype(F32)
    slot = lax.broadcasted_iota(jnp.int32, (tm, 128), 1)
    work = s + b_ref[...]
    sel = jnp.zeros((tm, e), F32)
    e_acc = jnp.zeros((tm, 128), F32)
    g_acc = jnp.zeros((tm, 128), F32)
    g_sum = jnp.zeros((tm, 1), F32)
    picks = []
    for k in range(TOP_K):
        m = jnp.max(work, axis=-1, keepdims=True)
        idx = jnp.min(jnp.where(work == m, lane, float(e)), axis=-1, keepdims=True)
        onehot = lane == idx
        gk = jnp.sum(jnp.where(onehot, s, 0.0), axis=-1, keepdims=True)
        sel = jnp.where(onehot, 1.0, sel)
        work = jnp.where(onehot, -jnp.inf, work)
        e_acc = jnp.where(slot == k, idx, e_acc)
        g_acc = jnp.where(slot == k, gk, g_acc)
        g_sum = g_sum + gk
        picks.append(idx)

    r_i = lax.broadcasted_iota(jnp.int32, (tm, tm), 0)
    c_i = lax.broadcasted_iota(jnp.int32, (tm, tm), 1)
    tri = jnp.where(c_i < r_i, 1.0, 0.0).astype(BF16)
    rank = jnp.dot(tri, sel.astype(BF16), preferred_element_type=F32) + carry[...]
    r_acc = jnp.zeros((tm, 128), F32)
    for k in range(TOP_K):
        rk = jnp.sum(jnp.where(lane == picks[k], rank, 0.0), axis=-1, keepdims=True)
        r_acc = jnp.where(slot == k, rk, r_acc)

    carry[...] = carry[...] + jnp.sum(sel, axis=0, keepdims=True)
    ek_ref[...] = e_acc[:, :TOP_K].astype(jnp.int32)
    rk_ref[...] = r_acc[:, :TOP_K].astype(jnp.int32)
    gk_ref[...] = (g_acc / g_sum * ROUTED_SCALE)[:, :TOP_K]
    cnt_ref[...] = carry[...].astype(jnp.int32)


def _router(scores, router_bias, n_rows):
    e = N_EXPERTS
    tm = ROW_TILE
    row = lambda i: (i, 0)
    return pl.pallas_call(
        _router_kernel,
        grid=(n_rows // tm,),
        in_specs=[pl.BlockSpec((tm, e), row), pl.BlockSpec((1, e), lambda i: (0, 0))],
        out_specs=[pl.BlockSpec((tm, TOP_K), row), pl.BlockSpec((tm, TOP_K), row), pl.BlockSpec((tm, TOP_K), row),
                   pl.BlockSpec((1, e), lambda i: (0, 0))],
        out_shape=[jax.ShapeDtypeStruct((n_rows, TOP_K), jnp.int32), jax.ShapeDtypeStruct((n_rows, TOP_K), jnp.int32),
                   jax.ShapeDtypeStruct((n_rows, TOP_K), F32), jax.ShapeDtypeStruct((1, e), jnp.int32)],
        scratch_shapes=[pltpu.VMEM((1, e), F32)],
        compiler_params=_params(("arbitrary",)),
        name="router",
    )(scores, router_bias.reshape(1, e).astype(F32))


def _layout(e_k, r_k, counts, tm_e):
    e = N_EXPERTS
    n_assign = e_k.size
    counts = counts.reshape(e)
    padded = (counts + tm_e - 1) // tm_e * tm_e
    pends = jnp.cumsum(padded)
    pstarts = pends - padded
    dest = (pstarts[e_k] + r_k).reshape(n_assign).astype(jnp.int32)
    n_blocks = (n_assign + e * (tm_e - 1) + tm_e - 1) // tm_e
    blk_e = jnp.minimum(jnp.searchsorted(pends, jnp.arange(n_blocks, dtype=jnp.int32) * tm_e, side='right'),
                        e - 1).astype(jnp.int32)
    n_valid = (pends[-1] // tm_e).astype(jnp.int32).reshape(1)
    pad_info = jnp.concatenate([pstarts + counts, padded - counts]).astype(jnp.int32)
    return dest, blk_e, n_valid, pad_info, n_blocks


def _dispatch_kernel(pad_ref, dest_ref, f_ref, xs_ref, zbuf, sem, zsem):
    tm = f_ref.shape[0]
    pad_bits = EXPERT_TILE.bit_length() - 1

    def zero_copies(e_idx):
        start = pad_ref[e_idx]
        pad = pad_ref[N_EXPERTS + e_idx]
        n_single = (-start) & (SUBLANES - 1)
        out = []
        for j in range(SUBLANES - 1):
            out.append((j < n_single, pltpu.make_async_copy(
                zbuf.at[pl.ds(0, 1)], xs_ref.at[pl.ds(start + j, 1)], zsem)))
        base = start + n_single
        rem = pad - n_single
        for b in range(SUBLANES.bit_length() - 1, pad_bits):
            off = pl.multiple_of(base + (rem & ((1 << b) - 1)), SUBLANES)
            out.append((((rem >> b) & 1) == 1, pltpu.make_async_copy(
                zbuf.at[pl.ds(0, 1 << b)], xs_ref.at[pl.ds(off, 1 << b)], zsem)))
        return out

    @pl.when(pl.program_id(0) == 0)
    def _():
        zbuf[...] = jnp.zeros_like(zbuf)

        def start_e(e_idx, carry):
            for cond, cp in zero_copies(e_idx):
                pl.when(cond)(cp.start)
            return carry

        def wait_e(e_idx, carry):
            for cond, cp in zero_copies(e_idx):
                pl.when(cond)(cp.wait)
            return carry

        lax.fori_loop(0, N_EXPERTS, start_e, 0)
        lax.fori_loop(0, N_EXPERTS, wait_e, 0)

    def tok(t, carry):
        for k in range(TOP_K):
            d = dest_ref[t * TOP_K + k]
            pltpu.make_async_copy(f_ref.at[pl.ds(t, 1)], xs_ref.at[pl.ds(d, 1)], sem).start()
        return carry

    lax.fori_loop(0, tm, tok, 0)
    for k in range(TOP_K):
        pltpu.make_async_copy(f_ref, xs_ref.at[pl.ds(0, tm)], sem).wait()


def _dispatch(f_in, dest, pad_info, n_rows, n_blocks):
    d = f_in.shape[1]
    tm = ROW_TILE
    grid_spec = pltpu.PrefetchScalarGridSpec(
        num_scalar_prefetch=1,
        grid=(n_rows // tm,),
        in_specs=[pl.BlockSpec((tm * TOP_K,), lambda i, p: (i,), memory_space=pltpu.SMEM),
                  pl.BlockSpec((tm, d), lambda i, p: (i, 0))],
        out_specs=pl.BlockSpec(memory_space=pl.ANY),
        scratch_shapes=[pltpu.VMEM((EXPERT_TILE // 2, d), F32), pltpu.SemaphoreType.DMA, pltpu.SemaphoreType.DMA],
    )
    return pl.pallas_call(
        _dispatch_kernel,
        grid_spec=grid_spec,
        out_shape=jax.ShapeDtypeStruct((n_blocks * EXPERT_TILE, d), F32),
        compiler_params=_params(("arbitrary",)),
        name="dispatch",
    )(pad_info, dest, f_in)


def _expert_kernel(be_ref, nv_ref, x_ref, wg_ref, wu_ref, wd_ref, y_ref, wgb, wub, wdb):
    i = pl.program_id(0)
    valid = i < nv_ref[0]
    changed = jnp.logical_or(i == 0, be_ref[i] != be_ref[jnp.maximum(i - 1, 0)])
    d, de = wg_ref.shape[2], wg_ref.shape[3]
    n_chunks = 8

    @pl.when(jnp.logical_and(valid, changed))
    def _():
        def cast_chunk(c, carry):
            r0 = pl.multiple_of(c * (d // n_chunks), d // n_chunks)
            wgb[pl.ds(r0, d // n_chunks), :] = wg_ref[0, 0, pl.ds(r0, d // n_chunks), :].astype(BF16)
            wub[pl.ds(r0, d // n_chunks), :] = wu_ref[0, 0, pl.ds(r0, d // n_chunks), :].astype(BF16)
            r1 = pl.multiple_of(c * (de // n_chunks), de // n_chunks)
            wdb[pl.ds(r1, de // n_chunks), :] = wd_ref[0, 0, pl.ds(r1, de // n_chunks), :].astype(BF16)
            return carry

        lax.fori_loop(0, n_chunks, cast_chunk, 0)

    @pl.when(valid)
    def _():
        x = x_ref[...].astype(BF16)
        gv = jnp.dot(x, wgb[...], preferred_element_type=F32)
        uv = jnp.dot(x, wub[...], preferred_element_type=F32)
        act = (_silu(gv) * uv).astype(BF16)
        y_ref[...] = jnp.dot(act, wdb[...], preferred_element_type=F32)


def _expert_ffn(xs, blk_e, n_valid, w_gate, w_up, w_down, layer):
    n_rows, d = xs.shape
    de = w_gate.shape[3]
    tm = EXPERT_TILE
    n_blocks = n_rows // tm

    def xmap(i, be, nv):
        return (jnp.minimum(i, nv[0] - 1), 0)

    def wmap(i, be, nv):
        return (layer, be[jnp.minimum(i, nv[0] - 1)], 0, 0)

    grid_spec = pltpu.PrefetchScalarGridSpec(
        num_scalar_prefetch=2,
        grid=(n_blocks,),
        in_specs=[pl.BlockSpec((tm, d), xmap),
                  pl.BlockSpec((1, 1, d, de), wmap),
                  pl.BlockSpec((1, 1, d, de), wmap),
                  pl.BlockSpec((1, 1, de, d), wmap)],
        out_specs=pl.BlockSpec((tm, d), xmap),
        scratch_shapes=[pltpu.VMEM((d, de), BF16), pltpu.VMEM((d, de), BF16), pltpu.VMEM((de, d), BF16)],
    )
    return pl.pallas_call(
        _expert_kernel,
        grid_spec=grid_spec,
        out_shape=jax.ShapeDtypeStruct((n_rows, d), F32),
        compiler_params=_params(("arbitrary",)),
        name="expert_ffn",
    )(blk_e, n_valid, xs, w_gate, w_up, w_down)


def _ffn_out_kernel(dcur_ref, dnext_ref, f_ref, gk_ref, wg_ref, wu_ref, wd_ref, h_ref, gate_ref, lng_ref, lnb_ref,
                    y_ref, o_ref, ybuf, sems):
    i = pl.program_id(0)
    n_steps = pl.num_programs(0)
    tm = f_ref.shape[0]
    slot = lax.rem(i, 2)

    def start_gather(dref, s):
        def tok(t, carry):
            for k in range(TOP_K):
                row = dref[t * TOP_K + k]
                pltpu.make_async_copy(y_ref.at[pl.ds(row, 1)], ybuf.at[s, k, pl.ds(t, 1)], sems.at[s]).start()
            return carry

        lax.fori_loop(0, tm, tok, 0)

    @pl.when(i == 0)
    def _():
        start_gather(dcur_ref, slot)

    @pl.when(i + 1 < n_steps)
    def _():
        start_gather(dnext_ref, 1 - slot)

    x = f_ref[...].astype(BF16)
    gv = jnp.dot(x, wg_ref[...], preferred_element_type=F32)
    uv = jnp.dot(x, wu_ref[...], preferred_element_type=F32)
    act = (_silu(gv) * uv).astype(BF16)
    f = jnp.dot(act, wd_ref[...], preferred_element_type=F32)

    gk = gk_ref[...]
    for k in range(TOP_K):
        pltpu.make_async_copy(y_ref.at[pl.ds(0, tm)], ybuf.at[slot, k], sems.at[slot]).wait()
    for k in range(TOP_K):
        f = f + gk[:, k:k + 1] * ybuf[slot, k]
    o_ref[...] = _layer_norm(DEEPNORM_ALPHA * h_ref[...] + gate_ref[0] * f, lng_ref[...], lnb_ref[...])


def _ffn_out(f_in, y, dest, gates, ws_gate, ws_up, ws_down, h, gate, ln_g, ln_b, n_rows, seq):
    d = D_MODEL
    ds_ = ws_gate.shape[1]
    tm = COMBINE_TILE
    n_steps = n_rows // tm
    grp = _group_of_tile(tm, seq)
    row = lambda i: (i, 0)
    const = lambda i: (0, 0)
    return pl.pallas_call(
        _ffn_out_kernel,
        grid=(n_steps,),
        in_specs=[pl.BlockSpec((tm * TOP_K,), lambda i: (i,), memory_space=pltpu.SMEM),
                  pl.BlockSpec((tm * TOP_K,), lambda i: (jnp.minimum(i + 1, n_steps - 1),), memory_space=pltpu.SMEM),
                  pl.BlockSpec((tm, d), row), pl.BlockSpec((tm, TOP_K), row),
                  pl.BlockSpec((d, ds_), const), pl.BlockSpec((d, ds_), const), pl.BlockSpec((ds_, d), const),
                  pl.BlockSpec((tm, d), row), pl.BlockSpec((1, 1, d), lambda i: (grp(i), 0, 0)),
                  pl.BlockSpec((1, d), const), pl.BlockSpec((1, d), const),
                  pl.BlockSpec(memory_space=pl.ANY)],
        out_specs=pl.BlockSpec((tm, d), row),
        out_shape=jax.ShapeDtypeStruct((n_rows, d), F32),
        scratch_shapes=[pltpu.VMEM((2, TOP_K, tm, d), F32), pltpu.SemaphoreType.DMA((2,))],
        compiler_params=_params(("arbitrary",)),
        name="ffn_out",
    )(dest, dest, f_in, gates, ws_gate, ws_up, ws_down, h, gate, ln_g.reshape(1, d), ln_b.reshape(1, d), y)


def _moe_and_norm(h, f_in, scores, gate_f, router_bias, we_gate, we_up, we_down, ws_gate, ws_up, ws_down,
                  ln_g, ln_b, n_rows, seq, layer):
    e_k, r_k, gates, counts = _router(scores, router_bias, n_rows)
    dest, blk_e, n_valid, pad_info, n_blocks = _layout(e_k, r_k, counts, EXPERT_TILE)
    xs = _dispatch(f_in, dest, pad_info, n_rows, n_blocks)
    y = _expert_ffn(xs, blk_e, n_valid, we_gate, we_up, we_down, layer)
    return _ffn_out(f_in, y, dest, gates, ws_gate.astype(BF16), ws_up.astype(BF16), ws_down.astype(BF16),
                    h, gate_f, ln_g, ln_b, n_rows, seq)


def kernel(x, c, ctx, c_ctx, mod_w, mod_b, ln_mix_g, ln_mix_b, ln_ffn_g, ln_ffn_b, na_w_qkv, na_rpb, na_w_o,
           cv_w_pw1, cv_b_pw1, cv_w_dw, cv_b_dw, cv_ln_g, cv_ln_b, cv_w_pw2, cv_b_pw2, moe_w_router,
           moe_router_bias, moe_w_gate, moe_w_up, moe_w_down, sh_w_gate, sh_w_up, sh_w_down):
    n_batch, seq, d = x.shape
    ctx_len = ctx.shape[1]
    n_lat = n_batch * seq
    n_tok = n_lat + n_batch * ctx_len

    cvec = jnp.zeros((8, d), F32).at[:n_batch].set(c).at[n_batch].set(c_ctx)
    mod_all = _mod_matmul(cvec, mod_w, mod_b).reshape(DEPTH, 8, 6, d)[:, :n_batch + 1]
    mods = [[mod_all[i, :, p, :].reshape(n_batch + 1, 1, d) for p in range(6)] for i in range(DEPTH)]

    h = jnp.concatenate([x.reshape(n_lat, d), ctx.reshape(n_batch * ctx_len, d)], axis=0)

    sh_a, sc_a, g_a, sh_f, sc_f, g_f = mods[0]
    qkv = _mod_matmul_qkv(h, sc_a, sh_a, na_w_qkv[0].astype(BF16), n_tok, seq)
    attn = _na_attention(qkv, _na_bias_table(na_rpb[0]), n_batch, seq, ctx_len, n_tok)
    attn = _ctx_attention(qkv, attn, n_batch, seq, ctx_len)
    h, f_in, scores = _proj_ln(attn, na_w_o[0].astype(BF16), None, h, g_a, ln_mix_g[0], ln_mix_b[0],
                               sc_f, sh_f, moe_w_router[0].astype(BF16), n_tok, seq)
    h = _moe_and_norm(h, f_in, scores, g_f, moe_router_bias[0], moe_w_gate, moe_w_up, moe_w_down,
                      sh_w_gate[0], sh_w_up[0], sh_w_down[0], ln_ffn_g[0], ln_ffn_b[0], n_tok, seq, 0)

    sh_a, sc_a, g_a, sh_f, sc_f, g_f = mods[1]
    glu = _mod_matmul_glu(h, sc_a, sh_a, cv_w_pw1[0].astype(BF16), cv_b_pw1[0], n_lat, seq)
    conv = _dwconv(glu, cv_w_dw[0], cv_b_dw[0], n_batch, seq)
    h, f_in, scores = _proj_ln(conv, cv_w_pw2[0].astype(BF16), cv_b_pw2[0], h, g_a, ln_mix_g[1], ln_mix_b[1],
                               sc_f, sh_f, moe_w_router[1].astype(BF16), n_lat, seq,
                               pre_ln=(cv_ln_g[0], cv_ln_b[0]))
    h = _moe_and_norm(h, f_in, scores, g_f, moe_router_bias[1], moe_w_gate, moe_w_up, moe_w_down,
                      sh_w_gate[1], sh_w_up[1], sh_w_down[1], ln_ffn_g[1], ln_ffn_b[1], n_lat, seq, 1)
    return h.reshape(n_batch, seq, d)
```

```python
import functools

import jax
import jax.numpy as jnp
from jax import lax
from jax.experimental import pallas as pl
from jax.experimental.pallas import tpu as pltpu

D_MODEL = 2048
GRID_W = 64
N_HEADS = 16
HEAD_DIM = D_MODEL // N_HEADS
WIN_ROWS = 8
WIN_COLS = 16
CONV_WIDTH = 31
N_EXPERTS = 64
TOP_K = 8
ROUTED_SCALE = 2.5
LN_EPS = 1e-6
SUBLANES = 8
NEG_INF = -1e30
DEPTH = 2
DEEPNORM_ALPHA = (2 * DEPTH) ** 0.25

F32 = jnp.float32
BF16 = jnp.bfloat16

VMEM_LIMIT_BYTES = 56 * 1024 * 1024
ROW_TILE = 256
MM_ROW_TILE = 512
EXPERT_TILE = 256
COMBINE_TILE = 128
ROWS_PER_STEP = 8
CONV_CH = 256
CONV_TT = 64


def _params(sem):
    return pltpu.CompilerParams(dimension_semantics=sem, vmem_limit_bytes=VMEM_LIMIT_BYTES)


def _layer_norm(z, g, b):
    mu = jnp.mean(z, axis=-1, keepdims=True)
    zc = z - mu
    var = jnp.mean(zc * zc, axis=-1, keepdims=True)
    return zc * lax.rsqrt(var + LN_EPS) * g + b


def _silu(x):
    return x * jax.nn.sigmoid(x)


def _pack_bf16_pair(x):
    n = x.shape[1] // 2
    lo = pltpu.bitcast(x[:, :n].astype(BF16).astype(F32), jnp.uint32)
    hi = pltpu.bitcast(x[:, n:].astype(BF16).astype(F32), jnp.uint32)
    return (lo >> 16) | (hi & jnp.uint32(0xFFFF0000))


def _unpack_bf16_pair(w):
    lo = pltpu.bitcast(w << 16, F32)
    hi = pltpu.bitcast(w & jnp.uint32(0xFFFF0000), F32)
    return lo, hi


def _mod_kernel(c_ref, w_ref, b_ref, o_ref):
    a = _silu(c_ref[...]).astype(BF16)
    o_ref[0] = jnp.dot(a, w_ref[0].astype(BF16), preferred_element_type=F32) + b_ref[0]


def _mod_matmul(cvec, w, b):
    m, k = cvec.shape
    layers, _, n = w.shape
    tn = 1024
    return pl.pallas_call(
        _mod_kernel,
        grid=(layers, n // tn),
        in_specs=[pl.BlockSpec((m, k), lambda l, j: (0, 0)),
                  pl.BlockSpec((1, k, tn), lambda l, j: (l, 0, j)),
                  pl.BlockSpec((1, 1, tn), lambda l, j: (l, 0, j))],
        out_specs=pl.BlockSpec((1, m, tn), lambda l, j: (l, 0, j)),
        out_shape=jax.ShapeDtypeStruct((layers, m, n), F32),
        compiler_params=_params(("parallel", "arbitrary")),
        name="mod_matmul",
    )(cvec, w, b.reshape(layers, 1, n))


def _modmm_kernel(h_ref, sc_ref, sh_ref, w_ref, o_ref, a_scr):
    @pl.when(pl.program_id(1) == 0)
    def _():
        a_scr[...] = (h_ref[...] * (1.0 + sc_ref[0]) + sh_ref[0]).astype(BF16)

    o_ref[...] = jnp.dot(a_scr[...], w_ref[...], preferred_element_type=F32).astype(o_ref.dtype)


def _modglu_kernel(h_ref, sc_ref, sh_ref, wa_ref, wg_ref, ba_ref, bg_ref, o_ref, a_scr):
    @pl.when(pl.program_id(1) == 0)
    def _():
        a_scr[...] = (h_ref[...] * (1.0 + sc_ref[0]) + sh_ref[0]).astype(BF16)

    a = a_scr[...]
    va = jnp.dot(a, wa_ref[...], preferred_element_type=F32) + ba_ref[...]
    vg = jnp.dot(a, wg_ref[...], preferred_element_type=F32) + bg_ref[...]
    o_ref[...] = va * jax.nn.sigmoid(vg)


def _group_of_tile(tile_rows, seq):
    return lambda i: (i * tile_rows) // seq


def _mod_matmul_qkv(h, sc, sh, w, n_rows, seq):
    k = h.shape[1]
    n = w.shape[1]
    tm, tn = MM_ROW_TILE, 2048
    grp = _group_of_tile(tm, seq)
    return pl.pallas_call(
        _modmm_kernel,
        grid=(n_rows // tm, n // tn),
        in_specs=[pl.BlockSpec((tm, k), lambda i, j: (i, 0)),
                  pl.BlockSpec((1, 1, k), lambda i, j: (grp(i), 0, 0)),
                  pl.BlockSpec((1, 1, k), lambda i, j: (grp(i), 0, 0)),
                  pl.BlockSpec((k, tn), lambda i, j: (0, j))],
        out_specs=pl.BlockSpec((tm, tn), lambda i, j: (i, j)),
        out_shape=jax.ShapeDtypeStruct((n_rows, n), BF16),
        scratch_shapes=[pltpu.VMEM((tm, k), BF16)],
        compiler_params=_params(("parallel", "arbitrary")),
        name="mod_qkv",
    )(h, sc, sh, w)


def _mod_matmul_glu(h, sc, sh, w, b, n_rows, seq):
    k = h.shape[1]
    n = w.shape[1] // 2
    tm, tn = MM_ROW_TILE, 1024
    grp = _group_of_tile(tm, seq)
    nj = n // tn
    b2 = b.reshape(1, 2 * n)
    return pl.pallas_call(
        _modglu_kernel,
        grid=(n_rows // tm, nj),
        in_specs=[pl.BlockSpec((tm, k), lambda i, j: (i, 0)),
                  pl.BlockSpec((1, 1, k), lambda i, j: (grp(i), 0, 0)),
                  pl.BlockSpec((1, 1, k), lambda i, j: (grp(i), 0, 0)),
                  pl.BlockSpec((k, tn), lambda i, j: (0, j)),
                  pl.BlockSpec((k, tn), lambda i, j: (0, j + nj)),
                  pl.BlockSpec((1, tn), lambda i, j: (0, j)),
                  pl.BlockSpec((1, tn), lambda i, j: (0, j + nj))],
        out_specs=pl.BlockSpec((tm, tn), lambda i, j: (i, j)),
        out_shape=jax.ShapeDtypeStruct((n_rows, n), F32),
        scratch_shapes=[pltpu.VMEM((tm, k), BF16)],
        compiler_params=_params(("parallel", "arbitrary")),
        name="mod_pw1_glu",
    )(h, sc, sh, w, w, b2, b2)


def _na_kernel(q_ref, k_ref, v_ref, kc_ref, vc_ref, bias_ref, o_ref, *, rows):
    g = pl.program_id(2)
    scale = HEAD_DIM ** -0.5
    kc = kc_ref[...]
    vc = vc_ref[...]
    n_loc = WIN_ROWS * GRID_W
    contract_last = (((1,), (1,)), ((), ()))

    def body(rr, carry):
        r = g * ROWS_PER_STEP + rr
        rs = jnp.clip(r - WIN_ROWS // 2, 0, rows - WIN_ROWS)
        off = r - rs
        q = q_ref[pl.ds(pl.multiple_of(rr * GRID_W, GRID_W), GRID_W), :]
        k_start = pl.multiple_of(rs * GRID_W, GRID_W)
        ks = k_ref[pl.ds(k_start, n_loc), :]
        vs = v_ref[pl.ds(k_start, n_loc), :]
        s_loc = lax.dot_general(q, ks, contract_last, preferred_element_type=F32) * scale + bias_ref[0, off]
        s_ctx = lax.dot_general(q, kc, contract_last, preferred_element_type=F32) * scale
        m = jnp.maximum(jnp.max(s_loc, axis=-1, keepdims=True), jnp.max(s_ctx, axis=-1, keepdims=True))
        p_loc = jnp.exp(s_loc - m)
        p_ctx = jnp.exp(s_ctx - m)
        denom = jnp.sum(p_loc, axis=-1, keepdims=True) + jnp.sum(p_ctx, axis=-1, keepdims=True)
        o = (jnp.dot(p_loc.astype(BF16), vs, preferred_element_type=F32)
             + jnp.dot(p_ctx.astype(BF16), vc, preferred_element_type=F32))
        o_ref[pl.ds(pl.multiple_of(rr * GRID_W, GRID_W), GRID_W), :] = (o / denom).astype(o_ref.dtype)
        return carry

    lax.fori_loop(0, ROWS_PER_STEP, body, 0)


def _na_bias_table(rpb):
    col = jnp.arange(GRID_W)
    cstart = jnp.clip(col - WIN_COLS // 2, 0, GRID_W - WIN_COLS)
    col_ok = (col[None, :] >= cstart[:, None]) & (col[None, :] < cstart[:, None] + WIN_COLS)
    cidx = jnp.clip(col[None, :] - col[:, None] + WIN_COLS - 1, 0, 2 * WIN_COLS - 2)
    onehot = (cidx[:, :, None] == jnp.arange(2 * WIN_COLS - 1)).astype(F32)
    t = jnp.einsum('hrc,qkc->hrqk', rpb.astype(F32), onehot, precision=lax.Precision.HIGHEST)
    t = jnp.where(col_ok[None, None], t, NEG_INF)
    t = jnp.stack([t[:, WIN_ROWS - 1 - off:2 * WIN_ROWS - 1 - off] for off in range(WIN_ROWS)], axis=1)
    t = t.transpose(0, 1, 3, 2, 4)
    return t.reshape(rpb.shape[0], WIN_ROWS, GRID_W, WIN_ROWS * GRID_W)


def _na_attention(qkv, bias_table, n_batch, seq, ctx_len, n_rows_out):
    rows = seq // GRID_W
    groups = rows // ROWS_PER_STEP
    tq = ROWS_PER_STEP * GRID_W
    ctx_blk0 = (n_batch * seq) // ctx_len
    h_ = N_HEADS
    return pl.pallas_call(
        functools.partial(_na_kernel, rows=rows),
        grid=(n_batch, h_, groups),
        in_specs=[pl.BlockSpec((tq, HEAD_DIM), lambda b, h, g: (b * groups + g, h)),
                  pl.BlockSpec((seq, HEAD_DIM), lambda b, h, g: (b, h_ + h)),
                  pl.BlockSpec((seq, HEAD_DIM), lambda b, h, g: (b, 2 * h_ + h)),
                  pl.BlockSpec((ctx_len, HEAD_DIM), lambda b, h, g: (ctx_blk0 + b, h_ + h)),
                  pl.BlockSpec((ctx_len, HEAD_DIM), lambda b, h, g: (ctx_blk0 + b, 2 * h_ + h)),
                  pl.BlockSpec((1, WIN_ROWS, GRID_W, WIN_ROWS * GRID_W), lambda b, h, g: (h, 0, 0, 0))],
        out_specs=pl.BlockSpec((tq, HEAD_DIM), lambda b, h, g: (b * groups + g, h)),
        out_shape=jax.ShapeDtypeStruct((n_rows_out, D_MODEL), BF16),
        compiler_params=_params(("parallel", "parallel", "arbitrary")),
        name="na_attention",
    )(qkv, qkv, qkv, qkv, qkv, bias_table)


def _ctx_attn_kernel(q_ref, k_ref, v_ref, prev_ref, o_ref):
    del prev_ref
    scale = HEAD_DIM ** -0.5
    s = lax.dot_general(q_ref[...], k_ref[...], (((1,), (1,)), ((), ())), preferred_element_type=F32) * scale
    m = jnp.max(s, axis=-1, keepdims=True)
    p = jnp.exp(s - m)
    denom = jnp.sum(p, axis=-1, keepdims=True)
    o = jnp.dot(p.astype(BF16), v_ref[...], preferred_element_type=F32)
    o_ref[...] = (o / denom).astype(o_ref.dtype)


def _ctx_attention(qkv, attn_out, n_batch, seq, ctx_len):
    ctx_blk0 = (n_batch * seq) // ctx_len
    h_ = N_HEADS
    return pl.pallas_call(
        _ctx_attn_kernel,
        grid=(n_batch, h_),
        in_specs=[pl.BlockSpec((ctx_len, HEAD_DIM), lambda b, h: (ctx_blk0 + b, h)),
                  pl.BlockSpec((ctx_len, HEAD_DIM), lambda b, h: (ctx_blk0 + b, h_ + h)),
                  pl.BlockSpec((ctx_len, HEAD_DIM), lambda b, h: (ctx_blk0 + b, 2 * h_ + h)),
                  pl.BlockSpec(memory_space=pl.ANY)],
        out_specs=pl.BlockSpec((ctx_len, HEAD_DIM), lambda b, h: (ctx_blk0 + b, h)),
        out_shape=jax.ShapeDtypeStruct(attn_out.shape, attn_out.dtype),
        input_output_aliases={3: 0},
        compiler_params=_params(("parallel", "parallel")),
        name="ctx_attention",
    )(qkv, qkv, qkv, attn_out)


def _dwconv_kernel(x_ref, w_ref, b_ref, o_ref, xpad, *, seq):
    half = CONV_WIDTH // 2
    lead = 2 * SUBLANES
    ch = x_ref.shape[1]
    xpad[pl.ds(0, lead), :] = jnp.zeros((lead, ch), F32)
    xpad[pl.ds(lead + seq, lead), :] = jnp.zeros((lead, ch), F32)
    xpad[pl.ds(lead, seq), :] = x_ref[...]
    w = w_ref[...]
    bias = b_ref[...]
    win_rows = CONV_TT + 2 * lead

    def body(c, carry):
        t0 = pl.multiple_of(c * CONV_TT, CONV_TT)
        win = xpad[pl.ds(t0, win_rows), :]
        acc = jnp.broadcast_to(bias, (CONV_TT, ch))
        for p in range(SUBLANES):
            shifted = win if p == 0 else pltpu.roll(win, win_rows - p, axis=0)
            for k in range(CONV_WIDTH):
                o = lead - half + k
                if o % SUBLANES == p:
                    acc = acc + shifted[o - p:o - p + CONV_TT, :] * w[k:k + 1, :]
        o_ref[pl.ds(t0, CONV_TT), :] = acc
        return carry

    lax.fori_loop(0, seq // CONV_TT, body, 0)


def _dwconv(x, w_dw, b_dw, n_batch, seq):
    d = x.shape[1]
    nc = d // CONV_CH
    return pl.pallas_call(
        functools.partial(_dwconv_kernel, seq=seq),
        grid=(n_batch, nc),
        in_specs=[pl.BlockSpec((seq, CONV_CH), lambda b, c: (b, c)),
                  pl.BlockSpec((CONV_WIDTH, CONV_CH), lambda b, c: (0, c)),
                  pl.BlockSpec((1, CONV_CH), lambda b, c: (0, c))],
        out_specs=pl.BlockSpec((seq, CONV_CH), lambda b, c: (b, c)),
        out_shape=jax.ShapeDtypeStruct((n_batch * seq, d), F32),
        scratch_shapes=[pltpu.VMEM((seq + 4 * SUBLANES, CONV_CH), F32)],
        compiler_params=_params(("parallel", "parallel")),
        name="dwconv",
    )(x, w_dw, b_dw.reshape(1, d))


def _proj_ln_kernel(*refs, pre_ln, has_bias):
    it = iter(refs)
    a_ref = next(it)
    pre_g = next(it) if pre_ln else None
    pre_b = next(it) if pre_ln else None
    w_ref = next(it)
    bias_ref = next(it) if has_bias else None
    h_ref, gate_ref, lng_ref, lnb_ref, scf_ref, shf_ref, wr_ref = (next(it) for _ in range(7))
    hnew_ref, fin_ref, score_ref = (next(it) for _ in range(3))

    a = a_ref[...]
    if pre_ln:
        a = _silu(_layer_norm(a, pre_g[...], pre_b[...]))
    m = jnp.dot(a.astype(BF16), w_ref[...], preferred_element_type=F32)
    if has_bias:
        m = m + bias_ref[...]
    hn = _layer_norm(DEEPNORM_ALPHA * h_ref[...] + gate_ref[0] * m, lng_ref[...], lnb_ref[...])
    hnew_ref[...] = hn
    f = hn * (1.0 + scf_ref[0]) + shf_ref[0]
    fin_ref[...] = _pack_bf16_pair(f)
    score_ref[...] = jax.nn.sigmoid(jnp.dot(f.astype(BF16), wr_ref[...], preferred_element_type=F32))


def _proj_ln(a, w, bias, h, gate, ln_g, ln_b, sc_f, sh_f, w_router, n_rows, seq, pre_ln=None):
    d = D_MODEL
    tm = ROW_TILE
    grp = _group_of_tile(tm, seq)
    row = lambda i: (i, 0)
    const = lambda i: (0, 0)
    grp3 = lambda i: (grp(i), 0, 0)
    args, specs = [a], [pl.BlockSpec((tm, d), row)]
    if pre_ln is not None:
        args += [pre_ln[0].reshape(1, d), pre_ln[1].reshape(1, d)]
        specs += [pl.BlockSpec((1, d), const)] * 2
    args.append(w)
    specs.append(pl.BlockSpec((d, d), const))
    if bias is not None:
        args.append(bias.reshape(1, d))
        specs.append(pl.BlockSpec((1, d), const))
    args += [h, gate, ln_g.reshape(1, d), ln_b.reshape(1, d), sc_f, sh_f, w_router]
    specs += [pl.BlockSpec((tm, d), row), pl.BlockSpec((1, 1, d), grp3),
              pl.BlockSpec((1, d), const), pl.BlockSpec((1, d), const),
              pl.BlockSpec((1, 1, d), grp3), pl.BlockSpec((1, 1, d), grp3),
              pl.BlockSpec((d, N_EXPERTS), const)]
    return pl.pallas_call(
        functools.partial(_proj_ln_kernel, pre_ln=pre_ln is not None, has_bias=bias is not None),
        grid=(n_rows // tm,),
        in_specs=specs,
        out_specs=[pl.BlockSpec((tm, d), row), pl.BlockSpec((tm, d // 2), row),
                   pl.BlockSpec((tm, N_EXPERTS), row)],
        out_shape=[jax.ShapeDtypeStruct((n_rows, d), F32),
                   jax.ShapeDtypeStruct((n_rows, d // 2), jnp.uint32),
                   jax.ShapeDtypeStruct((n_rows, N_EXPERTS), F32)],
        compiler_params=_params(("parallel",)),
        name="proj_ln",
    )(*args)


def _router_kernel(s_ref, b_ref, ek_ref, rk_ref, gk_ref, cnt_ref, carry):
    tm, e = s_ref.shape

    @pl.when(pl.program_id(0) == 0)
    def _():
        carry[...] = jnp.zeros_like(carry)

    s = s_ref[...]
    lane = lax.broadcasted_iota(jnp.int32, (tm, e), 1).astype(F32)
    slot = lax.broadcasted_iota(jnp.int32, (tm, 128), 1)
    work = s + b_ref[...]
    sel = jnp.zeros((tm, e), F32)
    e_acc = jnp.zeros((tm, 128), F32)
    g_acc = jnp.zeros((tm, 128), F32)
    g_sum = jnp.zeros((tm, 1), F32)
    picks = []
    for k in range(TOP_K):
        m = jnp.max(work, axis=-1, keepdims=True)
        idx = jnp.min(jnp.where(work == m, lane, float(e)), axis=-1, keepdims=True)
        onehot = lane == idx
        gk = jnp.sum(jnp.where(onehot, s, 0.0), axis=-1, keepdims=True)
        sel = jnp.where(onehot, 1.0, sel)
        work = jnp.where(onehot, -jnp.inf, work)
        e_acc = jnp.where(slot == k, idx, e_acc)
        g_acc = jnp.where(slot == k, gk, g_acc)
        g_sum = g_sum + gk
        picks.append(idx)

    r_i = lax.broadcasted_iota(jnp.int32, (tm, tm), 0)
    c_i = lax.broadcasted_iota(jnp.int32, (tm, tm), 1)
    tri = jnp.where(c_i < r_i, 1.0, 0.0).astype(BF16)
    rank = jnp.dot(tri, sel.astype(BF16), preferred_element_type=F32) + carry[...]
    r_acc = jnp.zeros((tm, 128), F32)
    for k in range(TOP_K):
        rk = jnp.sum(jnp.where(lane == picks[k], rank, 0.0), axis=-1, keepdims=True)
        r_acc = jnp.where(slot == k, rk, r_acc)

    carry[...] = carry[...] + jnp.sum(sel, axis=0, keepdims=True)
    ek_ref[...] = e_acc[:, :TOP_K].astype(jnp.int32)
    rk_ref[...] = r_acc[:, :TOP_K].astype(jnp.int32)
    gk_ref[...] = (g_acc / g_sum * ROUTED_SCALE)[:, :TOP_K]
    cnt_ref[...] = carry[...].astype(jnp.int32)


def _router(scores, router_bias, n_rows):
    e = N_EXPERTS
    tm = ROW_TILE
    row = lambda i: (i, 0)
    return pl.pallas_call(
        _router_kernel,
        grid=(n_rows // tm,),
        in_specs=[pl.BlockSpec((tm, e), row), pl.BlockSpec((1, e), lambda i: (0, 0))],
        out_specs=[pl.BlockSpec((tm, TOP_K), row), pl.BlockSpec((tm, TOP_K), row), pl.BlockSpec((tm, TOP_K), row),
                   pl.BlockSpec((1, e), lambda i: (0, 0))],
        out_shape=[jax.ShapeDtypeStruct((n_rows, TOP_K), jnp.int32), jax.ShapeDtypeStruct((n_rows, TOP_K), jnp.int32),
                   jax.ShapeDtypeStruct((n_rows, TOP_K), F32), jax.ShapeDtypeStruct((1, e), jnp.int32)],
        scratch_shapes=[pltpu.VMEM((1, e), F32)],
        compiler_params=_params(("arbitrary",)),
        name="router",
    )(scores, router_bias.reshape(1, e).astype(F32))


def _layout(e_k, r_k, counts, tm_e):
    e = N_EXPERTS
    n_assign = e_k.size
    counts = counts.reshape(e)
    padded = (counts + tm_e - 1) // tm_e * tm_e
    pends = jnp.cumsum(padded)
    pstarts = pends - padded
    experts = jnp.arange(e, dtype=jnp.int32)
    start_of = jnp.sum(jnp.where(e_k[..., None] == experts, pstarts, 0), axis=-1)
    dest = (start_of + r_k).reshape(n_assign).astype(jnp.int32)
    n_blocks = (n_assign + e * (tm_e - 1) + tm_e - 1) // tm_e
    blk_row0 = jnp.arange(n_blocks, dtype=jnp.int32) * tm_e
    blk_e = jnp.minimum(jnp.sum(pends[None, :] <= blk_row0[:, None], axis=-1), e - 1).astype(jnp.int32)
    n_valid = (pends[-1] // tm_e).astype(jnp.int32).reshape(1)
    pad_info = jnp.concatenate([pstarts + counts, padded - counts]).astype(jnp.int32)
    return dest, blk_e, n_valid, pad_info, n_blocks


def _dispatch_kernel(pad_ref, dest_ref, f_ref, xs_ref, zbuf, sem, zsem):
    tm = f_ref.shape[0]
    pad_bits = EXPERT_TILE.bit_length() - 1

    def zero_copies(e_idx):
        start = pad_ref[e_idx]
        pad = pad_ref[N_EXPERTS + e_idx]
        n_single = (-start) & (SUBLANES - 1)
        out = []
        for j in range(SUBLANES - 1):
            out.append((j < n_single, pltpu.make_async_copy(
                zbuf.at[pl.ds(0, 1)], xs_ref.at[pl.ds(start + j, 1)], zsem)))
        base = start + n_single
        rem = pad - n_single
        for b in range(SUBLANES.bit_length() - 1, pad_bits):
            off = pl.multiple_of(base + (rem & ((1 << b) - 1)), SUBLANES)
            out.append((((rem >> b) & 1) == 1, pltpu.make_async_copy(
                zbuf.at[pl.ds(0, 1 << b)], xs_ref.at[pl.ds(off, 1 << b)], zsem)))
        return out

    @pl.when(pl.program_id(0) == 0)
    def _():
        zbuf[...] = jnp.zeros_like(zbuf)

        def start_e(e_idx, carry):
            for cond, cp in zero_copies(e_idx):
                pl.when(cond)(cp.start)
            return carry

        def wait_e(e_idx, carry):
            for cond, cp in zero_copies(e_idx):
                pl.when(cond)(cp.wait)
            return carry

        lax.fori_loop(0, N_EXPERTS, start_e, 0)
        lax.fori_loop(0, N_EXPERTS, wait_e, 0)

    def tok(t, carry):
        for k in range(TOP_K):
            d = dest_ref[t * TOP_K + k]
            pltpu.make_async_copy(f_ref.at[pl.ds(t, 1)], xs_ref.at[pl.ds(d, 1)], sem).start(priority=k % 2)
        return carry

    lax.fori_loop(0, tm, tok, 0)
    for k in range(TOP_K):
        pltpu.make_async_copy(f_ref, xs_ref.at[pl.ds(0, tm)], sem).wait()


def _dispatch(f_in, dest, pad_info, n_rows, n_blocks):
    d = f_in.shape[1]
    tm = ROW_TILE
    grid_spec = pltpu.PrefetchScalarGridSpec(
        num_scalar_prefetch=1,
        grid=(n_rows // tm,),
        in_specs=[pl.BlockSpec((tm * TOP_K,), lambda i, p: (i,), memory_space=pltpu.SMEM),
                  pl.BlockSpec((tm, d), lambda i, p: (i, 0))],
        out_specs=pl.BlockSpec(memory_space=pl.ANY),
        scratch_shapes=[pltpu.VMEM((EXPERT_TILE // 2, d), f_in.dtype), pltpu.SemaphoreType.DMA,
                        pltpu.SemaphoreType.DMA],
    )
    return pl.pallas_call(
        _dispatch_kernel,
        grid_spec=grid_spec,
        out_shape=jax.ShapeDtypeStruct((n_blocks * EXPERT_TILE, d), f_in.dtype),
        compiler_params=_params(("arbitrary",)),
        name="dispatch",
    )(pad_info, dest, f_in)


def _gate_up(x_lo, x_hi, wg_ref, wu_ref):
    half = x_lo.shape[1]
    gv = (jnp.dot(x_lo, wg_ref[pl.ds(0, half), :], preferred_element_type=F32)
          + jnp.dot(x_hi, wg_ref[pl.ds(half, half), :], preferred_element_type=F32))
    uv = (jnp.dot(x_lo, wu_ref[pl.ds(0, half), :], preferred_element_type=F32)
          + jnp.dot(x_hi, wu_ref[pl.ds(half, half), :], preferred_element_type=F32))
    return gv, uv


def _expert_kernel(be_ref, nv_ref, x_ref, wg_ref, wu_ref, wd_ref, y_ref, wgb, wub, wdb):
    i = pl.program_id(0)
    valid = i < nv_ref[0]
    changed = jnp.logical_or(i == 0, be_ref[i] != be_ref[jnp.maximum(i - 1, 0)])
    d, de = wg_ref.shape[2], wg_ref.shape[3]
    n_chunks = 8

    @pl.when(jnp.logical_and(valid, changed))
    def _():
        def cast_chunk(c, carry):
            r0 = pl.multiple_of(c * (d // n_chunks), d // n_chunks)
            wgb[pl.ds(r0, d // n_chunks), :] = wg_ref[0, 0, pl.ds(r0, d // n_chunks), :].astype(BF16)
            wub[pl.ds(r0, d // n_chunks), :] = wu_ref[0, 0, pl.ds(r0, d // n_chunks), :].astype(BF16)
            r1 = pl.multiple_of(c * (de // n_chunks), de // n_chunks)
            wdb[pl.ds(r1, de // n_chunks), :] = wd_ref[0, 0, pl.ds(r1, de // n_chunks), :].astype(BF16)
            return carry

        lax.fori_loop(0, n_chunks, cast_chunk, 0)

    @pl.when(valid)
    def _():
        lo, hi = _unpack_bf16_pair(x_ref[...])
        gv, uv = _gate_up(lo.astype(BF16), hi.astype(BF16), wgb, wub)
        act = (_silu(gv) * uv).astype(BF16)
        y_ref[...] = _pack_bf16_pair(jnp.dot(act, wdb[...], preferred_element_type=F32))


def _expert_ffn(xs, blk_e, n_valid, w_gate, w_up, w_down, layer):
    n_rows, dp = xs.shape
    d, de = w_gate.shape[2], w_gate.shape[3]
    tm = EXPERT_TILE
    n_blocks = n_rows // tm

    def xmap(i, be, nv):
        return (jnp.minimum(i, nv[0] - 1), 0)

    def wmap(i, be, nv):
        return (layer, be[jnp.minimum(i, nv[0] - 1)], 0, 0)

    grid_spec = pltpu.PrefetchScalarGridSpec(
        num_scalar_prefetch=2,
        grid=(n_blocks,),
        in_specs=[pl.BlockSpec((tm, dp), xmap),
                  pl.BlockSpec((1, 1, d, de), wmap),
                  pl.BlockSpec((1, 1, d, de), wmap),
                  pl.BlockSpec((1, 1, de, d), wmap)],
        out_specs=pl.BlockSpec((tm, dp), xmap),
        scratch_shapes=[pltpu.VMEM((d, de), BF16), pltpu.VMEM((d, de), BF16), pltpu.VMEM((de, d), BF16)],
    )
    return pl.pallas_call(
        _expert_kernel,
        grid_spec=grid_spec,
        out_shape=jax.ShapeDtypeStruct((n_rows, dp), jnp.uint32),
        compiler_params=_params(("arbitrary",)),
        name="expert_ffn",
    )(blk_e, n_valid, xs, w_gate, w_up, w_down)


def _ffn_out_kernel(dcur_ref, dnext_ref, f_ref, gk_ref, wg_ref, wu_ref, wd_ref, h_ref, gate_ref, lng_ref, lnb_ref,
                    y_ref, o_ref, ybuf, sems):
    i = pl.program_id(0)
    n_steps = pl.num_programs(0)
    tm = f_ref.shape[0]
    slot = lax.rem(i, 2)

    def start_gather(dref, s):
        def tok(t, carry):
            for k in range(TOP_K):
                row = dref[t * TOP_K + k]
                pltpu.make_async_copy(y_ref.at[pl.ds(row, 1)], ybuf.at[s, k, pl.ds(t, 1)],
                                      sems.at[s]).start(priority=k % 2)
            return carry

        lax.fori_loop(0, tm, tok, 0)

    @pl.when(i == 0)
    def _():
        start_gather(dcur_ref, slot)

    @pl.when(i + 1 < n_steps)
    def _():
        start_gather(dnext_ref, 1 - slot)

    x_lo, x_hi = _unpack_bf16_pair(f_ref[...])
    gv, uv = _gate_up(x_lo.astype(BF16), x_hi.astype(BF16), wg_ref, wu_ref)
    act = (_silu(gv) * uv).astype(BF16)
    f = jnp.dot(act, wd_ref[...], preferred_element_type=F32)

    gk = gk_ref[...]
    for k in range(TOP_K):
        pltpu.make_async_copy(y_ref.at[pl.ds(0, tm)], ybuf.at[slot, k], sems.at[slot]).wait()
    r_lo = jnp.zeros((tm, f.shape[1] // 2), F32)
    r_hi = jnp.zeros((tm, f.shape[1] // 2), F32)
    for k in range(TOP_K):
        y_lo, y_hi = _unpack_bf16_pair(ybuf[slot, k])
        r_lo = r_lo + gk[:, k:k + 1] * y_lo
        r_hi = r_hi + gk[:, k:k + 1] * y_hi
    f = f + jnp.concatenate([r_lo, r_hi], axis=1)
    o_ref[...] = _layer_norm(DEEPNORM_ALPHA * h_ref[...] + gate_ref[0] * f, lng_ref[...], lnb_ref[...])


def _ffn_out(f_in, y, dest, gates, ws_gate, ws_up, ws_down, h, gate, ln_g, ln_b, n_rows, seq):
    d = D_MODEL
    ds_ = ws_gate.shape[1]
    tm = COMBINE_TILE
    n_steps = n_rows // tm
    grp = _group_of_tile(tm, seq)
    row = lambda i: (i, 0)
    const = lambda i: (0, 0)
    return pl.pallas_call(
        _ffn_out_kernel,
        grid=(n_steps,),
        in_specs=[pl.BlockSpec((tm * TOP_K,), lambda i: (i,), memory_space=pltpu.SMEM),
                  pl.BlockSpec((tm * TOP_K,), lambda i: (jnp.minimum(i + 1, n_steps - 1),), memory_space=pltpu.SMEM),
                  pl.BlockSpec((tm, d // 2), row), pl.BlockSpec((tm, TOP_K), row),
                  pl.BlockSpec((d, ds_), const), pl.BlockSpec((d, ds_), const), pl.BlockSpec((ds_, d), const),
                  pl.BlockSpec((tm, d), row), pl.BlockSpec((1, 1, d), lambda i: (grp(i), 0, 0)),
                  pl.BlockSpec((1, d), const), pl.BlockSpec((1, d), const),
                  pl.BlockSpec(memory_space=pl.ANY)],
        out_specs=pl.BlockSpec((tm, d), row),
        out_shape=jax.ShapeDtypeStruct((n_rows, d), F32),
        scratch_shapes=[pltpu.VMEM((2, TOP_K, tm, d // 2), jnp.uint32), pltpu.SemaphoreType.DMA((2,))],
        compiler_params=_params(("arbitrary",)),
        name="ffn_out",
    )(dest, dest, f_in, gates, ws_gate, ws_up, ws_down, h, gate, ln_g.reshape(1, d), ln_b.reshape(1, d), y)


def _moe_and_norm(h, f_in, scores, gate_f, router_bias, we_gate, we_up, we_down, ws_gate, ws_up, ws_down,
                  ln_g, ln_b, n_rows, seq, layer):
    e_k, r_k, gates, counts = _router(scores, router_bias, n_rows)
    dest, blk_e, n_valid, pad_info, n_blocks = _layout(e_k, r_k, counts, EXPERT_TILE)
    xs = _dispatch(f_in, dest, pad_info, n_rows, n_blocks)
    y = _expert_ffn(xs, blk_e, n_valid, we_gate, we_up, we_down, layer)
    return _ffn_out(f_in, y, dest, gates, ws_gate.astype(BF16), ws_up.astype(BF16), ws_down.astype(BF16),
                    h, gate_f, ln_g, ln_b, n_rows, seq)


def kernel(x, c, ctx, c_ctx, mod_w, mod_b, ln_mix_g, ln_mix_b, ln_ffn_g, ln_ffn_b, na_w_qkv, na_rpb, na_w_o,
           cv_w_pw1, cv_b_pw1, cv_w_dw, cv_b_dw, cv_ln_g, cv_ln_b, cv_w_pw2, cv_b_pw2, moe_w_router,
           moe_router_bias, moe_w_gate, moe_w_up, moe_w_down, sh_w_gate, sh_w_up, sh_w_down):
    n_batch, seq, d = x.shape
    ctx_len = ctx.shape[1]
    n_lat = n_batch * seq
    n_tok = n_lat + n_batch * ctx_len

    cvec = jnp.zeros((8, d), F32).at[:n_batch].set(c).at[n_batch].set(c_ctx)
    mod_all = _mod_matmul(cvec, mod_w, mod_b).reshape(DEPTH, 8, 6, d)[:, :n_batch + 1]
    mods = [[mod_all[i, :, p, :].reshape(n_batch + 1, 1, d) for p in range(6)] for i in range(DEPTH)]

    h = jnp.concatenate([x.reshape(n_lat, d), ctx.reshape(n_batch * ctx_len, d)], axis=0)

    sh_a, sc_a, g_a, sh_f, sc_f, g_f = mods[0]
    qkv = _mod_matmul_qkv(h, sc_a, sh_a, na_w_qkv[0].astype(BF16), n_tok, seq)
    attn = _na_attention(qkv, _na_bias_table(na_rpb[0]), n_batch, seq, ctx_len, n_tok)
    attn = _ctx_attention(qkv, attn, n_batch, seq, ctx_len)
    h, f_in, scores = _proj_ln(attn, na_w_o[0].astype(BF16), None, h, g_a, ln_mix_g[0], ln_mix_b[0],
                               sc_f, sh_f, moe_w_router[0].astype(BF16), n_tok, seq)
    h = _moe_and_norm(h, f_in, scores, g_f, moe_router_bias[0], moe_w_gate, moe_w_up, moe_w_down,
                      sh_w_gate[0], sh_w_up[0], sh_w_down[0], ln_ffn_g[0], ln_ffn_b[0], n_tok, seq, 0)

    sh_a, sc_a, g_a, sh_f, sc_f, g_f = mods[1]
    glu = _mod_matmul_glu(h, sc_a, sh_a, cv_w_pw1[0].astype(BF16), cv_b_pw1[0], n_lat, seq)
    conv = _dwconv(glu, cv_w_dw[0], cv_b_dw[0], n_batch, seq)
    h, f_in, scores = _proj_ln(conv, cv_w_pw2[0].astype(BF16), cv_b_pw2[0], h, g_a, ln_mix_g[1], ln_mix_b[1],
                               sc_f, sh_f, moe_w_router[1].astype(BF16), n_lat, seq,
                               pre_ln=(cv_ln_g[0], cv_ln_b[0]))
    h = _moe_and_norm(h, f_in, scores, g_f, moe_router_bias[1], moe_w_gate, moe_w_up, moe_w_down,
                      sh_w_gate[1], sh_w_up[1], sh_w_down[1], ln_ffn_g[1], ln_ffn_b[1], n_lat, seq, 1)
    return h.reshape(n_batch, seq, d)
```

```python
import functools

import jax
import jax.numpy as jnp
from jax import lax
from jax.experimental import pallas as pl
from jax.experimental.pallas import tpu as pltpu

D_MODEL = 2048
GRID_W = 64
N_HEADS = 16
HEAD_DIM = D_MODEL // N_HEADS
WIN_ROWS = 8
WIN_COLS = 16
CONV_WIDTH = 31
N_EXPERTS = 64
TOP_K = 8
ROUTED_SCALE = 2.5
LN_EPS = 1e-6
SUBLANES = 8
NEG_INF = -1e30
DEPTH = 2
DEEPNORM_ALPHA = (2 * DEPTH) ** 0.25

F32 = jnp.float32
BF16 = jnp.bfloat16

VMEM_LIMIT_BYTES = 56 * 1024 * 1024
ROW_TILE = 256
MM_ROW_TILE = 512
EXPERT_TILE = 256
COMBINE_TILE = 128
ROWS_PER_STEP = 8
CONV_CH = 256
CONV_TT = 64


def _params(sem):
    return pltpu.CompilerParams(dimension_semantics=sem, vmem_limit_bytes=VMEM_LIMIT_BYTES)


def _layer_norm(z, g, b):
    mu = jnp.mean(z, axis=-1, keepdims=True)
    zc = z - mu
    var = jnp.mean(zc * zc, axis=-1, keepdims=True)
    return zc * lax.rsqrt(var + LN_EPS) * g + b


def _silu(x):
    return x * jax.nn.sigmoid(x)


def _pack_bf16_pair(x):
    n = x.shape[1] // 2
    lo = pltpu.bitcast(x[:, :n].astype(BF16).astype(F32), jnp.uint32)
    hi = pltpu.bitcast(x[:, n:].astype(BF16).astype(F32), jnp.uint32)
    return (lo >> 16) | (hi & jnp.uint32(0xFFFF0000))


def _unpack_bf16_pair(w):
    lo = pltpu.bitcast(w << 16, F32)
    hi = pltpu.bitcast(w & jnp.uint32(0xFFFF0000), F32)
    return lo, hi


def _mod_kernel(c_ref, w_ref, b_ref, o_ref):
    a = _silu(c_ref[...]).astype(BF16)
    o_ref[0] = jnp.dot(a, w_ref[0].astype(BF16), preferred_element_type=F32) + b_ref[0]


def _mod_matmul(cvec, w, b):
    m, k = cvec.shape
    layers, _, n = w.shape
    tn = 1024
    return pl.pallas_call(
        _mod_kernel,
        grid=(layers, n // tn),
        in_specs=[pl.BlockSpec((m, k), lambda l, j: (0, 0)),
                  pl.BlockSpec((1, k, tn), lambda l, j: (l, 0, j)),
                  pl.BlockSpec((1, 1, tn), lambda l, j: (l, 0, j))],
        out_specs=pl.BlockSpec((1, m, tn), lambda l, j: (l, 0, j)),
        out_shape=jax.ShapeDtypeStruct((layers, m, n), F32),
        compiler_params=_params(("parallel", "arbitrary")),
        name="mod_matmul",
    )(cvec, w, b.reshape(layers, 1, n))


def _modmm_kernel(h_ref, sc_ref, sh_ref, w_ref, o_ref, a_scr):
    @pl.when(pl.program_id(1) == 0)
    def _():
        a_scr[...] = (h_ref[...] * (1.0 + sc_ref[0]) + sh_ref[0]).astype(BF16)

    o_ref[...] = jnp.dot(a_scr[...], w_ref[...], preferred_element_type=F32).astype(o_ref.dtype)


def _modglu_kernel(h_ref, sc_ref, sh_ref, wa_ref, wg_ref, ba_ref, bg_ref, o_ref, a_scr):
    @pl.when(pl.program_id(1) == 0)
    def _():
        a_scr[...] = (h_ref[...] * (1.0 + sc_ref[0]) + sh_ref[0]).astype(BF16)

    a = a_scr[...]
    va = jnp.dot(a, wa_ref[...], preferred_element_type=F32) + ba_ref[...]
    vg = jnp.dot(a, wg_ref[...], preferred_element_type=F32) + bg_ref[...]
    o_ref[...] = va * jax.nn.sigmoid(vg)


def _group_of_tile(tile_rows, seq):
    return lambda i: (i * tile_rows) // seq


def _mod_matmul_qkv(h, sc, sh, w, n_rows, seq):
    k = h.shape[1]
    n = w.shape[1]
    tm, tn = MM_ROW_TILE, 2048
    grp = _group_of_tile(tm, seq)
    return pl.pallas_call(
        _modmm_kernel,
        grid=(n_rows // tm, n // tn),
        in_specs=[pl.BlockSpec((tm, k), lambda i, j: (i, 0)),
                  pl.BlockSpec((1, 1, k), lambda i, j: (grp(i), 0, 0)),
                  pl.BlockSpec((1, 1, k), lambda i, j: (grp(i), 0, 0)),
                  pl.BlockSpec((k, tn), lambda i, j: (0, j))],
        out_specs=pl.BlockSpec((tm, tn), lambda i, j: (i, j)),
        out_shape=jax.ShapeDtypeStruct((n_rows, n), BF16),
        scratch_shapes=[pltpu.VMEM((tm, k), BF16)],
        compiler_params=_params(("parallel", "arbitrary")),
        name="mod_qkv",
    )(h, sc, sh, w)


def _mod_matmul_glu(h, sc, sh, w, b, n_rows, seq):
    k = h.shape[1]
    n = w.shape[1] // 2
    tm, tn = MM_ROW_TILE, 1024
    grp = _group_of_tile(tm, seq)
    nj = n // tn
    b2 = b.reshape(1, 2 * n)
    return pl.pallas_call(
        _modglu_kernel,
        grid=(n_rows // tm, nj),
        in_specs=[pl.BlockSpec((tm, k), lambda i, j: (i, 0)),
                  pl.BlockSpec((1, 1, k), lambda i, j: (grp(i), 0, 0)),
                  pl.BlockSpec((1, 1, k), lambda i, j: (grp(i), 0, 0)),
                  pl.BlockSpec((k, tn), lambda i, j: (0, j)),
                  pl.BlockSpec((k, tn), lambda i, j: (0, j + nj)),
                  pl.BlockSpec((1, tn), lambda i, j: (0, j)),
                  pl.BlockSpec((1, tn), lambda i, j: (0, j + nj))],
        out_specs=pl.BlockSpec((tm, tn), lambda i, j: (i, j)),
        out_shape=jax.ShapeDtypeStruct((n_rows, n), F32),
        scratch_shapes=[pltpu.VMEM((tm, k), BF16)],
        compiler_params=_params(("parallel", "arbitrary")),
        name="mod_pw1_glu",
    )(h, sc, sh, w, w, b2, b2)


def _na_window_start(g, rows):
    return jnp.clip(g * ROWS_PER_STEP - WIN_ROWS // 2, 0, rows - 2 * WIN_ROWS)


def _na_kernel(q_ref, k_ref, v_ref, kc_ref, vc_ref, bias_ref, o_ref, *, rows):
    scale = HEAD_DIM ** -0.5
    contract_last = (((1,), (1,)), ((), ()))
    k_start = pl.multiple_of(_na_window_start(pl.program_id(2), rows) * GRID_W, GRID_W)
    n_keys = 2 * WIN_ROWS * GRID_W
    q = q_ref[...]
    kw = k_ref[pl.ds(k_start, n_keys), :]
    vw = v_ref[pl.ds(k_start, n_keys), :]
    s_loc = lax.dot_general(q, kw, contract_last, preferred_element_type=F32) * scale + bias_ref[0, 0]
    s_ctx = lax.dot_general(q, kc_ref[...], contract_last, preferred_element_type=F32) * scale
    m = jnp.maximum(jnp.max(s_loc, axis=-1, keepdims=True), jnp.max(s_ctx, axis=-1, keepdims=True))
    p_loc = jnp.exp(s_loc - m)
    p_ctx = jnp.exp(s_ctx - m)
    denom = jnp.sum(p_loc, axis=-1, keepdims=True) + jnp.sum(p_ctx, axis=-1, keepdims=True)
    o = (jnp.dot(p_loc.astype(BF16), vw, preferred_element_type=F32)
         + jnp.dot(p_ctx.astype(BF16), vc_ref[...], preferred_element_type=F32))
    o_ref[...] = (o / denom).astype(o_ref.dtype)


def _na_group_kinds(rows):
    groups = rows // ROWS_PER_STEP
    return [0, 1, groups - 1], lambda g: jnp.where(g == 0, 0, jnp.where(g == groups - 1, 2, 1))


def _na_group_bias(row_table, rows):
    n_heads = row_table.shape[0]
    kinds, _ = _na_group_kinds(rows)
    per_kind = []
    for g in kinds:
        ws = min(max(g * ROWS_PER_STEP - WIN_ROWS // 2, 0), rows - 2 * WIN_ROWS)
        per_row = []
        for rr in range(ROWS_PER_STEP):
            r = g * ROWS_PER_STEP + rr
            rs = min(max(r - WIN_ROWS // 2, 0), rows - WIN_ROWS)
            before = jnp.full((n_heads, GRID_W, (rs - ws) * GRID_W), NEG_INF, F32)
            after = jnp.full((n_heads, GRID_W, (WIN_ROWS - (rs - ws)) * GRID_W), NEG_INF, F32)
            per_row.append(jnp.concatenate([before, row_table[:, r - rs], after], axis=-1))
        per_kind.append(jnp.concatenate(per_row, axis=1))
    return jnp.stack(per_kind, axis=1)


def _na_bias_table(rpb):
    col = jnp.arange(GRID_W)
    cstart = jnp.clip(col - WIN_COLS // 2, 0, GRID_W - WIN_COLS)
    col_ok = (col[None, :] >= cstart[:, None]) & (col[None, :] < cstart[:, None] + WIN_COLS)
    cidx = jnp.clip(col[None, :] - col[:, None] + WIN_COLS - 1, 0, 2 * WIN_COLS - 2)
    onehot = (cidx[:, :, None] == jnp.arange(2 * WIN_COLS - 1)).astype(F32)
    t = jnp.einsum('hrc,qkc->hrqk', rpb.astype(F32), onehot, precision=lax.Precision.HIGHEST)
    t = jnp.where(col_ok[None, None], t, NEG_INF)
    t = jnp.stack([t[:, WIN_ROWS - 1 - off:2 * WIN_ROWS - 1 - off] for off in range(WIN_ROWS)], axis=1)
    t = t.transpose(0, 1, 3, 2, 4)
    return t.reshape(rpb.shape[0], WIN_ROWS, GRID_W, WIN_ROWS * GRID_W)


def _na_attention(qkv, bias_table, n_batch, seq, ctx_len, n_rows_out):
    rows = seq // GRID_W
    groups = rows // ROWS_PER_STEP
    tq = ROWS_PER_STEP * GRID_W
    ctx_blk0 = (n_batch * seq) // ctx_len
    h_ = N_HEADS
    _, kind_of = _na_group_kinds(rows)
    group_bias = _na_group_bias(bias_table, rows)
    return pl.pallas_call(
        functools.partial(_na_kernel, rows=rows),
        grid=(n_batch, h_, groups),
        in_specs=[pl.BlockSpec((tq, HEAD_DIM), lambda b, h, g: (b * groups + g, h)),
                  pl.BlockSpec((seq, HEAD_DIM), lambda b, h, g: (b, h_ + h)),
                  pl.BlockSpec((seq, HEAD_DIM), lambda b, h, g: (b, 2 * h_ + h)),
                  pl.BlockSpec((ctx_len, HEAD_DIM), lambda b, h, g: (ctx_blk0 + b, h_ + h)),
                  pl.BlockSpec((ctx_len, HEAD_DIM), lambda b, h, g: (ctx_blk0 + b, 2 * h_ + h)),
                  pl.BlockSpec((1, 1, tq, 2 * WIN_ROWS * GRID_W), lambda b, h, g: (h, kind_of(g), 0, 0))],
        out_specs=pl.BlockSpec((tq, HEAD_DIM), lambda b, h, g: (b * groups + g, h)),
        out_shape=jax.ShapeDtypeStruct((n_rows_out, D_MODEL), BF16),
        compiler_params=_params(("parallel", "parallel", "arbitrary")),
        name="na_attention",
    )(qkv, qkv, qkv, qkv, qkv, group_bias)


def _ctx_attn_kernel(q_ref, k_ref, v_ref, prev_ref, o_ref):
    del prev_ref
    scale = HEAD_DIM ** -0.5
    s = lax.dot_general(q_ref[...], k_ref[...], (((1,), (1,)), ((), ())), preferred_element_type=F32) * scale
    m = jnp.max(s, axis=-1, keepdims=True)
    p = jnp.exp(s - m)
    denom = jnp.sum(p, axis=-1, keepdims=True)
    o = jnp.dot(p.astype(BF16), v_ref[...], preferred_element_type=F32)
    o_ref[...] = (o / denom).astype(o_ref.dtype)


def _ctx_attention(qkv, attn_out, n_batch, seq, ctx_len):
    ctx_blk0 = (n_batch * seq) // ctx_len
    h_ = N_HEADS
    return pl.pallas_call(
        _ctx_attn_kernel,
        grid=(n_batch, h_),
        in_specs=[pl.BlockSpec((ctx_len, HEAD_DIM), lambda b, h: (ctx_blk0 + b, h)),
                  pl.BlockSpec((ctx_len, HEAD_DIM), lambda b, h: (ctx_blk0 + b, h_ + h)),
                  pl.BlockSpec((ctx_len, HEAD_DIM), lambda b, h: (ctx_blk0 + b, 2 * h_ + h)),
                  pl.BlockSpec(memory_space=pl.ANY)],
        out_specs=pl.BlockSpec((ctx_len, HEAD_DIM), lambda b, h: (ctx_blk0 + b, h)),
        out_shape=jax.ShapeDtypeStruct(attn_out.shape, attn_out.dtype),
        input_output_aliases={3: 0},
        compiler_params=_params(("parallel", "parallel")),
        name="ctx_attention",
    )(qkv, qkv, qkv, attn_out)


def _dwconv_kernel(x_ref, w_ref, b_ref, o_ref, xpad, *, seq):
    half = CONV_WIDTH // 2
    lead = 2 * SUBLANES
    ch = x_ref.shape[1]
    xpad[pl.ds(0, lead), :] = jnp.zeros((lead, ch), F32)
    xpad[pl.ds(lead + seq, lead), :] = jnp.zeros((lead, ch), F32)
    xpad[pl.ds(lead, seq), :] = x_ref[...]
    w = w_ref[...]
    bias = b_ref[...]
    win_rows = CONV_TT + 2 * lead

    def body(c, carry):
        t0 = pl.multiple_of(c * CONV_TT, CONV_TT)
        win = xpad[pl.ds(t0, win_rows), :]
        acc = jnp.broadcast_to(bias, (CONV_TT, ch))
        for p in range(SUBLANES):
            shifted = win if p == 0 else pltpu.roll(win, win_rows - p, axis=0)
            for k in range(CONV_WIDTH):
                o = lead - half + k
                if o % SUBLANES == p:
                    acc = acc + shifted[o - p:o - p + CONV_TT, :] * w[k:k + 1, :]
        o_ref[pl.ds(t0, CONV_TT), :] = acc
        return carry

    lax.fori_loop(0, seq // CONV_TT, body, 0)


def _dwconv(x, w_dw, b_dw, n_batch, seq):
    d = x.shape[1]
    nc = d // CONV_CH
    return pl.pallas_call(
        functools.partial(_dwconv_kernel, seq=seq),
        grid=(n_batch, nc),
        in_specs=[pl.BlockSpec((seq, CONV_CH), lambda b, c: (b, c)),
                  pl.BlockSpec((CONV_WIDTH, CONV_CH), lambda b, c: (0, c)),
                  pl.BlockSpec((1, CONV_CH), lambda b, c: (0, c))],
        out_specs=pl.BlockSpec((seq, CONV_CH), lambda b, c: (b, c)),
        out_shape=jax.ShapeDtypeStruct((n_batch * seq, d), F32),
        scratch_shapes=[pltpu.VMEM((seq + 4 * SUBLANES, CONV_CH), F32)],
        compiler_params=_params(("parallel", "parallel")),
        name="dwconv",
    )(x, w_dw, b_dw.reshape(1, d))


def _proj_ln_kernel(*refs, pre_ln, has_bias):
    it = iter(refs)
    a_ref = next(it)
    pre_g = next(it) if pre_ln else None
    pre_b = next(it) if pre_ln else None
    w_ref = next(it)
    bias_ref = next(it) if has_bias else None
    h_ref, gate_ref, lng_ref, lnb_ref, scf_ref, shf_ref, wr_ref = (next(it) for _ in range(7))
    hnew_ref, fin_ref, score_ref = (next(it) for _ in range(3))

    a = a_ref[...]
    if pre_ln:
        a = _silu(_layer_norm(a, pre_g[...], pre_b[...]))
    m = jnp.dot(a.astype(BF16), w_ref[...], preferred_element_type=F32)
    if has_bias:
        m = m + bias_ref[...]
    hn = _layer_norm(DEEPNORM_ALPHA * h_ref[...] + gate_ref[0] * m, lng_ref[...], lnb_ref[...])
    hnew_ref[...] = hn
    f = hn * (1.0 + scf_ref[0]) + shf_ref[0]
    fin_ref[...] = _pack_bf16_pair(f)
    score_ref[...] = jax.nn.sigmoid(jnp.dot(f.astype(BF16), wr_ref[...], preferred_element_type=F32))


def _proj_ln(a, w, bias, h, gate, ln_g, ln_b, sc_f, sh_f, w_router, n_rows, seq, pre_ln=None):
    d = D_MODEL
    tm = ROW_TILE
    grp = _group_of_tile(tm, seq)
    row = lambda i: (i, 0)
    const = lambda i: (0, 0)
    grp3 = lambda i: (grp(i), 0, 0)
    args, specs = [a], [pl.BlockSpec((tm, d), row)]
    if pre_ln is not None:
        args += [pre_ln[0].reshape(1, d), pre_ln[1].reshape(1, d)]
        specs += [pl.BlockSpec((1, d), const)] * 2
    args.append(w)
    specs.append(pl.BlockSpec((d, d), const))
    if bias is not None:
        args.append(bias.reshape(1, d))
        specs.append(pl.BlockSpec((1, d), const))
    args += [h, gate, ln_g.reshape(1, d), ln_b.reshape(1, d), sc_f, sh_f, w_router]
    specs += [pl.BlockSpec((tm, d), row), pl.BlockSpec((1, 1, d), grp3),
              pl.BlockSpec((1, d), const), pl.BlockSpec((1, d), const),
              pl.BlockSpec((1, 1, d), grp3), pl.BlockSpec((1, 1, d), grp3),
              pl.BlockSpec((d, N_EXPERTS), const)]
    return pl.pallas_call(
        functools.partial(_proj_ln_kernel, pre_ln=pre_ln is not None, has_bias=bias is not None),
        grid=(n_rows // tm,),
        in_specs=specs,
        out_specs=[pl.BlockSpec((tm, d), row), pl.BlockSpec((tm, d // 2), row),
                   pl.BlockSpec((tm, N_EXPERTS), row)],
        out_shape=[jax.ShapeDtypeStruct((n_rows, d), F32),
                   jax.ShapeDtypeStruct((n_rows, d // 2), jnp.uint32),
                   jax.ShapeDtypeStruct((n_rows, N_EXPERTS), F32)],
        compiler_params=_params(("parallel",)),
        name="proj_ln",
    )(*args)


def _router_kernel(s_ref, b_ref, ek_ref, rk_ref, gk_ref, cnt_ref, carry):
    tm, e = s_ref.shape

    @pl.when(pl.program_id(0) == 0)
    def _():
        carry[...] = jnp.zeros_like(carry)

    s = s_ref[...]
    lane = lax.broadcasted_iota(jnp.int32, (tm, e), 1).astype(F32)
    slot = lax.broadcasted_iota(jnp.int32, (tm, 128), 1)
    work = s + b_ref[...]
    sel = jnp.zeros((tm, e), F32)
    e_acc = jnp.zeros((tm, 128), F32)
    g_acc = jnp.zeros((tm, 128), F32)
    g_sum = jnp.zeros((tm, 1), F32)
    picks = []
    for k in range(TOP_K):
        m = jnp.max(work, axis=-1, keepdims=True)
        idx = jnp.min(jnp.where(work == m, lane, float(e)), axis=-1, keepdims=True)
        onehot = lane == idx
        gk = jnp.sum(jnp.where(onehot, s, 0.0), axis=-1, keepdims=True)
        sel = jnp.where(onehot, 1.0, sel)
        work = jnp.where(onehot, -jnp.inf, work)
        e_acc = jnp.where(slot == k, idx, e_acc)
        g_acc = jnp.where(slot == k, gk, g_acc)
        g_sum = g_sum + gk
        picks.append(idx)

    r_i = lax.broadcasted_iota(jnp.int32, (tm, tm), 0)
    c_i = lax.broadcasted_iota(jnp.int32, (tm, tm), 1)
    tri = jnp.where(c_i < r_i, 1.0, 0.0).astype(BF16)
    rank = jnp.dot(tri, sel.astype(BF16), preferred_element_type=F32) + carry[...]
    r_acc = jnp.zeros((tm, 128), F32)
    for k in range(TOP_K):
        rk = jnp.sum(jnp.where(lane == picks[k], rank, 0.0), axis=-1, keepdims=True)
        r_acc = jnp.where(slot == k, rk, r_acc)

    carry[...] = carry[...] + jnp.sum(sel, axis=0, keepdims=True)
    ek_ref[...] = e_acc[:, :TOP_K].astype(jnp.int32)
    rk_ref[...] = r_acc[:, :TOP_K].astype(jnp.int32)
    gk_ref[...] = (g_acc / g_sum * ROUTED_SCALE)[:, :TOP_K]
    cnt_ref[...] = carry[...].astype(jnp.int32)


def _router(scores, router_bias, n_rows):
    e = N_EXPERTS
    tm = ROW_TILE
    row = lambda i: (i, 0)
    return pl.pallas_call(
        _router_kernel,
        grid=(n_rows // tm,),
        in_specs=[pl.BlockSpec((tm, e), row), pl.BlockSpec((1, e), lambda i: (0, 0))],
        out_specs=[pl.BlockSpec((tm, TOP_K), row), pl.BlockSpec((tm, TOP_K), row), pl.BlockSpec((tm, TOP_K), row),
                   pl.BlockSpec((1, e), lambda i: (0, 0))],
        out_shape=[jax.ShapeDtypeStruct((n_rows, TOP_K), jnp.int32), jax.ShapeDtypeStruct((n_rows, TOP_K), jnp.int32),
                   jax.ShapeDtypeStruct((n_rows, TOP_K), F32), jax.ShapeDtypeStruct((1, e), jnp.int32)],
        scratch_shapes=[pltpu.VMEM((1, e), F32)],
        compiler_params=_params(("arbitrary",)),
        name="router",
    )(scores, router_bias.reshape(1, e).astype(F32))


def _layout(e_k, r_k, counts, tm_e):
    e = N_EXPERTS
    n_assign = e_k.size
    counts = counts.reshape(e)
    padded = (counts + tm_e - 1) // tm_e * tm_e
    pends = jnp.cumsum(padded)
    pstarts = pends - padded
    experts = jnp.arange(e, dtype=jnp.int32)
    start_of = jnp.sum(jnp.where(e_k[..., None] == experts, pstarts, 0), axis=-1)
    dest = (start_of + r_k).reshape(n_assign).astype(jnp.int32)
    n_blocks = (n_assign + e * (tm_e - 1) + tm_e - 1) // tm_e
    blk_row0 = jnp.arange(n_blocks, dtype=jnp.int32) * tm_e
    blk_e = jnp.minimum(jnp.sum(pends[None, :] <= blk_row0[:, None], axis=-1), e - 1).astype(jnp.int32)
    n_valid = (pends[-1] // tm_e).astype(jnp.int32).reshape(1)
    pad_info = jnp.concatenate([pstarts + counts, padded - counts]).astype(jnp.int32)
    later = (experts[None, :] > experts[:, None]) & (counts[None, :] > 0)
    next_e = jnp.min(jnp.where(later, experts[None, :], e), axis=1)
    next_e = jnp.where(next_e == e, -1, next_e)
    order_e = jnp.sum((experts[None, :] < experts[:, None]) & (counts[None, :] > 0), axis=1)
    chain = jnp.concatenate([next_e, order_e]).astype(jnp.int32)
    return dest, blk_e, n_valid, pad_info, chain, n_blocks


def _dispatch_kernel(pad_ref, dest_ref, f_ref, xs_ref, zbuf, sem, zsem):
    tm = f_ref.shape[0]
    pad_bits = EXPERT_TILE.bit_length() - 1

    def zero_copies(e_idx):
        start = pad_ref[e_idx]
        pad = pad_ref[N_EXPERTS + e_idx]
        n_single = (-start) & (SUBLANES - 1)
        out = []
        for j in range(SUBLANES - 1):
            out.append((j < n_single, pltpu.make_async_copy(
                zbuf.at[pl.ds(0, 1)], xs_ref.at[pl.ds(start + j, 1)], zsem)))
        base = start + n_single
        rem = pad - n_single
        for b in range(SUBLANES.bit_length() - 1, pad_bits):
            off = pl.multiple_of(base + (rem & ((1 << b) - 1)), SUBLANES)
            out.append((((rem >> b) & 1) == 1, pltpu.make_async_copy(
                zbuf.at[pl.ds(0, 1 << b)], xs_ref.at[pl.ds(off, 1 << b)], zsem)))
        return out

    @pl.when(pl.program_id(0) == 0)
    def _():
        zbuf[...] = jnp.zeros_like(zbuf)

        def start_e(e_idx, carry):
            for cond, cp in zero_copies(e_idx):
                pl.when(cond)(cp.start)
            return carry

        def wait_e(e_idx, carry):
            for cond, cp in zero_copies(e_idx):
                pl.when(cond)(cp.wait)
            return carry

        lax.fori_loop(0, N_EXPERTS, start_e, 0)
        lax.fori_loop(0, N_EXPERTS, wait_e, 0)

    def tok(t, carry):
        for k in range(TOP_K):
            d = dest_ref[t * TOP_K + k]
            pltpu.make_async_copy(f_ref.at[pl.ds(t, 1)], xs_ref.at[pl.ds(d, 1)], sem).start(priority=k % 2)
        return carry

    lax.fori_loop(0, tm, tok, 0)
    for k in range(TOP_K):
        pltpu.make_async_copy(f_ref, xs_ref.at[pl.ds(0, tm)], sem).wait()


def _dispatch(f_in, dest, pad_info, n_rows, n_blocks):
    d = f_in.shape[1]
    tm = ROW_TILE
    grid_spec = pltpu.PrefetchScalarGridSpec(
        num_scalar_prefetch=1,
        grid=(n_rows // tm,),
        in_specs=[pl.BlockSpec((tm * TOP_K,), lambda i, p: (i,), memory_space=pltpu.SMEM),
                  pl.BlockSpec((tm, d), lambda i, p: (i, 0))],
        out_specs=pl.BlockSpec(memory_space=pl.ANY),
        scratch_shapes=[pltpu.VMEM((EXPERT_TILE // 2, d), f_in.dtype), pltpu.SemaphoreType.DMA,
                        pltpu.SemaphoreType.DMA],
    )
    return pl.pallas_call(
        _dispatch_kernel,
        grid_spec=grid_spec,
        out_shape=jax.ShapeDtypeStruct((n_blocks * EXPERT_TILE, d), f_in.dtype),
        compiler_params=_params(("arbitrary",)),
        name="dispatch",
    )(pad_info, dest, f_in)


def _gate_up(x_lo, x_hi, wg_ref, wu_ref):
    half = x_lo.shape[1]
    gv = (jnp.dot(x_lo, wg_ref[pl.ds(0, half), :], preferred_element_type=F32)
          + jnp.dot(x_hi, wg_ref[pl.ds(half, half), :], preferred_element_type=F32))
    uv = (jnp.dot(x_lo, wu_ref[pl.ds(0, half), :], preferred_element_type=F32)
          + jnp.dot(x_hi, wu_ref[pl.ds(half, half), :], preferred_element_type=F32))
    return gv, uv


def _expert_kernel(be_ref, nv_ref, chain_ref, x_ref, wg_hbm, wu_hbm, wd_hbm, y_ref,
                   wg32, wu32, wd32, wgb, wub, wdb, sems, *, layer):
    i = pl.program_id(0)
    valid = i < nv_ref[0]
    e_cur = be_ref[i]
    first = jnp.logical_or(i == 0, e_cur != be_ref[jnp.maximum(i - 1, 0)])
    d, de = wgb.shape
    n_chunks = 8

    def weight_copies(e_idx, slot):
        return [pltpu.make_async_copy(w_hbm.at[layer, e_idx], buf.at[slot], sems.at[slot, j])
                for j, (w_hbm, buf) in enumerate(((wg_hbm, wg32), (wu_hbm, wu32), (wd_hbm, wd32)))]

    @pl.when(jnp.logical_and(valid, first))
    def _():
        slot = chain_ref[N_EXPERTS + e_cur] & 1
        e_next = chain_ref[e_cur]

        @pl.when(i == 0)
        def _():
            for cp in weight_copies(e_cur, slot):
                cp.start()

        for cp in weight_copies(e_cur, slot):
            cp.wait()

        @pl.when(e_next >= 0)
        def _():
            for cp in weight_copies(e_next, 1 - slot):
                cp.start()

        def cast_chunk(c, carry):
            r0 = pl.multiple_of(c * (d // n_chunks), d // n_chunks)
            wgb[pl.ds(r0, d // n_chunks), :] = wg32[slot, pl.ds(r0, d // n_chunks), :].astype(BF16)
            wub[pl.ds(r0, d // n_chunks), :] = wu32[slot, pl.ds(r0, d // n_chunks), :].astype(BF16)
            r1 = pl.multiple_of(c * (de // n_chunks), de // n_chunks)
            wdb[pl.ds(r1, de // n_chunks), :] = wd32[slot, pl.ds(r1, de // n_chunks), :].astype(BF16)
            return carry

        lax.fori_loop(0, n_chunks, cast_chunk, 0)

    @pl.when(valid)
    def _():
        lo, hi = _unpack_bf16_pair(x_ref[...])
        gv, uv = _gate_up(lo.astype(BF16), hi.astype(BF16), wgb, wub)
        act = (_silu(gv) * uv).astype(BF16)
        y_ref[...] = _pack_bf16_pair(jnp.dot(act, wdb[...], preferred_element_type=F32))


def _expert_ffn(xs, blk_e, n_valid, chain, w_gate, w_up, w_down, layer):
    n_rows, dp = xs.shape
    d, de = w_gate.shape[2], w_gate.shape[3]
    tm = EXPERT_TILE
    n_blocks = n_rows // tm

    def xmap(i, be, nv, ch):
        return (jnp.minimum(i, nv[0] - 1), 0)

    grid_spec = pltpu.PrefetchScalarGridSpec(
        num_scalar_prefetch=3,
        grid=(n_blocks,),
        in_specs=[pl.BlockSpec((tm, dp), xmap),
                  pl.BlockSpec(memory_space=pl.ANY),
                  pl.BlockSpec(memory_space=pl.ANY),
                  pl.BlockSpec(memory_space=pl.ANY)],
        out_specs=pl.BlockSpec((tm, dp), xmap),
        scratch_shapes=[pltpu.VMEM((2, d, de), F32), pltpu.VMEM((2, d, de), F32), pltpu.VMEM((2, de, d), F32),
                        pltpu.VMEM((d, de), BF16), pltpu.VMEM((d, de), BF16), pltpu.VMEM((de, d), BF16),
                        pltpu.SemaphoreType.DMA((2, 3))],
    )
    return pl.pallas_call(
        functools.partial(_expert_kernel, layer=layer),
        grid_spec=grid_spec,
        out_shape=jax.ShapeDtypeStruct((n_rows, dp), jnp.uint32),
        compiler_params=_params(("arbitrary",)),
        name="expert_ffn",
    )(blk_e, n_valid, chain, xs, w_gate, w_up, w_down)


def _ffn_out_kernel(dcur_ref, dnext_ref, f_ref, gk_ref, wg_ref, wu_ref, wd_ref, h_ref, gate_ref, lng_ref, lnb_ref,
                    y_ref, o_ref, ybuf, sems):
    i = pl.program_id(0)
    n_steps = pl.num_programs(0)
    tm = f_ref.shape[0]
    slot = lax.rem(i, 2)

    def start_gather(dref, s):
        def tok(t, carry):
            for k in range(TOP_K):
                row = dref[t * TOP_K + k]
                pltpu.make_async_copy(y_ref.at[pl.ds(row, 1)], ybuf.at[s, k, pl.ds(t, 1)],
                                      sems.at[s]).start(priority=k % 2)
            return carry

        lax.fori_loop(0, tm, tok, 0)

    @pl.when(i == 0)
    def _():
        start_gather(dcur_ref, slot)

    @pl.when(i + 1 < n_steps)
    def _():
        start_gather(dnext_ref, 1 - slot)

    x_lo, x_hi = _unpack_bf16_pair(f_ref[...])
    gv, uv = _gate_up(x_lo.astype(BF16), x_hi.astype(BF16), wg_ref, wu_ref)
    act = (_silu(gv) * uv).astype(BF16)
    f = jnp.dot(act, wd_ref[...], preferred_element_type=F32)

    gk = gk_ref[...]
    for k in range(TOP_K):
        pltpu.make_async_copy(y_ref.at[pl.ds(0, tm)], ybuf.at[slot, k], sems.at[slot]).wait()
    r_lo = jnp.zeros((tm, f.shape[1] // 2), F32)
    r_hi = jnp.zeros((tm, f.shape[1] // 2), F32)
    for k in range(TOP_K):
        y_lo, y_hi = _unpack_bf16_pair(ybuf[slot, k])
        r_lo = r_lo + gk[:, k:k + 1] * y_lo
        r_hi = r_hi + gk[:, k:k + 1] * y_hi
    f = f + jnp.concatenate([r_lo, r_hi], axis=1)
    o_ref[...] = _layer_norm(DEEPNORM_ALPHA * h_ref[...] + gate_ref[0] * f, lng_ref[...], lnb_ref[...])


def _ffn_out(f_in, y, dest, gates, ws_gate, ws_up, ws_down, h, gate, ln_g, ln_b, n_rows, seq):
    d = D_MODEL
    ds_ = ws_gate.shape[1]
    tm = COMBINE_TILE
    n_steps = n_rows // tm
    grp = _group_of_tile(tm, seq)
    row = lambda i: (i, 0)
    const = lambda i: (0, 0)
    return pl.pallas_call(
        _ffn_out_kernel,
        grid=(n_steps,),
        in_specs=[pl.BlockSpec((tm * TOP_K,), lambda i: (i,), memory_space=pltpu.SMEM),
                  pl.BlockSpec((tm * TOP_K,), lambda i: (jnp.minimum(i + 1, n_steps - 1),), memory_space=pltpu.SMEM),
                  pl.BlockSpec((tm, d // 2), row), pl.BlockSpec((tm, TOP_K), row),
                  pl.BlockSpec((d, ds_), const), pl.BlockSpec((d, ds_), const), pl.BlockSpec((ds_, d), const),
                  pl.BlockSpec((tm, d), row), pl.BlockSpec((1, 1, d), lambda i: (grp(i), 0, 0)),
                  pl.BlockSpec((1, d), const), pl.BlockSpec((1, d), const),
                  pl.BlockSpec(memory_space=pl.ANY)],
        out_specs=pl.BlockSpec((tm, d), row),
        out_shape=jax.ShapeDtypeStruct((n_rows, d), F32),
        scratch_shapes=[pltpu.VMEM((2, TOP_K, tm, d // 2), jnp.uint32), pltpu.SemaphoreType.DMA((2,))],
        compiler_params=_params(("arbitrary",)),
        name="ffn_out",
    )(dest, dest, f_in, gates, ws_gate, ws_up, ws_down, h, gate, ln_g.reshape(1, d), ln_b.reshape(1, d), y)


def _moe_and_norm(h, f_in, scores, gate_f, router_bias, we_gate, we_up, we_down, ws_gate, ws_up, ws_down,
                  ln_g, ln_b, n_rows, seq, layer):
    e_k, r_k, gates, counts = _router(scores, router_bias, n_rows)
    dest, blk_e, n_valid, pad_info, chain, n_blocks = _layout(e_k, r_k, counts, EXPERT_TILE)
    xs = _dispatch(f_in, dest, pad_info, n_rows, n_blocks)
    y = _expert_ffn(xs, blk_e, n_valid, chain, we_gate, we_up, we_down, layer)
    return _ffn_out(f_in, y, dest, gates, ws_gate.astype(BF16), ws_up.astype(BF16), ws_down.astype(BF16),
                    h, gate_f, ln_g, ln_b, n_rows, seq)


def kernel(x, c, ctx, c_ctx, mod_w, mod_b, ln_mix_g, ln_mix_b, ln_ffn_g, ln_ffn_b, na_w_qkv, na_rpb, na_w_o,
           cv_w_pw1, cv_b_pw1, cv_w_dw, cv_b_dw, cv_ln_g, cv_ln_b, cv_w_pw2, cv_b_pw2, moe_w_router,
           moe_router_bias, moe_w_gate, moe_w_up, moe_w_down, sh_w_gate, sh_w_up, sh_w_down):
    n_batch, seq, d = x.shape
    ctx_len = ctx.shape[1]
    n_lat = n_batch * seq
    n_tok = n_lat + n_batch * ctx_len

    cvec = jnp.zeros((8, d), F32).at[:n_batch].set(c).at[n_batch].set(c_ctx)
    mod_all = _mod_matmul(cvec, mod_w, mod_b).reshape(DEPTH, 8, 6, d)[:, :n_batch + 1]
    mods = [[mod_all[i, :, p, :].reshape(n_batch + 1, 1, d) for p in range(6)] for i in range(DEPTH)]

    h = jnp.concatenate([x.reshape(n_lat, d), ctx.reshape(n_batch * ctx_len, d)], axis=0)

    sh_a, sc_a, g_a, sh_f, sc_f, g_f = mods[0]
    qkv = _mod_matmul_qkv(h, sc_a, sh_a, na_w_qkv[0].astype(BF16), n_tok, seq)
    attn = _na_attention(qkv, _na_bias_table(na_rpb[0]), n_batch, seq, ctx_len, n_tok)
    attn = _ctx_attention(qkv, attn, n_batch, seq, ctx_len)
    h, f_in, scores = _proj_ln(attn, na_w_o[0].astype(BF16), None, h, g_a, ln_mix_g[0], ln_mix_b[0],
                               sc_f, sh_f, moe_w_router[0].astype(BF16), n_tok, seq)
    h = _moe_and_norm(h, f_in, scores, g_f, moe_router_bias[0], moe_w_gate, moe_w_up, moe_w_down,
                      sh_w_gate[0], sh_w_up[0], sh_w_down[0], ln_ffn_g[0], ln_ffn_b[0], n_tok, seq, 0)

    sh_a, sc_a, g_a, sh_f, sc_f, g_f = mods[1]
    glu = _mod_matmul_glu(h, sc_a, sh_a, cv_w_pw1[0].astype(BF16), cv_b_pw1[0], n_lat, seq)
    conv = _dwconv(glu, cv_w_dw[0], cv_b_dw[0], n_batch, seq)
    h, f_in, scores = _proj_ln(conv, cv_w_pw2[0].astype(BF16), cv_b_pw2[0], h, g_a, ln_mix_g[1], ln_mix_b[1],
                               sc_f, sh_f, moe_w_router[1].astype(BF16), n_lat, seq,
                               pre_ln=(cv_ln_g[0], cv_ln_b[0]))
    h = _moe_and_norm(h, f_in, scores, g_f, moe_router_bias[1], moe_w_gate, moe_w_up, moe_w_down,
                      sh_w_gate[1], sh_w_up[1], sh_w_down[1], ln_ffn_g[1], ln_ffn_b[1], n_lat, seq, 1)
    return h.reshape(n_batch, seq, d)
```

```python
import functools

import jax
import jax.numpy as jnp
from jax import lax
from jax.experimental import pallas as pl
from jax.experimental.pallas import tpu as pltpu

D_MODEL = 2048
GRID_W = 64
N_HEADS = 16
HEAD_DIM = D_MODEL // N_HEADS
WIN_ROWS = 8
WIN_COLS = 16
CONV_WIDTH = 31
N_EXPERTS = 64
TOP_K = 8
ROUTED_SCALE = 2.5
LN_EPS = 1e-6
SUBLANES = 8
NEG_INF = -1e30
DEPTH = 2
DEEPNORM_ALPHA = (2 * DEPTH) ** 0.25

F32 = jnp.float32
BF16 = jnp.bfloat16

VMEM_LIMIT_BYTES = 56 * 1024 * 1024
ROW_TILE = 256
MM_ROW_TILE = 512
EXPERT_TILE = 256
COMBINE_TILE = 128
ROWS_PER_STEP = 8
CONV_CH = 256
CONV_TT = 64


def _params(sem):
    return pltpu.CompilerParams(dimension_semantics=sem, vmem_limit_bytes=VMEM_LIMIT_BYTES)


def _layer_norm(z, g, b):
    mu = jnp.mean(z, axis=-1, keepdims=True)
    zc = z - mu
    var = jnp.mean(zc * zc, axis=-1, keepdims=True)
    return zc * lax.rsqrt(var + LN_EPS) * g + b


def _silu(x):
    return x * jax.nn.sigmoid(x)


LANES = 128
ROW_WORDS = D_MODEL // 2
assert ROW_WORDS == SUBLANES * LANES


def _store_packed_rows(ref, x):
    rows = x.shape[0]
    for s_ in range(SUBLANES):
        base = 2 * LANES * s_
        lo = pltpu.bitcast(x[:, base:base + LANES].astype(BF16).astype(F32), jnp.uint32)
        hi = pltpu.bitcast(x[:, base + LANES:base + 2 * LANES].astype(BF16).astype(F32), jnp.uint32)
        ref[pl.ds(s_, rows, stride=SUBLANES), :] = (lo >> 16) | (hi & jnp.uint32(0xFFFF0000))


def _load_packed_group(ref, s_):
    w = ref[pl.ds(s_, ref.shape[0] // SUBLANES, stride=SUBLANES), :]
    return pltpu.bitcast(w << 16, F32), pltpu.bitcast(w & jnp.uint32(0xFFFF0000), F32)


def _row_tile(ref, t, n=1):
    return ref.at[pl.ds(pl.multiple_of(t * SUBLANES, SUBLANES), n * SUBLANES)]


def _load_packed_rows_bf16(ref):
    parts = []
    for s_ in range(SUBLANES):
        lo, hi = _load_packed_group(ref, s_)
        parts += [lo.astype(BF16), hi.astype(BF16)]
    return jnp.concatenate(parts, axis=1)


def _mod_kernel(c_ref, w_ref, b_ref, o_ref):
    a = _silu(c_ref[...]).astype(BF16)
    o_ref[0] = jnp.dot(a, w_ref[0].astype(BF16), preferred_element_type=F32) + b_ref[0]


def _mod_matmul(cvec, w, b):
    m, k = cvec.shape
    layers, _, n = w.shape
    tn = 1024
    return pl.pallas_call(
        _mod_kernel,
        grid=(layers, n // tn),
        in_specs=[pl.BlockSpec((m, k), lambda l, j: (0, 0)),
                  pl.BlockSpec((1, k, tn), lambda l, j: (l, 0, j)),
                  pl.BlockSpec((1, 1, tn), lambda l, j: (l, 0, j))],
        out_specs=pl.BlockSpec((1, m, tn), lambda l, j: (l, 0, j)),
        out_shape=jax.ShapeDtypeStruct((layers, m, n), F32),
        compiler_params=_params(("parallel", "arbitrary")),
        name="mod_matmul",
    )(cvec, w, b.reshape(layers, 1, n))


def _modmm_kernel(h_ref, sc_ref, sh_ref, w_ref, o_ref, a_scr):
    @pl.when(pl.program_id(1) == 0)
    def _():
        a_scr[...] = (h_ref[...] * (1.0 + sc_ref[0]) + sh_ref[0]).astype(BF16)

    o_ref[...] = jnp.dot(a_scr[...], w_ref[...], preferred_element_type=F32).astype(o_ref.dtype)


def _modglu_kernel(h_ref, sc_ref, sh_ref, wa_ref, wg_ref, ba_ref, bg_ref, o_ref, a_scr):
    @pl.when(pl.program_id(1) == 0)
    def _():
        a_scr[...] = (h_ref[...] * (1.0 + sc_ref[0]) + sh_ref[0]).astype(BF16)

    a = a_scr[...]
    va = jnp.dot(a, wa_ref[...], preferred_element_type=F32) + ba_ref[...]
    vg = jnp.dot(a, wg_ref[...], preferred_element_type=F32) + bg_ref[...]
    o_ref[...] = va * jax.nn.sigmoid(vg)


def _group_of_tile(tile_rows, seq):
    return lambda i: (i * tile_rows) // seq


def _mod_matmul_qkv(h, sc, sh, w, n_rows, seq):
    k = h.shape[1]
    n = w.shape[1]
    tm, tn = MM_ROW_TILE, 2048
    grp = _group_of_tile(tm, seq)
    return pl.pallas_call(
        _modmm_kernel,
        grid=(n_rows // tm, n // tn),
        in_specs=[pl.BlockSpec((tm, k), lambda i, j: (i, 0)),
                  pl.BlockSpec((1, 1, k), lambda i, j: (grp(i), 0, 0)),
                  pl.BlockSpec((1, 1, k), lambda i, j: (grp(i), 0, 0)),
                  pl.BlockSpec((k, tn), lambda i, j: (0, j))],
        out_specs=pl.BlockSpec((tm, tn), lambda i, j: (i, j)),
        out_shape=jax.ShapeDtypeStruct((n_rows, n), BF16),
        scratch_shapes=[pltpu.VMEM((tm, k), BF16)],
        compiler_params=_params(("parallel", "arbitrary")),
        name="mod_qkv",
    )(h, sc, sh, w)


def _mod_matmul_glu(h, sc, sh, w, b, n_rows, seq):
    k = h.shape[1]
    n = w.shape[1] // 2
    tm, tn = MM_ROW_TILE, 1024
    grp = _group_of_tile(tm, seq)
    nj = n // tn
    b2 = b.reshape(1, 2 * n)
    return pl.pallas_call(
        _modglu_kernel,
        grid=(n_rows // tm, nj),
        in_specs=[pl.BlockSpec((tm, k), lambda i, j: (i, 0)),
                  pl.BlockSpec((1, 1, k), lambda i, j: (grp(i), 0, 0)),
                  pl.BlockSpec((1, 1, k), lambda i, j: (grp(i), 0, 0)),
                  pl.BlockSpec((k, tn), lambda i, j: (0, j)),
                  pl.BlockSpec((k, tn), lambda i, j: (0, j + nj)),
                  pl.BlockSpec((1, tn), lambda i, j: (0, j)),
                  pl.BlockSpec((1, tn), lambda i, j: (0, j + nj))],
        out_specs=pl.BlockSpec((tm, tn), lambda i, j: (i, j)),
        out_shape=jax.ShapeDtypeStruct((n_rows, n), F32),
        scratch_shapes=[pltpu.VMEM((tm, k), BF16)],
        compiler_params=_params(("parallel", "arbitrary")),
        name="mod_pw1_glu",
    )(h, sc, sh, w, w, b2, b2)


def _na_window_start(g, rows):
    return jnp.clip(g * ROWS_PER_STEP - WIN_ROWS // 2, 0, rows - 2 * WIN_ROWS)


def _na_kernel(q_ref, k_ref, v_ref, kc_ref, vc_ref, bias_ref, o_ref, *, rows):
    scale = HEAD_DIM ** -0.5
    contract_last = (((1,), (1,)), ((), ()))
    k_start = pl.multiple_of(_na_window_start(pl.program_id(2), rows) * GRID_W, GRID_W)
    n_keys = 2 * WIN_ROWS * GRID_W
    q = q_ref[...]
    kw = k_ref[pl.ds(k_start, n_keys), :]
    vw = v_ref[pl.ds(k_start, n_keys), :]
    s_loc = lax.dot_general(q, kw, contract_last, preferred_element_type=F32) * scale + bias_ref[0, 0]
    s_ctx = lax.dot_general(q, kc_ref[...], contract_last, preferred_element_type=F32) * scale
    m = jnp.maximum(jnp.max(s_loc, axis=-1, keepdims=True), jnp.max(s_ctx, axis=-1, keepdims=True))
    p_loc = jnp.exp(s_loc - m)
    p_ctx = jnp.exp(s_ctx - m)
    denom = jnp.sum(p_loc, axis=-1, keepdims=True) + jnp.sum(p_ctx, axis=-1, keepdims=True)
    o = (jnp.dot(p_loc.astype(BF16), vw, preferred_element_type=F32)
         + jnp.dot(p_ctx.astype(BF16), vc_ref[...], preferred_element_type=F32))
    o_ref[...] = (o / denom).astype(o_ref.dtype)


def _na_group_kinds(rows):
    groups = rows // ROWS_PER_STEP
    return [0, 1, groups - 1], lambda g: jnp.where(g == 0, 0, jnp.where(g == groups - 1, 2, 1))


def _na_group_bias(row_table, rows):
    n_heads = row_table.shape[0]
    kinds, _ = _na_group_kinds(rows)
    per_kind = []
    for g in kinds:
        ws = min(max(g * ROWS_PER_STEP - WIN_ROWS // 2, 0), rows - 2 * WIN_ROWS)
        per_row = []
        for rr in range(ROWS_PER_STEP):
            r = g * ROWS_PER_STEP + rr
            rs = min(max(r - WIN_ROWS // 2, 0), rows - WIN_ROWS)
            before = jnp.full((n_heads, GRID_W, (rs - ws) * GRID_W), NEG_INF, F32)
            after = jnp.full((n_heads, GRID_W, (WIN_ROWS - (rs - ws)) * GRID_W), NEG_INF, F32)
            per_row.append(jnp.concatenate([before, row_table[:, r - rs], after], axis=-1))
        per_kind.append(jnp.concatenate(per_row, axis=1))
    return jnp.stack(per_kind, axis=1)


def _na_bias_table(rpb):
    col = jnp.arange(GRID_W)
    cstart = jnp.clip(col - WIN_COLS // 2, 0, GRID_W - WIN_COLS)
    col_ok = (col[None, :] >= cstart[:, None]) & (col[None, :] < cstart[:, None] + WIN_COLS)
    cidx = jnp.clip(col[None, :] - col[:, None] + WIN_COLS - 1, 0, 2 * WIN_COLS - 2)
    onehot = (cidx[:, :, None] == jnp.arange(2 * WIN_COLS - 1)).astype(F32)
    t = jnp.einsum('hrc,qkc->hrqk', rpb.astype(F32), onehot, precision=lax.Precision.HIGHEST)
    t = jnp.where(col_ok[None, None], t, NEG_INF)
    t = jnp.stack([t[:, WIN_ROWS - 1 - off:2 * WIN_ROWS - 1 - off] for off in range(WIN_ROWS)], axis=1)
    t = t.transpose(0, 1, 3, 2, 4)
    return t.reshape(rpb.shape[0], WIN_ROWS, GRID_W, WIN_ROWS * GRID_W)


def _na_attention(qkv, bias_table, n_batch, seq, ctx_len, n_rows_out):
    rows = seq // GRID_W
    groups = rows // ROWS_PER_STEP
    tq = ROWS_PER_STEP * GRID_W
    ctx_blk0 = (n_batch * seq) // ctx_len
    h_ = N_HEADS
    _, kind_of = _na_group_kinds(rows)
    group_bias = _na_group_bias(bias_table, rows)
    return pl.pallas_call(
        functools.partial(_na_kernel, rows=rows),
        grid=(n_batch, h_, groups),
        in_specs=[pl.BlockSpec((tq, HEAD_DIM), lambda b, h, g: (b * groups + g, h)),
                  pl.BlockSpec((seq, HEAD_DIM), lambda b, h, g: (b, h_ + h)),
                  pl.BlockSpec((seq, HEAD_DIM), lambda b, h, g: (b, 2 * h_ + h)),
                  pl.BlockSpec((ctx_len, HEAD_DIM), lambda b, h, g: (ctx_blk0 + b, h_ + h)),
                  pl.BlockSpec((ctx_len, HEAD_DIM), lambda b, h, g: (ctx_blk0 + b, 2 * h_ + h)),
                  pl.BlockSpec((1, 1, tq, 2 * WIN_ROWS * GRID_W), lambda b, h, g: (h, kind_of(g), 0, 0))],
        out_specs=pl.BlockSpec((tq, HEAD_DIM), lambda b, h, g: (b * groups + g, h)),
        out_shape=jax.ShapeDtypeStruct((n_rows_out, D_MODEL), BF16),
        compiler_params=_params(("parallel", "parallel", "arbitrary")),
        name="na_attention",
    )(qkv, qkv, qkv, qkv, qkv, group_bias)


def _ctx_attn_kernel(q_ref, k_ref, v_ref, prev_ref, o_ref):
    del prev_ref
    scale = HEAD_DIM ** -0.5
    s = lax.dot_general(q_ref[...], k_ref[...], (((1,), (1,)), ((), ())), preferred_element_type=F32) * scale
    m = jnp.max(s, axis=-1, keepdims=True)
    p = jnp.exp(s - m)
    denom = jnp.sum(p, axis=-1, keepdims=True)
    o = jnp.dot(p.astype(BF16), v_ref[...], preferred_element_type=F32)
    o_ref[...] = (o / denom).astype(o_ref.dtype)


def _ctx_attention(qkv, attn_out, n_batch, seq, ctx_len):
    ctx_blk0 = (n_batch * seq) // ctx_len
    h_ = N_HEADS
    return pl.pallas_call(
        _ctx_attn_kernel,
        grid=(n_batch, h_),
        in_specs=[pl.BlockSpec((ctx_len, HEAD_DIM), lambda b, h: (ctx_blk0 + b, h)),
                  pl.BlockSpec((ctx_len, HEAD_DIM), lambda b, h: (ctx_blk0 + b, h_ + h)),
                  pl.BlockSpec((ctx_len, HEAD_DIM), lambda b, h: (ctx_blk0 + b, 2 * h_ + h)),
                  pl.BlockSpec(memory_space=pl.ANY)],
        out_specs=pl.BlockSpec((ctx_len, HEAD_DIM), lambda b, h: (ctx_blk0 + b, h)),
        out_shape=jax.ShapeDtypeStruct(attn_out.shape, attn_out.dtype),
        input_output_aliases={3: 0},
        compiler_params=_params(("parallel", "parallel")),
        name="ctx_attention",
    )(qkv, qkv, qkv, attn_out)


def _dwconv_kernel(x_ref, w_ref, b_ref, o_ref, xpad, *, seq):
    half = CONV_WIDTH // 2
    lead = 2 * SUBLANES
    ch = x_ref.shape[1]
    xpad[pl.ds(0, lead), :] = jnp.zeros((lead, ch), F32)
    xpad[pl.ds(lead + seq, lead), :] = jnp.zeros((lead, ch), F32)
    xpad[pl.ds(lead, seq), :] = x_ref[...]
    w = w_ref[...]
    bias = b_ref[...]
    win_rows = CONV_TT + 2 * lead

    def body(c, carry):
        t0 = pl.multiple_of(c * CONV_TT, CONV_TT)
        win = xpad[pl.ds(t0, win_rows), :]
        acc = jnp.broadcast_to(bias, (CONV_TT, ch))
        for p in range(SUBLANES):
            shifted = win if p == 0 else pltpu.roll(win, win_rows - p, axis=0)
            for k in range(CONV_WIDTH):
                o = lead - half + k
                if o % SUBLANES == p:
                    acc = acc + shifted[o - p:o - p + CONV_TT, :] * w[k:k + 1, :]
        o_ref[pl.ds(t0, CONV_TT), :] = acc
        return carry

    lax.fori_loop(0, seq // CONV_TT, body, 0)


def _dwconv(x, w_dw, b_dw, n_batch, seq):
    d = x.shape[1]
    nc = d // CONV_CH
    return pl.pallas_call(
        functools.partial(_dwconv_kernel, seq=seq),
        grid=(n_batch, nc),
        in_specs=[pl.BlockSpec((seq, CONV_CH), lambda b, c: (b, c)),
                  pl.BlockSpec((CONV_WIDTH, CONV_CH), lambda b, c: (0, c)),
                  pl.BlockSpec((1, CONV_CH), lambda b, c: (0, c))],
        out_specs=pl.BlockSpec((seq, CONV_CH), lambda b, c: (b, c)),
        out_shape=jax.ShapeDtypeStruct((n_batch * seq, d), F32),
        scratch_shapes=[pltpu.VMEM((seq + 4 * SUBLANES, CONV_CH), F32)],
        compiler_params=_params(("parallel", "parallel")),
        name="dwconv",
    )(x, w_dw, b_dw.reshape(1, d))


def _proj_ln_kernel(*refs, pre_ln, has_bias):
    it = iter(refs)
    a_ref = next(it)
    pre_g = next(it) if pre_ln else None
    pre_b = next(it) if pre_ln else None
    w_ref = next(it)
    bias_ref = next(it) if has_bias else None
    h_ref, gate_ref, lng_ref, lnb_ref, scf_ref, shf_ref, wr_ref = (next(it) for _ in range(7))
    hnew_ref, fin_ref, score_ref = (next(it) for _ in range(3))

    a = a_ref[...]
    if pre_ln:
        a = _silu(_layer_norm(a, pre_g[...], pre_b[...]))
    m = jnp.dot(a.astype(BF16), w_ref[...], preferred_element_type=F32)
    if has_bias:
        m = m + bias_ref[...]
    hn = _layer_norm(DEEPNORM_ALPHA * h_ref[...] + gate_ref[0] * m, lng_ref[...], lnb_ref[...])
    hnew_ref[...] = hn
    f = hn * (1.0 + scf_ref[0]) + shf_ref[0]
    _store_packed_rows(fin_ref, f)
    score_ref[...] = jax.nn.sigmoid(jnp.dot(f.astype(BF16), wr_ref[...], preferred_element_type=F32))


def _proj_ln(a, w, bias, h, gate, ln_g, ln_b, sc_f, sh_f, w_router, n_rows, seq, pre_ln=None):
    d = D_MODEL
    tm = ROW_TILE
    grp = _group_of_tile(tm, seq)
    row = lambda i: (i, 0)
    const = lambda i: (0, 0)
    grp3 = lambda i: (grp(i), 0, 0)
    args, specs = [a], [pl.BlockSpec((tm, d), row)]
    if pre_ln is not None:
        args += [pre_ln[0].reshape(1, d), pre_ln[1].reshape(1, d)]
        specs += [pl.BlockSpec((1, d), const)] * 2
    args.append(w)
    specs.append(pl.BlockSpec((d, d), const))
    if bias is not None:
        args.append(bias.reshape(1, d))
        specs.append(pl.BlockSpec((1, d), const))
    args += [h, gate, ln_g.reshape(1, d), ln_b.reshape(1, d), sc_f, sh_f, w_router]
    specs += [pl.BlockSpec((tm, d), row), pl.BlockSpec((1, 1, d), grp3),
              pl.BlockSpec((1, d), const), pl.BlockSpec((1, d), const),
              pl.BlockSpec((1, 1, d), grp3), pl.BlockSpec((1, 1, d), grp3),
              pl.BlockSpec((d, N_EXPERTS), const)]
    return pl.pallas_call(
        functools.partial(_proj_ln_kernel, pre_ln=pre_ln is not None, has_bias=bias is not None),
        grid=(n_rows // tm,),
        in_specs=specs,
        out_specs=[pl.BlockSpec((tm, d), row), pl.BlockSpec((tm * SUBLANES, LANES), row),
                   pl.BlockSpec((tm, N_EXPERTS), row)],
        out_shape=[jax.ShapeDtypeStruct((n_rows, d), F32),
                   jax.ShapeDtypeStruct((n_rows * SUBLANES, LANES), jnp.uint32),
                   jax.ShapeDtypeStruct((n_rows, N_EXPERTS), F32)],
        compiler_params=_params(("parallel",)),
        name="proj_ln",
    )(*args)


def _router_kernel(s_ref, b_ref, ek_ref, rk_ref, gk_ref, cnt_ref, carry):
    tm, e = s_ref.shape

    @pl.when(pl.program_id(0) == 0)
    def _():
        carry[...] = jnp.zeros_like(carry)

    s = s_ref[...]
    lane = lax.broadcasted_iota(jnp.int32, (tm, e), 1).astype(F32)
    slot = lax.broadcasted_iota(jnp.int32, (tm, 128), 1)
    work = s + b_ref[...]
    sel = jnp.zeros((tm, e), F32)
    e_acc = jnp.zeros((tm, 128), F32)
    g_acc = jnp.zeros((tm, 128), F32)
    g_sum = jnp.zeros((tm, 1), F32)
    picks = []
    for k in range(TOP_K):
        m = jnp.max(work, axis=-1, keepdims=True)
        idx = jnp.min(jnp.where(work == m, lane, float(e)), axis=-1, keepdims=True)
        onehot = lane == idx
        gk = jnp.sum(jnp.where(onehot, s, 0.0), axis=-1, keepdims=True)
        sel = jnp.where(onehot, 1.0, sel)
        work = jnp.where(onehot, -jnp.inf, work)
        e_acc = jnp.where(slot == k, idx, e_acc)
        g_acc = jnp.where(slot == k, gk, g_acc)
        g_sum = g_sum + gk
        picks.append(idx)

    r_i = lax.broadcasted_iota(jnp.int32, (tm, tm), 0)
    c_i = lax.broadcasted_iota(jnp.int32, (tm, tm), 1)
    tri = jnp.where(c_i < r_i, 1.0, 0.0).astype(BF16)
    rank = jnp.dot(tri, sel.astype(BF16), preferred_element_type=F32) + carry[...]
    r_acc = jnp.zeros((tm, 128), F32)
    for k in range(TOP_K):
        rk = jnp.sum(jnp.where(lane == picks[k], rank, 0.0), axis=-1, keepdims=True)
        r_acc = jnp.where(slot == k, rk, r_acc)

    carry[...] = carry[...] + jnp.sum(sel, axis=0, keepdims=True)
    ek_ref[...] = e_acc[:, :TOP_K].astype(jnp.int32)
    rk_ref[...] = r_acc[:, :TOP_K].astype(jnp.int32)
    gk_ref[...] = (g_acc / g_sum * ROUTED_SCALE)[:, :TOP_K]
    cnt_ref[...] = carry[...].astype(jnp.int32)


def _router(scores, router_bias, n_rows):
    e = N_EXPERTS
    tm = ROW_TILE
    row = lambda i: (i, 0)
    return pl.pallas_call(
        _router_kernel,
        grid=(n_rows // tm,),
        in_specs=[pl.BlockSpec((tm, e), row), pl.BlockSpec((1, e), lambda i: (0, 0))],
        out_specs=[pl.BlockSpec((tm, TOP_K), row), pl.BlockSpec((tm, TOP_K), row), pl.BlockSpec((tm, TOP_K), row),
                   pl.BlockSpec((1, e), lambda i: (0, 0))],
        out_shape=[jax.ShapeDtypeStruct((n_rows, TOP_K), jnp.int32), jax.ShapeDtypeStruct((n_rows, TOP_K), jnp.int32),
                   jax.ShapeDtypeStruct((n_rows, TOP_K), F32), jax.ShapeDtypeStruct((1, e), jnp.int32)],
        scratch_shapes=[pltpu.VMEM((1, e), F32)],
        compiler_params=_params(("arbitrary",)),
        name="router",
    )(scores, router_bias.reshape(1, e).astype(F32))


def _layout(e_k, r_k, counts, tm_e):
    e = N_EXPERTS
    n_assign = e_k.size
    counts = counts.reshape(e)
    padded = (counts + tm_e - 1) // tm_e * tm_e
    pends = jnp.cumsum(padded)
    pstarts = pends - padded
    experts = jnp.arange(e, dtype=jnp.int32)
    start_of = jnp.sum(jnp.where(e_k[..., None] == experts, pstarts, 0), axis=-1)
    dest = (start_of + r_k).reshape(n_assign).astype(jnp.int32)
    n_blocks = (n_assign + e * (tm_e - 1) + tm_e - 1) // tm_e
    blk_row0 = jnp.arange(n_blocks, dtype=jnp.int32) * tm_e
    blk_e = jnp.minimum(jnp.sum(pends[None, :] <= blk_row0[:, None], axis=-1), e - 1).astype(jnp.int32)
    n_valid = (pends[-1] // tm_e).astype(jnp.int32).reshape(1)
    pad_info = jnp.concatenate([pstarts + counts, padded - counts]).astype(jnp.int32)
    later = (experts[None, :] > experts[:, None]) & (counts[None, :] > 0)
    next_e = jnp.min(jnp.where(later, experts[None, :], e), axis=1)
    next_e = jnp.where(next_e == e, -1, next_e)
    order_e = jnp.sum((experts[None, :] < experts[:, None]) & (counts[None, :] > 0), axis=1)
    chain = jnp.concatenate([next_e, order_e]).astype(jnp.int32)
    return dest, blk_e, n_valid, pad_info, chain, n_blocks


def _dispatch_kernel(pad_ref, dest_ref, f_ref, xs_ref, zbuf, sem, zsem):
    tm = f_ref.shape[0] // SUBLANES
    pad_bits = EXPERT_TILE.bit_length() - 1

    def zero_copies(e_idx):
        start = pad_ref[e_idx]
        pad = pad_ref[N_EXPERTS + e_idx]
        return [(((pad >> b) & 1) == 1, pltpu.make_async_copy(
            _row_tile(zbuf, 0, 1 << b), _row_tile(xs_ref, start + (pad & ((1 << b) - 1)), 1 << b), zsem))
            for b in range(pad_bits)]

    @pl.when(pl.program_id(0) == 0)
    def _():
        zbuf[...] = jnp.zeros_like(zbuf)

        def start_e(e_idx, carry):
            for cond, cp in zero_copies(e_idx):
                pl.when(cond)(cp.start)
            return carry

        def wait_e(e_idx, carry):
            for cond, cp in zero_copies(e_idx):
                pl.when(cond)(cp.wait)
            return carry

        lax.fori_loop(0, N_EXPERTS, start_e, 0)
        lax.fori_loop(0, N_EXPERTS, wait_e, 0)

    def tok(t, carry):
        for k in range(TOP_K):
            d = dest_ref[t * TOP_K + k]
            pltpu.make_async_copy(_row_tile(f_ref, t), _row_tile(xs_ref, d), sem).start(priority=k % 2)
        return carry

    lax.fori_loop(0, tm, tok, 0)
    for k in range(TOP_K):
        pltpu.make_async_copy(f_ref, _row_tile(xs_ref, 0, tm), sem).wait()


def _dispatch(f_in, dest, pad_info, n_rows, n_blocks):
    tm = ROW_TILE
    grid_spec = pltpu.PrefetchScalarGridSpec(
        num_scalar_prefetch=1,
        grid=(n_rows // tm,),
        in_specs=[pl.BlockSpec((tm * TOP_K,), lambda i, p: (i,), memory_space=pltpu.SMEM),
                  pl.BlockSpec((tm * SUBLANES, LANES), lambda i, p: (i, 0))],
        out_specs=pl.BlockSpec(memory_space=pl.ANY),
        scratch_shapes=[pltpu.VMEM((EXPERT_TILE // 2 * SUBLANES, LANES), f_in.dtype), pltpu.SemaphoreType.DMA,
                        pltpu.SemaphoreType.DMA],
    )
    return pl.pallas_call(
        _dispatch_kernel,
        grid_spec=grid_spec,
        out_shape=jax.ShapeDtypeStruct((n_blocks * EXPERT_TILE * SUBLANES, LANES), f_in.dtype),
        compiler_params=_params(("arbitrary",)),
        name="dispatch",
    )(pad_info, dest, f_in)


def _swiglu_hidden(x, wg_ref, wu_ref):
    gv = jnp.dot(x, wg_ref[...], preferred_element_type=F32)
    uv = jnp.dot(x, wu_ref[...], preferred_element_type=F32)
    return (_silu(gv) * uv).astype(BF16)


def _expert_kernel(be_ref, nv_ref, chain_ref, x_ref, wg_hbm, wu_hbm, wd_hbm, y_ref,
                   wg32, wu32, wd32, wgb, wub, wdb, sems, *, layer):
    i = pl.program_id(0)
    valid = i < nv_ref[0]
    e_cur = be_ref[i]
    first = jnp.logical_or(i == 0, e_cur != be_ref[jnp.maximum(i - 1, 0)])
    d, de = wgb.shape
    n_chunks = 8

    def weight_copies(e_idx, slot):
        return [pltpu.make_async_copy(w_hbm.at[layer, e_idx], buf.at[slot], sems.at[slot, j])
                for j, (w_hbm, buf) in enumerate(((wg_hbm, wg32), (wu_hbm, wu32), (wd_hbm, wd32)))]

    @pl.when(jnp.logical_and(valid, first))
    def _():
        slot = chain_ref[N_EXPERTS + e_cur] & 1
        e_next = chain_ref[e_cur]

        @pl.when(i == 0)
        def _():
            for cp in weight_copies(e_cur, slot):
                cp.start()

        for cp in weight_copies(e_cur, slot):
            cp.wait()

        @pl.when(e_next >= 0)
        def _():
            for cp in weight_copies(e_next, 1 - slot):
                cp.start()

        def cast_chunk(c, carry):
            r0 = pl.multiple_of(c * (d // n_chunks), d // n_chunks)
            wgb[pl.ds(r0, d // n_chunks), :] = wg32[slot, pl.ds(r0, d // n_chunks), :].astype(BF16)
            wub[pl.ds(r0, d // n_chunks), :] = wu32[slot, pl.ds(r0, d // n_chunks), :].astype(BF16)
            r1 = pl.multiple_of(c * (de // n_chunks), de // n_chunks)
            wdb[pl.ds(r1, de // n_chunks), :] = wd32[slot, pl.ds(r1, de // n_chunks), :].astype(BF16)
            return carry

        lax.fori_loop(0, n_chunks, cast_chunk, 0)

    @pl.when(valid)
    def _():
        act = _swiglu_hidden(_load_packed_rows_bf16(x_ref), wgb, wub)
        _store_packed_rows(y_ref, jnp.dot(act, wdb[...], preferred_element_type=F32))


def _expert_ffn(xs, blk_e, n_valid, chain, w_gate, w_up, w_down, layer):
    n_rows = xs.shape[0] // SUBLANES
    d, de = w_gate.shape[2], w_gate.shape[3]
    tm = EXPERT_TILE
    n_blocks = n_rows // tm

    def xmap(i, be, nv, ch):
        return (jnp.minimum(i, nv[0] - 1), 0)

    grid_spec = pltpu.PrefetchScalarGridSpec(
        num_scalar_prefetch=3,
        grid=(n_blocks,),
        in_specs=[pl.BlockSpec((tm * SUBLANES, LANES), xmap),
                  pl.BlockSpec(memory_space=pl.ANY),
                  pl.BlockSpec(memory_space=pl.ANY),
                  pl.BlockSpec(memory_space=pl.ANY)],
        out_specs=pl.BlockSpec((tm * SUBLANES, LANES), xmap),
        scratch_shapes=[pltpu.VMEM((2, d, de), F32), pltpu.VMEM((2, d, de), F32), pltpu.VMEM((2, de, d), F32),
                        pltpu.VMEM((d, de), BF16), pltpu.VMEM((d, de), BF16), pltpu.VMEM((de, d), BF16),
                        pltpu.SemaphoreType.DMA((2, 3))],
    )
    return pl.pallas_call(
        functools.partial(_expert_kernel, layer=layer),
        grid_spec=grid_spec,
        out_shape=jax.ShapeDtypeStruct((n_rows * SUBLANES, LANES), jnp.uint32),
        compiler_params=_params(("arbitrary",)),
        name="expert_ffn",
    )(blk_e, n_valid, chain, xs, w_gate, w_up, w_down)


def _ffn_out_kernel(dcur_ref, dnext_ref, f_ref, gk_ref, wg_ref, wu_ref, wd_ref, h_ref, gate_ref, lng_ref, lnb_ref,
                    y_ref, o_ref, ybuf, sems):
    i = pl.program_id(0)
    n_steps = pl.num_programs(0)
    tm = f_ref.shape[0] // SUBLANES
    slot = lax.rem(i, 2)

    def start_gather(dref, s):
        def tok(t, carry):
            for k in range(TOP_K):
                row = dref[t * TOP_K + k]
                pltpu.make_async_copy(_row_tile(y_ref, row), _row_tile(ybuf.at[s, k], t),
                                      sems.at[s]).start(priority=k % 2)
            return carry

        lax.fori_loop(0, tm, tok, 0)

    @pl.when(i == 0)
    def _():
        start_gather(dcur_ref, slot)

    @pl.when(i + 1 < n_steps)
    def _():
        start_gather(dnext_ref, 1 - slot)

    act = _swiglu_hidden(_load_packed_rows_bf16(f_ref), wg_ref, wu_ref)
    f = jnp.dot(act, wd_ref[...], preferred_element_type=F32)

    gk = gk_ref[...]
    for k in range(TOP_K):
        pltpu.make_async_copy(_row_tile(y_ref, 0, tm), ybuf.at[slot, k], sems.at[slot]).wait()
    gcols = [jnp.broadcast_to(gk[:, k:k + 1], (tm, LANES)) for k in range(TOP_K)]
    parts = []
    for s_ in range(SUBLANES):
        r_lo = jnp.zeros((tm, LANES), F32)
        r_hi = jnp.zeros((tm, LANES), F32)
        for k in range(TOP_K):
            y_lo, y_hi = _load_packed_group(ybuf.at[slot, k], s_)
            r_lo = r_lo + gcols[k] * y_lo
            r_hi = r_hi + gcols[k] * y_hi
        parts += [r_lo, r_hi]
    f = f + jnp.concatenate(parts, axis=1)
    o_ref[...] = _layer_norm(DEEPNORM_ALPHA * h_ref[...] + gate_ref[0] * f, lng_ref[...], lnb_ref[...])


def _ffn_out(f_in, y, dest, gates, ws_gate, ws_up, ws_down, h, gate, ln_g, ln_b, n_rows, seq):
    d = D_MODEL
    ds_ = ws_gate.shape[1]
    tm = COMBINE_TILE
    n_steps = n_rows // tm
    grp = _group_of_tile(tm, seq)
    row = lambda i: (i, 0)
    const = lambda i: (0, 0)
    return pl.pallas_call(
        _ffn_out_kernel,
        grid=(n_steps,),
        in_specs=[pl.BlockSpec((tm * TOP_K,), lambda i: (i,), memory_space=pltpu.SMEM),
                  pl.BlockSpec((tm * TOP_K,), lambda i: (jnp.minimum(i + 1, n_steps - 1),), memory_space=pltpu.SMEM),
                  pl.BlockSpec((tm * SUBLANES, LANES), row), pl.BlockSpec((tm, TOP_K), row),
                  pl.BlockSpec((d, ds_), const), pl.BlockSpec((d, ds_), const), pl.BlockSpec((ds_, d), const),
                  pl.BlockSpec((tm, d), row), pl.BlockSpec((1, 1, d), lambda i: (grp(i), 0, 0)),
                  pl.BlockSpec((1, d), const), pl.BlockSpec((1, d), const),
                  pl.BlockSpec(memory_space=pl.ANY)],
        out_specs=pl.BlockSpec((tm, d), row),
        out_shape=jax.ShapeDtypeStruct((n_rows, d), F32),
        scratch_shapes=[pltpu.VMEM((2, TOP_K, tm * SUBLANES, LANES), jnp.uint32), pltpu.SemaphoreType.DMA((2,))],
        compiler_params=_params(("arbitrary",)),
        name="ffn_out",
    )(dest, dest, f_in, gates, ws_gate, ws_up, ws_down, h, gate, ln_g.reshape(1, d), ln_b.reshape(1, d), y)


def _moe_and_norm(h, f_in, scores, gate_f, router_bias, we_gate, we_up, we_down, ws_gate, ws_up, ws_down,
                  ln_g, ln_b, n_rows, seq, layer):
    e_k, r_k, gates, counts = _router(scores, router_bias, n_rows)
    dest, blk_e, n_valid, pad_info, chain, n_blocks = _layout(e_k, r_k, counts, EXPERT_TILE)
    xs = _dispatch(f_in, dest, pad_info, n_rows, n_blocks)
    y = _expert_ffn(xs, blk_e, n_valid, chain, we_gate, we_up, we_down, layer)
    return _ffn_out(f_in, y, dest, gates, ws_gate.astype(BF16), ws_up.astype(BF16), ws_down.astype(BF16),
                    h, gate_f, ln_g, ln_b, n_rows, seq)


def kernel(x, c, ctx, c_ctx, mod_w, mod_b, ln_mix_g, ln_mix_b, ln_ffn_g, ln_ffn_b, na_w_qkv, na_rpb, na_w_o,
           cv_w_pw1, cv_b_pw1, cv_w_dw, cv_b_dw, cv_ln_g, cv_ln_b, cv_w_pw2, cv_b_pw2, moe_w_router,
           moe_router_bias, moe_w_gate, moe_w_up, moe_w_down, sh_w_gate, sh_w_up, sh_w_down):
    n_batch, seq, d = x.shape
    ctx_len = ctx.shape[1]
    n_lat = n_batch * seq
    n_tok = n_lat + n_batch * ctx_len

    cvec = jnp.zeros((8, d), F32).at[:n_batch].set(c).at[n_batch].set(c_ctx)
    mod_all = _mod_matmul(cvec, mod_w, mod_b).reshape(DEPTH, 8, 6, d)[:, :n_batch + 1]
    mods = [[mod_all[i, :, p, :].reshape(n_batch + 1, 1, d) for p in range(6)] for i in range(DEPTH)]

    h = jnp.concatenate([x.reshape(n_lat, d), ctx.reshape(n_batch * ctx_len, d)], axis=0)

    sh_a, sc_a, g_a, sh_f, sc_f, g_f = mods[0]
    qkv = _mod_matmul_qkv(h, sc_a, sh_a, na_w_qkv[0].astype(BF16), n_tok, seq)
    attn = _na_attention(qkv, _na_bias_table(na_rpb[0]), n_batch, seq, ctx_len, n_tok)
    attn = _ctx_attention(qkv, attn, n_batch, seq, ctx_len)
    h, f_in, scores = _proj_ln(attn, na_w_o[0].astype(BF16), None, h, g_a, ln_mix_g[0], ln_mix_b[0],
                               sc_f, sh_f, moe_w_router[0].astype(BF16), n_tok, seq)
    h = _moe_and_norm(h, f_in, scores, g_f, moe_router_bias[0], moe_w_gate, moe_w_up, moe_w_down,
                      sh_w_gate[0], sh_w_up[0], sh_w_down[0], ln_ffn_g[0], ln_ffn_b[0], n_tok, seq, 0)

    sh_a, sc_a, g_a, sh_f, sc_f, g_f = mods[1]
    glu = _mod_matmul_glu(h, sc_a, sh_a, cv_w_pw1[0].astype(BF16), cv_b_pw1[0], n_lat, seq)
    conv = _dwconv(glu, cv_w_dw[0], cv_b_dw[0], n_batch, seq)
    h, f_in, scores = _proj_ln(conv, cv_w_pw2[0].astype(BF16), cv_b_pw2[0], h, g_a, ln_mix_g[1], ln_mix_b[1],
                               sc_f, sh_f, moe_w_router[1].astype(BF16), n_lat, seq,
                               pre_ln=(cv_ln_g[0], cv_ln_b[0]))
    h = _moe_and_norm(h, f_in, scores, g_f, moe_router_bias[1], moe_w_gate, moe_w_up, moe_w_down,
                      sh_w_gate[1], sh_w_up[1], sh_w_down[1], ln_ffn_g[1], ln_ffn_b[1], n_lat, seq, 1)
    return h.reshape(n_batch, seq, d)
```

```python
import functools

import jax
import jax.numpy as jnp
from jax import lax
from jax.experimental import pallas as pl
from jax.experimental.pallas import tpu as pltpu

D_MODEL = 2048
GRID_W = 64
N_HEADS = 16
HEAD_DIM = D_MODEL // N_HEADS
WIN_ROWS = 8
WIN_COLS = 16
CONV_WIDTH = 31
N_EXPERTS = 64
TOP_K = 8
ROUTED_SCALE = 2.5
LN_EPS = 1e-6
SUBLANES = 8
NEG_INF = -1e30
DEPTH = 2
DEEPNORM_ALPHA = (2 * DEPTH) ** 0.25

F32 = jnp.float32
BF16 = jnp.bfloat16

VMEM_LIMIT_BYTES = 56 * 1024 * 1024
ROW_TILE = 256
MM_ROW_TILE = 512
EXPERT_TILE = 512
COMBINE_TILE = 128
ROWS_PER_STEP = 8
CONV_CH = 256
CONV_TT = 64


def _params(sem):
    return pltpu.CompilerParams(dimension_semantics=sem, vmem_limit_bytes=VMEM_LIMIT_BYTES)


def _layer_norm(z, g, b):
    mu = jnp.mean(z, axis=-1, keepdims=True)
    zc = z - mu
    var = jnp.mean(zc * zc, axis=-1, keepdims=True)
    return zc * lax.rsqrt(var + LN_EPS) * g + b


def _silu(x):
    return x * jax.nn.sigmoid(x)


LANES = 128
ROW_WORDS = D_MODEL // 2
assert ROW_WORDS == SUBLANES * LANES


def _store_packed_rows(ref, x):
    rows = x.shape[0]
    for s_ in range(SUBLANES):
        base = 2 * LANES * s_
        lo = pltpu.bitcast(x[:, base:base + LANES].astype(BF16).astype(F32), jnp.uint32)
        hi = pltpu.bitcast(x[:, base + LANES:base + 2 * LANES].astype(BF16).astype(F32), jnp.uint32)
        ref[pl.ds(s_, rows, stride=SUBLANES), :] = (lo >> 16) | (hi & jnp.uint32(0xFFFF0000))


def _load_packed_group(ref, s_):
    w = ref[pl.ds(s_, ref.shape[0] // SUBLANES, stride=SUBLANES), :]
    return pltpu.bitcast(w << 16, F32), pltpu.bitcast(w & jnp.uint32(0xFFFF0000), F32)


def _row_tile(ref, t, n=1):
    return ref.at[pl.ds(pl.multiple_of(t * SUBLANES, SUBLANES), n * SUBLANES)]


def _load_packed_rows_bf16(ref):
    parts = []
    for s_ in range(SUBLANES):
        lo, hi = _load_packed_group(ref, s_)
        parts += [lo.astype(BF16), hi.astype(BF16)]
    return jnp.concatenate(parts, axis=1)


def _mod_kernel(c_ref, w_ref, b_ref, o_ref):
    a = _silu(c_ref[...]).astype(BF16)
    o_ref[0] = jnp.dot(a, w_ref[0].astype(BF16), preferred_element_type=F32) + b_ref[0]


def _mod_matmul(cvec, w, b):
    m, k = cvec.shape
    layers, _, n = w.shape
    tn = 1024
    return pl.pallas_call(
        _mod_kernel,
        grid=(layers, n // tn),
        in_specs=[pl.BlockSpec((m, k), lambda l, j: (0, 0)),
                  pl.BlockSpec((1, k, tn), lambda l, j: (l, 0, j)),
                  pl.BlockSpec((1, 1, tn), lambda l, j: (l, 0, j))],
        out_specs=pl.BlockSpec((1, m, tn), lambda l, j: (l, 0, j)),
        out_shape=jax.ShapeDtypeStruct((layers, m, n), F32),
        compiler_params=_params(("parallel", "arbitrary")),
        name="mod_matmul",
    )(cvec, w, b.reshape(layers, 1, n))


def _modmm_kernel(h_ref, sc_ref, sh_ref, w_ref, o_ref, a_scr):
    @pl.when(pl.program_id(1) == 0)
    def _():
        a_scr[...] = (h_ref[...] * (1.0 + sc_ref[0]) + sh_ref[0]).astype(BF16)

    o_ref[...] = jnp.dot(a_scr[...], w_ref[...], preferred_element_type=F32).astype(o_ref.dtype)


def _modglu_kernel(h_ref, sc_ref, sh_ref, wa_ref, wg_ref, ba_ref, bg_ref, o_ref, a_scr):
    @pl.when(pl.program_id(1) == 0)
    def _():
        a_scr[...] = (h_ref[...] * (1.0 + sc_ref[0]) + sh_ref[0]).astype(BF16)

    a = a_scr[...]
    va = jnp.dot(a, wa_ref[...], preferred_element_type=F32) + ba_ref[...]
    vg = jnp.dot(a, wg_ref[...], preferred_element_type=F32) + bg_ref[...]
    o_ref[...] = va * jax.nn.sigmoid(vg)


def _group_of_tile(tile_rows, seq):
    return lambda i: (i * tile_rows) // seq


def _mod_matmul_qkv(h, sc, sh, w, n_rows, seq):
    k = h.shape[1]
    n = w.shape[1]
    tm, tn = MM_ROW_TILE, 2048
    grp = _group_of_tile(tm, seq)
    return pl.pallas_call(
        _modmm_kernel,
        grid=(n_rows // tm, n // tn),
        in_specs=[pl.BlockSpec((tm, k), lambda i, j: (i, 0)),
                  pl.BlockSpec((1, 1, k), lambda i, j: (grp(i), 0, 0)),
                  pl.BlockSpec((1, 1, k), lambda i, j: (grp(i), 0, 0)),
                  pl.BlockSpec((k, tn), lambda i, j: (0, j))],
        out_specs=pl.BlockSpec((tm, tn), lambda i, j: (i, j)),
        out_shape=jax.ShapeDtypeStruct((n_rows, n), BF16),
        scratch_shapes=[pltpu.VMEM((tm, k), BF16)],
        compiler_params=_params(("parallel", "arbitrary")),
        name="mod_qkv",
    )(h, sc, sh, w)


def _mod_matmul_glu(h, sc, sh, w, b, n_rows, seq):
    k = h.shape[1]
    n = w.shape[1] // 2
    tm, tn = MM_ROW_TILE, 1024
    grp = _group_of_tile(tm, seq)
    nj = n // tn
    b2 = b.reshape(1, 2 * n)
    return pl.pallas_call(
        _modglu_kernel,
        grid=(n_rows // tm, nj),
        in_specs=[pl.BlockSpec((tm, k), lambda i, j: (i, 0)),
                  pl.BlockSpec((1, 1, k), lambda i, j: (grp(i), 0, 0)),
                  pl.BlockSpec((1, 1, k), lambda i, j: (grp(i), 0, 0)),
                  pl.BlockSpec((k, tn), lambda i, j: (0, j)),
                  pl.BlockSpec((k, tn), lambda i, j: (0, j + nj)),
                  pl.BlockSpec((1, tn), lambda i, j: (0, j)),
                  pl.BlockSpec((1, tn), lambda i, j: (0, j + nj))],
        out_specs=pl.BlockSpec((tm, tn), lambda i, j: (i, j)),
        out_shape=jax.ShapeDtypeStruct((n_rows, n), F32),
        scratch_shapes=[pltpu.VMEM((tm, k), BF16)],
        compiler_params=_params(("parallel", "arbitrary")),
        name="mod_pw1_glu",
    )(h, sc, sh, w, w, b2, b2)


def _na_window_start(g, rows):
    return jnp.clip(g * ROWS_PER_STEP - WIN_ROWS // 2, 0, rows - 2 * WIN_ROWS)


def _na_kernel(q_ref, k_ref, v_ref, kc_ref, vc_ref, bias_ref, o_ref, *, rows):
    scale = HEAD_DIM ** -0.5
    contract_last = (((1,), (1,)), ((), ()))
    k_start = pl.multiple_of(_na_window_start(pl.program_id(2), rows) * GRID_W, GRID_W)
    n_keys = 2 * WIN_ROWS * GRID_W
    q = q_ref[...]
    kw = k_ref[pl.ds(k_start, n_keys), :]
    vw = v_ref[pl.ds(k_start, n_keys), :]
    s_loc = lax.dot_general(q, kw, contract_last, preferred_element_type=F32) * scale + bias_ref[0, 0]
    s_ctx = lax.dot_general(q, kc_ref[...], contract_last, preferred_element_type=F32) * scale
    m = jnp.maximum(jnp.max(s_loc, axis=-1, keepdims=True), jnp.max(s_ctx, axis=-1, keepdims=True))
    p_loc = jnp.exp(s_loc - m)
    p_ctx = jnp.exp(s_ctx - m)
    denom = jnp.sum(p_loc, axis=-1, keepdims=True) + jnp.sum(p_ctx, axis=-1, keepdims=True)
    o = (jnp.dot(p_loc.astype(BF16), vw, preferred_element_type=F32)
         + jnp.dot(p_ctx.astype(BF16), vc_ref[...], preferred_element_type=F32))
    o_ref[...] = (o / denom).astype(o_ref.dtype)


def _na_group_kinds(rows):
    groups = rows // ROWS_PER_STEP
    return [0, 1, groups - 1], lambda g: jnp.where(g == 0, 0, jnp.where(g == groups - 1, 2, 1))


def _na_group_bias(row_table, rows):
    n_heads = row_table.shape[0]
    kinds, _ = _na_group_kinds(rows)
    per_kind = []
    for g in kinds:
        ws = min(max(g * ROWS_PER_STEP - WIN_ROWS // 2, 0), rows - 2 * WIN_ROWS)
        per_row = []
        for rr in range(ROWS_PER_STEP):
            r = g * ROWS_PER_STEP + rr
            rs = min(max(r - WIN_ROWS // 2, 0), rows - WIN_ROWS)
            before = jnp.full((n_heads, GRID_W, (rs - ws) * GRID_W), NEG_INF, F32)
            after = jnp.full((n_heads, GRID_W, (WIN_ROWS - (rs - ws)) * GRID_W), NEG_INF, F32)
            per_row.append(jnp.concatenate([before, row_table[:, r - rs], after], axis=-1))
        per_kind.append(jnp.concatenate(per_row, axis=1))
    return jnp.stack(per_kind, axis=1)


def _na_bias_table(rpb):
    col = jnp.arange(GRID_W)
    cstart = jnp.clip(col - WIN_COLS // 2, 0, GRID_W - WIN_COLS)
    col_ok = (col[None, :] >= cstart[:, None]) & (col[None, :] < cstart[:, None] + WIN_COLS)
    cidx = jnp.clip(col[None, :] - col[:, None] + WIN_COLS - 1, 0, 2 * WIN_COLS - 2)
    onehot = (cidx[:, :, None] == jnp.arange(2 * WIN_COLS - 1)).astype(F32)
    t = jnp.einsum('hrc,qkc->hrqk', rpb.astype(F32), onehot, precision=lax.Precision.HIGHEST)
    t = jnp.where(col_ok[None, None], t, NEG_INF)
    t = jnp.stack([t[:, WIN_ROWS - 1 - off:2 * WIN_ROWS - 1 - off] for off in range(WIN_ROWS)], axis=1)
    t = t.transpose(0, 1, 3, 2, 4)
    return t.reshape(rpb.shape[0], WIN_ROWS, GRID_W, WIN_ROWS * GRID_W)


def _na_attention(qkv, bias_table, n_batch, seq, ctx_len, n_rows_out):
    rows = seq // GRID_W
    groups = rows // ROWS_PER_STEP
    tq = ROWS_PER_STEP * GRID_W
    ctx_blk0 = (n_batch * seq) // ctx_len
    h_ = N_HEADS
    _, kind_of = _na_group_kinds(rows)
    group_bias = _na_group_bias(bias_table, rows)
    return pl.pallas_call(
        functools.partial(_na_kernel, rows=rows),
        grid=(n_batch, h_, groups),
        in_specs=[pl.BlockSpec((tq, HEAD_DIM), lambda b, h, g: (b * groups + g, h)),
                  pl.BlockSpec((seq, HEAD_DIM), lambda b, h, g: (b, h_ + h)),
                  pl.BlockSpec((seq, HEAD_DIM), lambda b, h, g: (b, 2 * h_ + h)),
                  pl.BlockSpec((ctx_len, HEAD_DIM), lambda b, h, g: (ctx_blk0 + b, h_ + h)),
                  pl.BlockSpec((ctx_len, HEAD_DIM), lambda b, h, g: (ctx_blk0 + b, 2 * h_ + h)),
                  pl.BlockSpec((1, 1, tq, 2 * WIN_ROWS * GRID_W), lambda b, h, g: (h, kind_of(g), 0, 0))],
        out_specs=pl.BlockSpec((tq, HEAD_DIM), lambda b, h, g: (b * groups + g, h)),
        out_shape=jax.ShapeDtypeStruct((n_rows_out, D_MODEL), BF16),
        compiler_params=_params(("parallel", "parallel", "arbitrary")),
        name="na_attention",
    )(qkv, qkv, qkv, qkv, qkv, group_bias)


def _ctx_attn_kernel(q_ref, k_ref, v_ref, prev_ref, o_ref):
    del prev_ref
    scale = HEAD_DIM ** -0.5
    s = lax.dot_general(q_ref[...], k_ref[...], (((1,), (1,)), ((), ())), preferred_element_type=F32) * scale
    m = jnp.max(s, axis=-1, keepdims=True)
    p = jnp.exp(s - m)
    denom = jnp.sum(p, axis=-1, keepdims=True)
    o = jnp.dot(p.astype(BF16), v_ref[...], preferred_element_type=F32)
    o_ref[...] = (o / denom).astype(o_ref.dtype)


def _ctx_attention(qkv, attn_out, n_batch, seq, ctx_len):
    ctx_blk0 = (n_batch * seq) // ctx_len
    h_ = N_HEADS
    return pl.pallas_call(
        _ctx_attn_kernel,
        grid=(n_batch, h_),
        in_specs=[pl.BlockSpec((ctx_len, HEAD_DIM), lambda b, h: (ctx_blk0 + b, h)),
                  pl.BlockSpec((ctx_len, HEAD_DIM), lambda b, h: (ctx_blk0 + b, h_ + h)),
                  pl.BlockSpec((ctx_len, HEAD_DIM), lambda b, h: (ctx_blk0 + b, 2 * h_ + h)),
                  pl.BlockSpec(memory_space=pl.ANY)],
        out_specs=pl.BlockSpec((ctx_len, HEAD_DIM), lambda b, h: (ctx_blk0 + b, h)),
        out_shape=jax.ShapeDtypeStruct(attn_out.shape, attn_out.dtype),
        input_output_aliases={3: 0},
        compiler_params=_params(("parallel", "parallel")),
        name="ctx_attention",
    )(qkv, qkv, qkv, attn_out)


def _dwconv_kernel(x_ref, w_ref, b_ref, o_ref, xpad, *, seq):
    half = CONV_WIDTH // 2
    lead = 2 * SUBLANES
    ch = x_ref.shape[1]
    xpad[pl.ds(0, lead), :] = jnp.zeros((lead, ch), F32)
    xpad[pl.ds(lead + seq, lead), :] = jnp.zeros((lead, ch), F32)
    xpad[pl.ds(lead, seq), :] = x_ref[...]
    w = w_ref[...]
    bias = b_ref[...]
    win_rows = CONV_TT + 2 * lead

    def body(c, carry):
        t0 = pl.multiple_of(c * CONV_TT, CONV_TT)
        win = xpad[pl.ds(t0, win_rows), :]
        acc = jnp.broadcast_to(bias, (CONV_TT, ch))
        for p in range(SUBLANES):
            shifted = win if p == 0 else pltpu.roll(win, win_rows - p, axis=0)
            for k in range(CONV_WIDTH):
                o = lead - half + k
                if o % SUBLANES == p:
                    acc = acc + shifted[o - p:o - p + CONV_TT, :] * w[k:k + 1, :]
        o_ref[pl.ds(t0, CONV_TT), :] = acc
        return carry

    lax.fori_loop(0, seq // CONV_TT, body, 0)


def _dwconv(x, w_dw, b_dw, n_batch, seq):
    d = x.shape[1]
    nc = d // CONV_CH
    return pl.pallas_call(
        functools.partial(_dwconv_kernel, seq=seq),
        grid=(n_batch, nc),
        in_specs=[pl.BlockSpec((seq, CONV_CH), lambda b, c: (b, c)),
                  pl.BlockSpec((CONV_WIDTH, CONV_CH), lambda b, c: (0, c)),
                  pl.BlockSpec((1, CONV_CH), lambda b, c: (0, c))],
        out_specs=pl.BlockSpec((seq, CONV_CH), lambda b, c: (b, c)),
        out_shape=jax.ShapeDtypeStruct((n_batch * seq, d), F32),
        scratch_shapes=[pltpu.VMEM((seq + 4 * SUBLANES, CONV_CH), F32)],
        compiler_params=_params(("parallel", "parallel")),
        name="dwconv",
    )(x, w_dw, b_dw.reshape(1, d))


def _proj_ln_kernel(*refs, pre_ln, has_bias):
    it = iter(refs)
    a_ref = next(it)
    pre_g = next(it) if pre_ln else None
    pre_b = next(it) if pre_ln else None
    w_ref = next(it)
    bias_ref = next(it) if has_bias else None
    h_ref, gate_ref, lng_ref, lnb_ref, scf_ref, shf_ref, wr_ref = (next(it) for _ in range(7))
    hnew_ref, fin_ref, score_ref = (next(it) for _ in range(3))

    a = a_ref[...]
    if pre_ln:
        a = _silu(_layer_norm(a, pre_g[...], pre_b[...]))
    m = jnp.dot(a.astype(BF16), w_ref[...], preferred_element_type=F32)
    if has_bias:
        m = m + bias_ref[...]
    hn = _layer_norm(DEEPNORM_ALPHA * h_ref[...] + gate_ref[0] * m, lng_ref[...], lnb_ref[...])
    hnew_ref[...] = hn
    f = hn * (1.0 + scf_ref[0]) + shf_ref[0]
    _store_packed_rows(fin_ref, f)
    score_ref[...] = jax.nn.sigmoid(jnp.dot(f.astype(BF16), wr_ref[...], preferred_element_type=F32))


def _proj_ln(a, w, bias, h, gate, ln_g, ln_b, sc_f, sh_f, w_router, n_rows, seq, pre_ln=None):
    d = D_MODEL
    tm = ROW_TILE
    grp = _group_of_tile(tm, seq)
    row = lambda i: (i, 0)
    const = lambda i: (0, 0)
    grp3 = lambda i: (grp(i), 0, 0)
    args, specs = [a], [pl.BlockSpec((tm, d), row)]
    if pre_ln is not None:
        args += [pre_ln[0].reshape(1, d), pre_ln[1].reshape(1, d)]
        specs += [pl.BlockSpec((1, d), const)] * 2
    args.append(w)
    specs.append(pl.BlockSpec((d, d), const))
    if bias is not None:
        args.append(bias.reshape(1, d))
        specs.append(pl.BlockSpec((1, d), const))
    args += [h, gate, ln_g.reshape(1, d), ln_b.reshape(1, d), sc_f, sh_f, w_router]
    specs += [pl.BlockSpec((tm, d), row), pl.BlockSpec((1, 1, d), grp3),
              pl.BlockSpec((1, d), const), pl.BlockSpec((1, d), const),
              pl.BlockSpec((1, 1, d), grp3), pl.BlockSpec((1, 1, d), grp3),
              pl.BlockSpec((d, N_EXPERTS), const)]
    return pl.pallas_call(
        functools.partial(_proj_ln_kernel, pre_ln=pre_ln is not None, has_bias=bias is not None),
        grid=(n_rows // tm,),
        in_specs=specs,
        out_specs=[pl.BlockSpec((tm, d), row), pl.BlockSpec((tm * SUBLANES, LANES), row),
                   pl.BlockSpec((tm, N_EXPERTS), row)],
        out_shape=[jax.ShapeDtypeStruct((n_rows, d), F32),
                   jax.ShapeDtypeStruct((n_rows * SUBLANES, LANES), jnp.uint32),
                   jax.ShapeDtypeStruct((n_rows, N_EXPERTS), F32)],
        compiler_params=_params(("parallel",)),
        name="proj_ln",
    )(*args)


def _router_kernel(s_ref, b_ref, ek_ref, rk_ref, gk_ref, cnt_ref, carry):
    tm, e = s_ref.shape

    @pl.when(pl.program_id(0) == 0)
    def _():
        carry[...] = jnp.zeros_like(carry)

    s = s_ref[...]
    lane = lax.broadcasted_iota(jnp.int32, (tm, e), 1).astype(F32)
    slot = lax.broadcasted_iota(jnp.int32, (tm, 128), 1)
    work = s + b_ref[...]
    sel = jnp.zeros((tm, e), F32)
    e_acc = jnp.zeros((tm, 128), F32)
    g_acc = jnp.zeros((tm, 128), F32)
    g_sum = jnp.zeros((tm, 1), F32)
    picks = []
    for k in range(TOP_K):
        m = jnp.max(work, axis=-1, keepdims=True)
        idx = jnp.min(jnp.where(work == m, lane, float(e)), axis=-1, keepdims=True)
        onehot = lane == idx
        gk = jnp.sum(jnp.where(onehot, s, 0.0), axis=-1, keepdims=True)
        sel = jnp.where(onehot, 1.0, sel)
        work = jnp.where(onehot, -jnp.inf, work)
        e_acc = jnp.where(slot == k, idx, e_acc)
        g_acc = jnp.where(slot == k, gk, g_acc)
        g_sum = g_sum + gk
        picks.append(idx)

    r_i = lax.broadcasted_iota(jnp.int32, (tm, tm), 0)
    c_i = lax.broadcasted_iota(jnp.int32, (tm, tm), 1)
    tri = jnp.where(c_i < r_i, 1.0, 0.0).astype(BF16)
    rank = jnp.dot(tri, sel.astype(BF16), preferred_element_type=F32) + carry[...]
    r_acc = jnp.zeros((tm, 128), F32)
    for k in range(TOP_K):
        rk = jnp.sum(jnp.where(lane == picks[k], rank, 0.0), axis=-1, keepdims=True)
        r_acc = jnp.where(slot == k, rk, r_acc)

    carry[...] = carry[...] + jnp.sum(sel, axis=0, keepdims=True)
    ek_ref[...] = e_acc[:, :TOP_K].astype(jnp.int32)
    rk_ref[...] = r_acc[:, :TOP_K].astype(jnp.int32)
    gk_ref[...] = (g_acc / g_sum * ROUTED_SCALE)[:, :TOP_K]
    cnt_ref[...] = carry[...].astype(jnp.int32)


def _router(scores, router_bias, n_rows):
    e = N_EXPERTS
    tm = ROW_TILE
    row = lambda i: (i, 0)
    return pl.pallas_call(
        _router_kernel,
        grid=(n_rows // tm,),
        in_specs=[pl.BlockSpec((tm, e), row), pl.BlockSpec((1, e), lambda i: (0, 0))],
        out_specs=[pl.BlockSpec((tm, TOP_K), row), pl.BlockSpec((tm, TOP_K), row), pl.BlockSpec((tm, TOP_K), row),
                   pl.BlockSpec((1, e), lambda i: (0, 0))],
        out_shape=[jax.ShapeDtypeStruct((n_rows, TOP_K), jnp.int32), jax.ShapeDtypeStruct((n_rows, TOP_K), jnp.int32),
                   jax.ShapeDtypeStruct((n_rows, TOP_K), F32), jax.ShapeDtypeStruct((1, e), jnp.int32)],
        scratch_shapes=[pltpu.VMEM((1, e), F32)],
        compiler_params=_params(("arbitrary",)),
        name="router",
    )(scores, router_bias.reshape(1, e).astype(F32))


def _layout(e_k, r_k, counts, tm_e):
    e = N_EXPERTS
    n_assign = e_k.size
    counts = counts.reshape(e)
    padded = (counts + tm_e - 1) // tm_e * tm_e
    pends = jnp.cumsum(padded)
    pstarts = pends - padded
    experts = jnp.arange(e, dtype=jnp.int32)
    start_of = jnp.sum(jnp.where(e_k[..., None] == experts, pstarts, 0), axis=-1)
    dest = (start_of + r_k).reshape(n_assign).astype(jnp.int32)
    n_blocks = (n_assign + e * (tm_e - 1) + tm_e - 1) // tm_e
    blk_row0 = jnp.arange(n_blocks, dtype=jnp.int32) * tm_e
    blk_e = jnp.minimum(jnp.sum(pends[None, :] <= blk_row0[:, None], axis=-1), e - 1).astype(jnp.int32)
    n_valid = (pends[-1] // tm_e).astype(jnp.int32).reshape(1)
    pad_info = jnp.concatenate([pstarts + counts, padded - counts]).astype(jnp.int32)
    later = (experts[None, :] > experts[:, None]) & (counts[None, :] > 0)
    next_e = jnp.min(jnp.where(later, experts[None, :], e), axis=1)
    next_e = jnp.where(next_e == e, -1, next_e)
    order_e = jnp.sum((experts[None, :] < experts[:, None]) & (counts[None, :] > 0), axis=1)
    chain = jnp.concatenate([next_e, order_e]).astype(jnp.int32)
    return dest, blk_e, n_valid, pad_info, chain, n_blocks


def _dispatch_kernel(pad_ref, dest_ref, f_ref, xs_ref, zbuf, sem, zsem):
    tm = f_ref.shape[0] // SUBLANES
    pad_bits = EXPERT_TILE.bit_length() - 1

    def zero_copies(e_idx):
        start = pad_ref[e_idx]
        pad = pad_ref[N_EXPERTS + e_idx]
        return [(((pad >> b) & 1) == 1, pltpu.make_async_copy(
            _row_tile(zbuf, 0, 1 << b), _row_tile(xs_ref, start + (pad & ((1 << b) - 1)), 1 << b), zsem))
            for b in range(pad_bits)]

    @pl.when(pl.program_id(0) == 0)
    def _():
        zbuf[...] = jnp.zeros_like(zbuf)

        def start_e(e_idx, carry):
            for cond, cp in zero_copies(e_idx):
                pl.when(cond)(cp.start)
            return carry

        def wait_e(e_idx, carry):
            for cond, cp in zero_copies(e_idx):
                pl.when(cond)(cp.wait)
            return carry

        lax.fori_loop(0, N_EXPERTS, start_e, 0)
        lax.fori_loop(0, N_EXPERTS, wait_e, 0)

    def tok(t, carry):
        for k in range(TOP_K):
            d = dest_ref[t * TOP_K + k]
            pltpu.make_async_copy(_row_tile(f_ref, t), _row_tile(xs_ref, d), sem).start(priority=k % 2)
        return carry

    lax.fori_loop(0, tm, tok, 0)
    for k in range(TOP_K):
        pltpu.make_async_copy(f_ref, _row_tile(xs_ref, 0, tm), sem).wait()


def _dispatch(f_in, dest, pad_info, n_rows, n_blocks):
    tm = ROW_TILE
    grid_spec = pltpu.PrefetchScalarGridSpec(
        num_scalar_prefetch=1,
        grid=(n_rows // tm,),
        in_specs=[pl.BlockSpec((tm * TOP_K,), lambda i, p: (i,), memory_space=pltpu.SMEM),
                  pl.BlockSpec((tm * SUBLANES, LANES), lambda i, p: (i, 0))],
        out_specs=pl.BlockSpec(memory_space=pl.ANY),
        scratch_shapes=[pltpu.VMEM((EXPERT_TILE // 2 * SUBLANES, LANES), f_in.dtype), pltpu.SemaphoreType.DMA,
                        pltpu.SemaphoreType.DMA],
    )
    return pl.pallas_call(
        _dispatch_kernel,
        grid_spec=grid_spec,
        out_shape=jax.ShapeDtypeStruct((n_blocks * EXPERT_TILE * SUBLANES, LANES), f_in.dtype),
        compiler_params=_params(("arbitrary",)),
        name="dispatch",
    )(pad_info, dest, f_in)


def _swiglu_hidden(x, wg_ref, wu_ref):
    gv = jnp.dot(x, wg_ref[...], preferred_element_type=F32)
    uv = jnp.dot(x, wu_ref[...], preferred_element_type=F32)
    return (_silu(gv) * uv).astype(BF16)


def _expert_kernel(be_ref, nv_ref, chain_ref, x_ref, wg_hbm, wu_hbm, wd_hbm, y_ref,
                   wg32, wu32, wd32, wgb, wub, wdb, sems, *, layer):
    i = pl.program_id(0)
    valid = i < nv_ref[0]
    e_cur = be_ref[i]
    first = jnp.logical_or(i == 0, e_cur != be_ref[jnp.maximum(i - 1, 0)])
    d, de = wgb.shape
    n_chunks = 8

    def weight_copies(e_idx, slot):
        return [pltpu.make_async_copy(w_hbm.at[layer, e_idx], buf.at[slot], sems.at[slot, j])
                for j, (w_hbm, buf) in enumerate(((wg_hbm, wg32), (wu_hbm, wu32), (wd_hbm, wd32)))]

    @pl.when(jnp.logical_and(valid, first))
    def _():
        slot = chain_ref[N_EXPERTS + e_cur] & 1
        e_next = chain_ref[e_cur]

        @pl.when(i == 0)
        def _():
            for cp in weight_copies(e_cur, slot):
                cp.start()

        for cp in weight_copies(e_cur, slot):
            cp.wait()

        @pl.when(e_next >= 0)
        def _():
            for cp in weight_copies(e_next, 1 - slot):
                cp.start()

        def cast_chunk(c, carry):
            r0 = pl.multiple_of(c * (d // n_chunks), d // n_chunks)
            wgb[pl.ds(r0, d // n_chunks), :] = wg32[slot, pl.ds(r0, d // n_chunks), :].astype(BF16)
            wub[pl.ds(r0, d // n_chunks), :] = wu32[slot, pl.ds(r0, d // n_chunks), :].astype(BF16)
            r1 = pl.multiple_of(c * (de // n_chunks), de // n_chunks)
            wdb[pl.ds(r1, de // n_chunks), :] = wd32[slot, pl.ds(r1, de // n_chunks), :].astype(BF16)
            return carry

        lax.fori_loop(0, n_chunks, cast_chunk, 0)

    @pl.when(valid)
    def _():
        act = _swiglu_hidden(_load_packed_rows_bf16(x_ref), wgb, wub)
        _store_packed_rows(y_ref, jnp.dot(act, wdb[...], preferred_element_type=F32))


def _expert_ffn(xs, blk_e, n_valid, chain, w_gate, w_up, w_down, layer):
    n_rows = xs.shape[0] // SUBLANES
    d, de = w_gate.shape[2], w_gate.shape[3]
    tm = EXPERT_TILE
    n_blocks = n_rows // tm

    def xmap(i, be, nv, ch):
        return (jnp.minimum(i, nv[0] - 1), 0)

    grid_spec = pltpu.PrefetchScalarGridSpec(
        num_scalar_prefetch=3,
        grid=(n_blocks,),
        in_specs=[pl.BlockSpec((tm * SUBLANES, LANES), xmap),
                  pl.BlockSpec(memory_space=pl.ANY),
                  pl.BlockSpec(memory_space=pl.ANY),
                  pl.BlockSpec(memory_space=pl.ANY)],
        out_specs=pl.BlockSpec((tm * SUBLANES, LANES), xmap),
        scratch_shapes=[pltpu.VMEM((2, d, de), F32), pltpu.VMEM((2, d, de), F32), pltpu.VMEM((2, de, d), F32),
                        pltpu.VMEM((d, de), BF16), pltpu.VMEM((d, de), BF16), pltpu.VMEM((de, d), BF16),
                        pltpu.SemaphoreType.DMA((2, 3))],
    )
    return pl.pallas_call(
        functools.partial(_expert_kernel, layer=layer),
        grid_spec=grid_spec,
        out_shape=jax.ShapeDtypeStruct((n_rows * SUBLANES, LANES), jnp.uint32),
        compiler_params=_params(("arbitrary",)),
        name="expert_ffn",
    )(blk_e, n_valid, chain, xs, w_gate, w_up, w_down)


def _ffn_out_kernel(dcur_ref, dnext_ref, f_ref, gk_ref, wg_ref, wu_ref, wd_ref, h_ref, gate_ref, lng_ref, lnb_ref,
                    y_ref, o_ref, ybuf, sems):
    i = pl.program_id(0)
    n_steps = pl.num_programs(0)
    tm = f_ref.shape[0] // SUBLANES
    slot = lax.rem(i, 2)

    def gather_token(dref, s, t):
        for k in range(TOP_K):
            row = dref[t * TOP_K + k]
            pltpu.make_async_copy(_row_tile(y_ref, row), _row_tile(ybuf.at[s, k], t),
                                  sems.at[s]).start(priority=k % 2)

    def wait_gathers(s):
        for k in range(TOP_K):
            pltpu.make_async_copy(_row_tile(y_ref, 0, tm), ybuf.at[s, k], sems.at[s]).wait()

    @pl.when(i == 0)
    def _():
        def tok(t, carry):
            gather_token(dcur_ref, slot, t)
            return carry

        lax.fori_loop(0, tm, tok, 0)

    n_pieces = 2 + SUBLANES
    piece = [(p * tm) // n_pieces for p in range(n_pieces + 1)]

    def issue_piece(p):
        for t in range(piece[p], piece[p + 1]):
            gather_token(dnext_ref, 1 - slot, t)

    issue_piece(0)
    act = _swiglu_hidden(_load_packed_rows_bf16(f_ref), wg_ref, wu_ref)
    issue_piece(1)
    f = jnp.dot(act, wd_ref[...], preferred_element_type=F32)

    gk = gk_ref[...]
    wait_gathers(slot)
    gcols = [jnp.broadcast_to(gk[:, k:k + 1], (tm, LANES)) for k in range(TOP_K)]
    parts = []
    for s_ in range(SUBLANES):
        issue_piece(2 + s_)
        r_lo = jnp.zeros((tm, LANES), F32)
        r_hi = jnp.zeros((tm, LANES), F32)
        for k in range(TOP_K):
            y_lo, y_hi = _load_packed_group(ybuf.at[slot, k], s_)
            r_lo = r_lo + gcols[k] * y_lo
            r_hi = r_hi + gcols[k] * y_hi
        parts += [r_lo, r_hi]
    f = f + jnp.concatenate(parts, axis=1)
    o_ref[...] = _layer_norm(DEEPNORM_ALPHA * h_ref[...] + gate_ref[0] * f, lng_ref[...], lnb_ref[...])

    @pl.when(i == n_steps - 1)
    def _():
        wait_gathers(1 - slot)


def _ffn_out(f_in, y, dest, gates, ws_gate, ws_up, ws_down, h, gate, ln_g, ln_b, n_rows, seq):
    d = D_MODEL
    ds_ = ws_gate.shape[1]
    tm = COMBINE_TILE
    n_steps = n_rows // tm
    grp = _group_of_tile(tm, seq)
    row = lambda i: (i, 0)
    const = lambda i: (0, 0)
    return pl.pallas_call(
        _ffn_out_kernel,
        grid=(n_steps,),
        in_specs=[pl.BlockSpec((tm * TOP_K,), lambda i: (i,), memory_space=pltpu.SMEM),
                  pl.BlockSpec((tm * TOP_K,), lambda i: (jnp.minimum(i + 1, n_steps - 1),), memory_space=pltpu.SMEM),
                  pl.BlockSpec((tm * SUBLANES, LANES), row), pl.BlockSpec((tm, TOP_K), row),
                  pl.BlockSpec((d, ds_), const), pl.BlockSpec((d, ds_), const), pl.BlockSpec((ds_, d), const),
                  pl.BlockSpec((tm, d), row), pl.BlockSpec((1, 1, d), lambda i: (grp(i), 0, 0)),
                  pl.BlockSpec((1, d), const), pl.BlockSpec((1, d), const),
                  pl.BlockSpec(memory_space=pl.ANY)],
        out_specs=pl.BlockSpec((tm, d), row),
        out_shape=jax.ShapeDtypeStruct((n_rows, d), F32),
        scratch_shapes=[pltpu.VMEM((2, TOP_K, tm * SUBLANES, LANES), jnp.uint32), pltpu.SemaphoreType.DMA((2,))],
        compiler_params=_params(("arbitrary",)),
        name="ffn_out",
    )(dest, dest, f_in, gates, ws_gate, ws_up, ws_down, h, gate, ln_g.reshape(1, d), ln_b.reshape(1, d), y)


def _moe_and_norm(h, f_in, scores, gate_f, router_bias, we_gate, we_up, we_down, ws_gate, ws_up, ws_down,
                  ln_g, ln_b, n_rows, seq, layer):
    e_k, r_k, gates, counts = _router(scores, router_bias, n_rows)
    dest, blk_e, n_valid, pad_info, chain, n_blocks = _layout(e_k, r_k, counts, EXPERT_TILE)
    xs = _dispatch(f_in, dest, pad_info, n_rows, n_blocks)
    y = _expert_ffn(xs, blk_e, n_valid, chain, we_gate, we_up, we_down, layer)
    return _ffn_out(f_in, y, dest, gates, ws_gate.astype(BF16), ws_up.astype(BF16), ws_down.astype(BF16),
                    h, gate_f, ln_g, ln_b, n_rows, seq)


def kernel(x, c, ctx, c_ctx, mod_w, mod_b, ln_mix_g, ln_mix_b, ln_ffn_g, ln_ffn_b, na_w_qkv, na_rpb, na_w_o,
           cv_w_pw1, cv_b_pw1, cv_w_dw, cv_b_dw, cv_ln_g, cv_ln_b, cv_w_pw2, cv_b_pw2, moe_w_router,
           moe_router_bias, moe_w_gate, moe_w_up, moe_w_down, sh_w_gate, sh_w_up, sh_w_down):
    n_batch, seq, d = x.shape
    ctx_len = ctx.shape[1]
    n_lat = n_batch * seq
    n_tok = n_lat + n_batch * ctx_len

    cvec = jnp.zeros((8, d), F32).at[:n_batch].set(c).at[n_batch].set(c_ctx)
    mod_all = _mod_matmul(cvec, mod_w, mod_b).reshape(DEPTH, 8, 6, d)[:, :n_batch + 1]
    mods = [[mod_all[i, :, p, :].reshape(n_batch + 1, 1, d) for p in range(6)] for i in range(DEPTH)]

    h = jnp.concatenate([x.reshape(n_lat, d), ctx.reshape(n_batch * ctx_len, d)], axis=0)

    sh_a, sc_a, g_a, sh_f, sc_f, g_f = mods[0]
    qkv = _mod_matmul_qkv(h, sc_a, sh_a, na_w_qkv[0].astype(BF16), n_tok, seq)
    attn = _na_attention(qkv, _na_bias_table(na_rpb[0]), n_batch, seq, ctx_len, n_tok)
    attn = _ctx_attention(qkv, attn, n_batch, seq, ctx_len)
    h, f_in, scores = _proj_ln(attn, na_w_o[0].astype(BF16), None, h, g_a, ln_mix_g[0], ln_mix_b[0],
                               sc_f, sh_f, moe_w_router[0].astype(BF16), n_tok, seq)
    h = _moe_and_norm(h, f_in, scores, g_f, moe_router_bias[0], moe_w_gate, moe_w_up, moe_w_down,
                      sh_w_gate[0], sh_w_up[0], sh_w_down[0], ln_ffn_g[0], ln_ffn_b[0], n_tok, seq, 0)

    sh_a, sc_a, g_a, sh_f, sc_f, g_f = mods[1]
    glu = _mod_matmul_glu(h, sc_a, sh_a, cv_w_pw1[0].astype(BF16), cv_b_pw1[0], n_lat, seq)
    conv = _dwconv(glu, cv_w_dw[0], cv_b_dw[0], n_batch, seq)
    h, f_in, scores = _proj_ln(conv, cv_w_pw2[0].astype(BF16), cv_b_pw2[0], h, g_a, ln_mix_g[1], ln_mix_b[1],
                               sc_f, sh_f, moe_w_router[1].astype(BF16), n_lat, seq,
                               pre_ln=(cv_ln_g[0], cv_ln_b[0]))
    h = _moe_and_norm(h, f_in, scores, g_f, moe_router_bias[1], moe_w_gate, moe_w_up, moe_w_down,
                      sh_w_gate[1], sh_w_up[1], sh_w_down[1], ln_ffn_g[1], ln_ffn_b[1], n_lat, seq, 1)
    return h.reshape(n_batch, seq, d)
```

```python
import functools

import jax
import jax.numpy as jnp
from jax import lax
from jax.experimental import pallas as pl
from jax.experimental.pallas import tpu as pltpu

D_MODEL = 2048
GRID_W = 64
N_HEADS = 16
HEAD_DIM = D_MODEL // N_HEADS
WIN_ROWS = 8
WIN_COLS = 16
CONV_WIDTH = 31
N_EXPERTS = 64
TOP_K = 8
ROUTED_SCALE = 2.5
LN_EPS = 1e-6
SUBLANES = 8
NEG_INF = -1e30
DEPTH = 2
DEEPNORM_ALPHA = (2 * DEPTH) ** 0.25

F32 = jnp.float32
BF16 = jnp.bfloat16

VMEM_LIMIT_BYTES = 56 * 1024 * 1024
ROW_TILE = 256
MM_ROW_TILE = 512
EXPERT_TILE = 512
COMBINE_TILE = 128
ROWS_PER_STEP = 8
CONV_CH = 256
CONV_TT = 128


def _params(sem):
    return pltpu.CompilerParams(dimension_semantics=sem, vmem_limit_bytes=VMEM_LIMIT_BYTES)


def _layer_norm(z, g, b):
    mu = jnp.mean(z, axis=-1, keepdims=True)
    zc = z - mu
    var = jnp.mean(zc * zc, axis=-1, keepdims=True)
    return zc * lax.rsqrt(var + LN_EPS) * g + b


def _silu(x):
    return x * jax.nn.sigmoid(x)


LANES = 128
ROW_WORDS = D_MODEL // 2
assert ROW_WORDS == SUBLANES * LANES


def _store_packed_rows(ref, x):
    rows = x.shape[0]
    for s_ in range(SUBLANES):
        base = 2 * LANES * s_
        lo = pltpu.bitcast(x[:, base:base + LANES].astype(BF16).astype(F32), jnp.uint32)
        hi = pltpu.bitcast(x[:, base + LANES:base + 2 * LANES].astype(BF16).astype(F32), jnp.uint32)
        ref[pl.ds(s_, rows, stride=SUBLANES), :] = (lo >> 16) | (hi & jnp.uint32(0xFFFF0000))


def _load_packed_group(ref, s_):
    w = ref[pl.ds(s_, ref.shape[0] // SUBLANES, stride=SUBLANES), :]
    return pltpu.bitcast(w << 16, F32), pltpu.bitcast(w & jnp.uint32(0xFFFF0000), F32)


def _row_tile(ref, t, n=1):
    return ref.at[pl.ds(pl.multiple_of(t * SUBLANES, SUBLANES), n * SUBLANES)]


def _load_packed_rows_bf16(ref):
    parts = []
    for s_ in range(SUBLANES):
        lo, hi = _load_packed_group(ref, s_)
        parts += [lo.astype(BF16), hi.astype(BF16)]
    return jnp.concatenate(parts, axis=1)


def _mod_kernel(c_ref, w_ref, b_ref, o_ref):
    a = _silu(c_ref[...]).astype(BF16)
    o_ref[0] = jnp.dot(a, w_ref[0].astype(BF16), preferred_element_type=F32) + b_ref[0]


def _mod_matmul(cvec, w, b):
    m, k = cvec.shape
    layers, _, n = w.shape
    tn = 1024
    return pl.pallas_call(
        _mod_kernel,
        grid=(layers, n // tn),
        in_specs=[pl.BlockSpec((m, k), lambda l, j: (0, 0)),
                  pl.BlockSpec((1, k, tn), lambda l, j: (l, 0, j)),
                  pl.BlockSpec((1, 1, tn), lambda l, j: (l, 0, j))],
        out_specs=pl.BlockSpec((1, m, tn), lambda l, j: (l, 0, j)),
        out_shape=jax.ShapeDtypeStruct((layers, m, n), F32),
        compiler_params=_params(("parallel", "arbitrary")),
        name="mod_matmul",
    )(cvec, w, b.reshape(layers, 1, n))


def _modmm_kernel(x_ref, c_ref, sc_ref, sh_ref, w_ref, o_ref, a_scr, *, n_lat_tiles):
    @pl.when(pl.program_id(1) == 0)
    def _():
        h = jnp.where(pl.program_id(0) < n_lat_tiles, x_ref[...], c_ref[...])
        a_scr[...] = (h * (1.0 + sc_ref[0]) + sh_ref[0]).astype(BF16)

    o_ref[...] = jnp.dot(a_scr[...], w_ref[...], preferred_element_type=F32).astype(o_ref.dtype)


def _modglu_kernel(h_ref, sc_ref, sh_ref, wa_ref, wg_ref, ba_ref, bg_ref, o_ref, a_scr):
    @pl.when(pl.program_id(1) == 0)
    def _():
        a_scr[...] = (h_ref[...] * (1.0 + sc_ref[0]) + sh_ref[0]).astype(BF16)

    a = a_scr[...]
    va = jnp.dot(a, wa_ref[...], preferred_element_type=F32) + ba_ref[...]
    vg = jnp.dot(a, wg_ref[...], preferred_element_type=F32) + bg_ref[...]
    o_ref[...] = va * jax.nn.sigmoid(vg)


MOD_ROWS = 8
MOD_PIECES = 6
SH_A, SC_A, G_A, SH_F, SC_F, G_F = range(MOD_PIECES)


def _mod_spec(layer, piece, tile_rows, seq):
    def index(i, *_):
        return ((layer * MOD_ROWS + (i * tile_rows) // seq) * MOD_PIECES + piece, 0, 0)

    return pl.BlockSpec((1, 1, D_MODEL), index)


def _mod_matmul_qkv(x_lat, x_ctx, mods, layer, w, seq):
    n_lat, k = x_lat.shape
    n_rows = n_lat + x_ctx.shape[0]
    n = w.shape[1]
    tm, tn = MM_ROW_TILE, 2048
    n_lat_tiles = n_lat // tm
    return pl.pallas_call(
        functools.partial(_modmm_kernel, n_lat_tiles=n_lat_tiles),
        grid=(n_rows // tm, n // tn),
        in_specs=[pl.BlockSpec((tm, k), lambda i, j: (jnp.minimum(i, n_lat_tiles - 1), 0)),
                  pl.BlockSpec((tm, k), lambda i, j: (jnp.maximum(i - n_lat_tiles, 0), 0)),
                  _mod_spec(layer, SC_A, tm, seq), _mod_spec(layer, SH_A, tm, seq),
                  pl.BlockSpec((k, tn), lambda i, j: (0, j))],
        out_specs=pl.BlockSpec((tm, tn), lambda i, j: (i, j)),
        out_shape=jax.ShapeDtypeStruct((n_rows, n), BF16),
        scratch_shapes=[pltpu.VMEM((tm, k), BF16)],
        compiler_params=_params(("parallel", "arbitrary")),
        name="mod_qkv",
    )(x_lat, x_ctx, mods, mods, w)


def _mod_matmul_glu(h, mods, layer, w, b, n_rows, seq):
    k = h.shape[1]
    n = w.shape[1] // 2
    tm, tn = MM_ROW_TILE, 1024
    nj = n // tn
    b2 = b.reshape(1, 2 * n)
    return pl.pallas_call(
        _modglu_kernel,
        grid=(n_rows // tm, nj),
        in_specs=[pl.BlockSpec((tm, k), lambda i, j: (i, 0)),
                  _mod_spec(layer, SC_A, tm, seq), _mod_spec(layer, SH_A, tm, seq),
                  pl.BlockSpec((k, tn), lambda i, j: (0, j)),
                  pl.BlockSpec((k, tn), lambda i, j: (0, j + nj)),
                  pl.BlockSpec((1, tn), lambda i, j: (0, j)),
                  pl.BlockSpec((1, tn), lambda i, j: (0, j + nj))],
        out_specs=pl.BlockSpec((tm, tn), lambda i, j: (i, j)),
        out_shape=jax.ShapeDtypeStruct((n_rows, n), F32),
        scratch_shapes=[pltpu.VMEM((tm, k), BF16)],
        compiler_params=_params(("parallel", "arbitrary")),
        name="mod_pw1_glu",
    )(h, mods, mods, w, w, b2, b2)


def _na_window_start(g, rows):
    return jnp.clip(g * ROWS_PER_STEP - WIN_ROWS // 2, 0, rows - 2 * WIN_ROWS)


def _na_kernel(q_ref, k_ref, v_ref, kc_ref, vc_ref, bias_ref, o_ref, *, rows):
    scale = HEAD_DIM ** -0.5
    contract_last = (((1,), (1,)), ((), ()))
    k_start = pl.multiple_of(_na_window_start(pl.program_id(2), rows) * GRID_W, GRID_W)
    n_keys = 2 * WIN_ROWS * GRID_W
    q = q_ref[...]
    kw = k_ref[pl.ds(k_start, n_keys), :]
    vw = v_ref[pl.ds(k_start, n_keys), :]
    s_loc = lax.dot_general(q, kw, contract_last, preferred_element_type=F32) * scale + bias_ref[0, 0]
    s_ctx = lax.dot_general(q, kc_ref[...], contract_last, preferred_element_type=F32) * scale
    m = jnp.maximum(jnp.max(s_loc, axis=-1, keepdims=True), jnp.max(s_ctx, axis=-1, keepdims=True))
    p_loc = jnp.exp(s_loc - m)
    p_ctx = jnp.exp(s_ctx - m)
    denom = jnp.sum(p_loc, axis=-1, keepdims=True) + jnp.sum(p_ctx, axis=-1, keepdims=True)
    o = (jnp.dot(p_loc.astype(BF16), vw, preferred_element_type=F32)
         + jnp.dot(p_ctx.astype(BF16), vc_ref[...], preferred_element_type=F32))
    o_ref[...] = (o / denom).astype(o_ref.dtype)


def _na_group_kinds(rows):
    groups = rows // ROWS_PER_STEP
    return [0, 1, groups - 1], lambda g: jnp.where(g == 0, 0, jnp.where(g == groups - 1, 2, 1))


def _na_group_bias(row_table, rows):
    n_heads = row_table.shape[0]
    kinds, _ = _na_group_kinds(rows)
    per_kind = []
    for g in kinds:
        ws = min(max(g * ROWS_PER_STEP - WIN_ROWS // 2, 0), rows - 2 * WIN_ROWS)
        per_row = []
        for rr in range(ROWS_PER_STEP):
            r = g * ROWS_PER_STEP + rr
            rs = min(max(r - WIN_ROWS // 2, 0), rows - WIN_ROWS)
            before = jnp.full((n_heads, GRID_W, (rs - ws) * GRID_W), NEG_INF, F32)
            after = jnp.full((n_heads, GRID_W, (WIN_ROWS - (rs - ws)) * GRID_W), NEG_INF, F32)
            per_row.append(jnp.concatenate([before, row_table[:, r - rs], after], axis=-1))
        per_kind.append(jnp.concatenate(per_row, axis=1))
    return jnp.stack(per_kind, axis=1)


def _na_bias_table(rpb):
    col = jnp.arange(GRID_W)
    cstart = jnp.clip(col - WIN_COLS // 2, 0, GRID_W - WIN_COLS)
    col_ok = (col[None, :] >= cstart[:, None]) & (col[None, :] < cstart[:, None] + WIN_COLS)
    cidx = jnp.clip(col[None, :] - col[:, None] + WIN_COLS - 1, 0, 2 * WIN_COLS - 2)
    onehot = (cidx[:, :, None] == jnp.arange(2 * WIN_COLS - 1)).astype(F32)
    t = jnp.einsum('hrc,qkc->hrqk', rpb.astype(F32), onehot, precision=lax.Precision.HIGHEST)
    t = jnp.where(col_ok[None, None], t, NEG_INF)
    t = jnp.stack([t[:, WIN_ROWS - 1 - off:2 * WIN_ROWS - 1 - off] for off in range(WIN_ROWS)], axis=1)
    t = t.transpose(0, 1, 3, 2, 4)
    return t.reshape(rpb.shape[0], WIN_ROWS, GRID_W, WIN_ROWS * GRID_W)


def _na_attention(qkv, bias_table, n_batch, seq, ctx_len, n_rows_out):
    rows = seq // GRID_W
    groups = rows // ROWS_PER_STEP
    tq = ROWS_PER_STEP * GRID_W
    ctx_blk0 = (n_batch * seq) // ctx_len
    h_ = N_HEADS
    _, kind_of = _na_group_kinds(rows)
    group_bias = _na_group_bias(bias_table, rows)
    return pl.pallas_call(
        functools.partial(_na_kernel, rows=rows),
        grid=(n_batch, h_, groups),
        in_specs=[pl.BlockSpec((tq, HEAD_DIM), lambda b, h, g: (b * groups + g, h)),
                  pl.BlockSpec((seq, HEAD_DIM), lambda b, h, g: (b, h_ + h)),
                  pl.BlockSpec((seq, HEAD_DIM), lambda b, h, g: (b, 2 * h_ + h)),
                  pl.BlockSpec((ctx_len, HEAD_DIM), lambda b, h, g: (ctx_blk0 + b, h_ + h)),
                  pl.BlockSpec((ctx_len, HEAD_DIM), lambda b, h, g: (ctx_blk0 + b, 2 * h_ + h)),
                  pl.BlockSpec((1, 1, tq, 2 * WIN_ROWS * GRID_W), lambda b, h, g: (h, kind_of(g), 0, 0))],
        out_specs=pl.BlockSpec((tq, HEAD_DIM), lambda b, h, g: (b * groups + g, h)),
        out_shape=jax.ShapeDtypeStruct((n_rows_out, D_MODEL), BF16),
        compiler_params=_params(("parallel", "parallel", "arbitrary")),
        name="na_attention",
    )(qkv, qkv, qkv, qkv, qkv, group_bias)


def _ctx_attn_kernel(q_ref, k_ref, v_ref, prev_ref, o_ref):
    del prev_ref
    scale = HEAD_DIM ** -0.5
    s = lax.dot_general(q_ref[...], k_ref[...], (((1,), (1,)), ((), ())), preferred_element_type=F32) * scale
    m = jnp.max(s, axis=-1, keepdims=True)
    p = jnp.exp(s - m)
    denom = jnp.sum(p, axis=-1, keepdims=True)
    o = jnp.dot(p.astype(BF16), v_ref[...], preferred_element_type=F32)
    o_ref[...] = (o / denom).astype(o_ref.dtype)


def _ctx_attention(qkv, attn_out, n_batch, seq, ctx_len):
    ctx_blk0 = (n_batch * seq) // ctx_len
    h_ = N_HEADS
    return pl.pallas_call(
        _ctx_attn_kernel,
        grid=(n_batch, h_),
        in_specs=[pl.BlockSpec((ctx_len, HEAD_DIM), lambda b, h: (ctx_blk0 + b, h)),
                  pl.BlockSpec((ctx_len, HEAD_DIM), lambda b, h: (ctx_blk0 + b, h_ + h)),
                  pl.BlockSpec((ctx_len, HEAD_DIM), lambda b, h: (ctx_blk0 + b, 2 * h_ + h)),
                  pl.BlockSpec(memory_space=pl.ANY)],
        out_specs=pl.BlockSpec((ctx_len, HEAD_DIM), lambda b, h: (ctx_blk0 + b, h)),
        out_shape=jax.ShapeDtypeStruct(attn_out.shape, attn_out.dtype),
        input_output_aliases={3: 0},
        compiler_params=_params(("parallel", "parallel")),
        name="ctx_attention",
    )(qkv, qkv, qkv, attn_out)


def _dwconv_kernel(x_ref, w_ref, b_ref, o_ref, xpad, *, seq):
    half = CONV_WIDTH // 2
    lead = 2 * SUBLANES
    ch = x_ref.shape[1]
    xpad[pl.ds(0, lead), :] = jnp.zeros((lead, ch), F32)
    xpad[pl.ds(lead + seq, lead), :] = jnp.zeros((lead, ch), F32)
    xpad[pl.ds(lead, seq), :] = x_ref[...]
    w = w_ref[...]
    bias = b_ref[...]
    win_rows = CONV_TT + 2 * lead

    def body(c, carry):
        t0 = pl.multiple_of(c * CONV_TT, CONV_TT)
        win = xpad[pl.ds(t0, win_rows), :]
        acc = jnp.broadcast_to(bias, (CONV_TT, ch))
        for p in range(SUBLANES):
            shifted = win if p == 0 else pltpu.roll(win, win_rows - p, axis=0)
            for k in range(CONV_WIDTH):
                o = lead - half + k
                if o % SUBLANES == p:
                    acc = acc + shifted[o - p:o - p + CONV_TT, :] * w[k:k + 1, :]
        o_ref[pl.ds(t0, CONV_TT), :] = acc
        return carry

    lax.fori_loop(0, seq // CONV_TT, body, 0)


def _dwconv(x, w_dw, b_dw, n_batch, seq):
    d = x.shape[1]
    nc = d // CONV_CH
    return pl.pallas_call(
        functools.partial(_dwconv_kernel, seq=seq),
        grid=(n_batch, nc),
        in_specs=[pl.BlockSpec((seq, CONV_CH), lambda b, c: (b, c)),
                  pl.BlockSpec((CONV_WIDTH, CONV_CH), lambda b, c: (0, c)),
                  pl.BlockSpec((1, CONV_CH), lambda b, c: (0, c))],
        out_specs=pl.BlockSpec((seq, CONV_CH), lambda b, c: (b, c)),
        out_shape=jax.ShapeDtypeStruct((n_batch * seq, d), F32),
        scratch_shapes=[pltpu.VMEM((seq + 4 * SUBLANES, CONV_CH), F32)],
        compiler_params=_params(("parallel", "parallel")),
        name="dwconv",
    )(x, w_dw, b_dw.reshape(1, d))


def _proj_ln_kernel(*refs, pre_ln, has_bias, n_lat_tiles):
    it = iter(refs)
    a_ref = next(it)
    pre_g = next(it) if pre_ln else None
    pre_b = next(it) if pre_ln else None
    w_ref = next(it)
    bias_ref = next(it) if has_bias else None
    h_ref = next(it)
    hc_ref = next(it) if n_lat_tiles is not None else None
    gate_ref, lng_ref, lnb_ref, scf_ref, shf_ref, wr_ref = (next(it) for _ in range(6))
    hnew_ref, fin_ref, score_ref = (next(it) for _ in range(3))

    a = a_ref[...]
    if pre_ln:
        a = _silu(_layer_norm(a, pre_g[...], pre_b[...]))
    m = jnp.dot(a.astype(BF16), w_ref[...], preferred_element_type=F32)
    if has_bias:
        m = m + bias_ref[...]
    h_old = h_ref[...]
    if n_lat_tiles is not None:
        h_old = jnp.where(pl.program_id(0) < n_lat_tiles, h_old, hc_ref[...])
    hn = _layer_norm(DEEPNORM_ALPHA * h_old + gate_ref[0] * m, lng_ref[...], lnb_ref[...])
    hnew_ref[...] = hn
    f = hn * (1.0 + scf_ref[0]) + shf_ref[0]
    _store_packed_rows(fin_ref, f)
    score_ref[...] = jax.nn.sigmoid(jnp.dot(f.astype(BF16), wr_ref[...], preferred_element_type=F32))


def _proj_ln(a, w, bias, h, h_ctx, mods, layer, ln_g, ln_b, w_router, n_rows, seq, pre_ln=None):
    d = D_MODEL
    tm = ROW_TILE
    row = lambda i: (i, 0)
    const = lambda i: (0, 0)
    n_lat_tiles = None if h_ctx is None else h.shape[0] // tm
    args, specs = [a], [pl.BlockSpec((tm, d), row)]
    if pre_ln is not None:
        args += [pre_ln[0].reshape(1, d), pre_ln[1].reshape(1, d)]
        specs += [pl.BlockSpec((1, d), const)] * 2
    args.append(w)
    specs.append(pl.BlockSpec((d, d), const))
    if bias is not None:
        args.append(bias.reshape(1, d))
        specs.append(pl.BlockSpec((1, d), const))
    if h_ctx is None:
        args.append(h)
        specs.append(pl.BlockSpec((tm, d), row))
    else:
        args += [h, h_ctx]
        specs += [pl.BlockSpec((tm, d), lambda i: (jnp.minimum(i, n_lat_tiles - 1), 0)),
                  pl.BlockSpec((tm, d), lambda i: (jnp.maximum(i - n_lat_tiles, 0), 0))]
    args += [mods, ln_g.reshape(1, d), ln_b.reshape(1, d), mods, mods, w_router]
    specs += [_mod_spec(layer, G_A, tm, seq), pl.BlockSpec((1, d), const), pl.BlockSpec((1, d), const),
              _mod_spec(layer, SC_F, tm, seq), _mod_spec(layer, SH_F, tm, seq),
              pl.BlockSpec((d, N_EXPERTS), const)]
    return pl.pallas_call(
        functools.partial(_proj_ln_kernel, pre_ln=pre_ln is not None, has_bias=bias is not None,
                          n_lat_tiles=n_lat_tiles),
        grid=(n_rows // tm,),
        in_specs=specs,
        out_specs=[pl.BlockSpec((tm, d), row), pl.BlockSpec((tm * SUBLANES, LANES), row),
                   pl.BlockSpec((tm, N_EXPERTS), row)],
        out_shape=[jax.ShapeDtypeStruct((n_rows, d), F32),
                   jax.ShapeDtypeStruct((n_rows * SUBLANES, LANES), jnp.uint32),
                   jax.ShapeDtypeStruct((n_rows, N_EXPERTS), F32)],
        compiler_params=_params(("parallel",)),
        name="proj_ln",
    )(*args)


def _router_kernel(s_ref, b_ref, ek_ref, rk_ref, gk_ref, cnt_ref, carry):
    tm, e = s_ref.shape

    @pl.when(pl.program_id(0) == 0)
    def _():
        carry[...] = jnp.zeros_like(carry)

    s = s_ref[...]
    lane = lax.broadcasted_iota(jnp.int32, (tm, e), 1).astype(F32)
    slot = lax.broadcasted_iota(jnp.int32, (tm, 128), 1)
    work = s + b_ref[...]
    sel = jnp.zeros((tm, e), F32)
    e_acc = jnp.zeros((tm, 128), F32)
    g_acc = jnp.zeros((tm, 128), F32)
    g_sum = jnp.zeros((tm, 1), F32)
    picks = []
    for k in range(TOP_K):
        m = jnp.max(work, axis=-1, keepdims=True)
        idx = jnp.min(jnp.where(work == m, lane, float(e)), axis=-1, keepdims=True)
        onehot = lane == idx
        gk = jnp.sum(jnp.where(onehot, s, 0.0), axis=-1, keepdims=True)
        sel = jnp.where(onehot, 1.0, sel)
        work = jnp.where(onehot, -jnp.inf, work)
        e_acc = jnp.where(slot == k, idx, e_acc)
        g_acc = jnp.where(slot == k, gk, g_acc)
        g_sum = g_sum + gk
        picks.append(idx)

    r_i = lax.broadcasted_iota(jnp.int32, (tm, tm), 0)
    c_i = lax.broadcasted_iota(jnp.int32, (tm, tm), 1)
    tri = jnp.where(c_i < r_i, 1.0, 0.0).astype(BF16)
    rank = jnp.dot(tri, sel.astype(BF16), preferred_element_type=F32) + carry[...]
    r_acc = jnp.zeros((tm, 128), F32)
    for k in range(TOP_K):
        rk = jnp.sum(jnp.where(lane == picks[k], rank, 0.0), axis=-1, keepdims=True)
        r_acc = jnp.where(slot == k, rk, r_acc)

    carry[...] = carry[...] + jnp.sum(sel, axis=0, keepdims=True)
    ek_ref[...] = e_acc[:, :TOP_K].astype(jnp.int32)
    rk_ref[...] = r_acc[:, :TOP_K].astype(jnp.int32)
    gk_ref[...] = (g_acc / g_sum * ROUTED_SCALE)[:, :TOP_K]
    cnt_ref[...] = carry[...].astype(jnp.int32)


def _router(scores, router_bias, n_rows):
    e = N_EXPERTS
    tm = ROW_TILE
    row = lambda i: (i, 0)
    return pl.pallas_call(
        _router_kernel,
        grid=(n_rows // tm,),
        in_specs=[pl.BlockSpec((tm, e), row), pl.BlockSpec((1, e), lambda i: (0, 0))],
        out_specs=[pl.BlockSpec((tm, TOP_K), row), pl.BlockSpec((tm, TOP_K), row), pl.BlockSpec((tm, TOP_K), row),
                   pl.BlockSpec((1, e), lambda i: (0, 0))],
        out_shape=[jax.ShapeDtypeStruct((n_rows, TOP_K), jnp.int32), jax.ShapeDtypeStruct((n_rows, TOP_K), jnp.int32),
                   jax.ShapeDtypeStruct((n_rows, TOP_K), F32), jax.ShapeDtypeStruct((1, e), jnp.int32)],
        scratch_shapes=[pltpu.VMEM((1, e), F32)],
        compiler_params=_params(("arbitrary",)),
        name="router",
    )(scores, router_bias.reshape(1, e).astype(F32))


def _layout(e_k, r_k, counts, tm_e):
    e = N_EXPERTS
    n_assign = e_k.size
    counts = counts.reshape(e)
    padded = (counts + tm_e - 1) // tm_e * tm_e
    pends = jnp.cumsum(padded)
    pstarts = pends - padded
    experts = jnp.arange(e, dtype=jnp.int32)
    start_of = jnp.sum(jnp.where(e_k[..., None] == experts, pstarts, 0), axis=-1)
    dest = (start_of + r_k).reshape(n_assign).astype(jnp.int32)
    n_blocks = (n_assign + e * (tm_e - 1) + tm_e - 1) // tm_e
    blk_row0 = jnp.arange(n_blocks, dtype=jnp.int32) * tm_e
    blk_e = jnp.minimum(jnp.sum(pends[None, :] <= blk_row0[:, None], axis=-1), e - 1).astype(jnp.int32)
    n_valid = (pends[-1] // tm_e).astype(jnp.int32).reshape(1)
    pad_info = jnp.concatenate([pstarts + counts, padded - counts]).astype(jnp.int32)
    later = (experts[None, :] > experts[:, None]) & (counts[None, :] > 0)
    next_e = jnp.min(jnp.where(later, experts[None, :], e), axis=1)
    next_e = jnp.where(next_e == e, -1, next_e)
    order_e = jnp.sum((experts[None, :] < experts[:, None]) & (counts[None, :] > 0), axis=1)
    chain = jnp.concatenate([next_e, order_e]).astype(jnp.int32)
    return dest, blk_e, n_valid, pad_info, chain, n_blocks


def _dispatch_kernel(pad_ref, dest_ref, f_ref, xs_ref, zbuf, sem, zsem):
    tm = f_ref.shape[0] // SUBLANES
    pad_bits = EXPERT_TILE.bit_length() - 1

    def zero_copies(e_idx):
        start = pad_ref[e_idx]
        pad = pad_ref[N_EXPERTS + e_idx]
        return [(((pad >> b) & 1) == 1, pltpu.make_async_copy(
            _row_tile(zbuf, 0, 1 << b), _row_tile(xs_ref, start + (pad & ((1 << b) - 1)), 1 << b), zsem))
            for b in range(pad_bits)]

    @pl.when(pl.program_id(0) == 0)
    def _():
        zbuf[...] = jnp.zeros_like(zbuf)

        def start_e(e_idx, carry):
            for cond, cp in zero_copies(e_idx):
                pl.when(cond)(cp.start)
            return carry

        def wait_e(e_idx, carry):
            for cond, cp in zero_copies(e_idx):
                pl.when(cond)(cp.wait)
            return carry

        lax.fori_loop(0, N_EXPERTS, start_e, 0)
        lax.fori_loop(0, N_EXPERTS, wait_e, 0)

    def tok(t, carry):
        for k in range(TOP_K):
            d = dest_ref[t * TOP_K + k]
            pltpu.make_async_copy(_row_tile(f_ref, t), _row_tile(xs_ref, d), sem).start(priority=k % 2)
        return carry

    lax.fori_loop(0, tm, tok, 0)
    for k in range(TOP_K):
        pltpu.make_async_copy(f_ref, _row_tile(xs_ref, 0, tm), sem).wait()


def _dispatch(f_in, dest, pad_info, n_rows, n_blocks):
    tm = ROW_TILE
    grid_spec = pltpu.PrefetchScalarGridSpec(
        num_scalar_prefetch=1,
        grid=(n_rows // tm,),
        in_specs=[pl.BlockSpec((tm * TOP_K,), lambda i, p: (i,), memory_space=pltpu.SMEM),
                  pl.BlockSpec((tm * SUBLANES, LANES), lambda i, p: (i, 0))],
        out_specs=pl.BlockSpec(memory_space=pl.ANY),
        scratch_shapes=[pltpu.VMEM((EXPERT_TILE // 2 * SUBLANES, LANES), f_in.dtype), pltpu.SemaphoreType.DMA,
                        pltpu.SemaphoreType.DMA],
    )
    return pl.pallas_call(
        _dispatch_kernel,
        grid_spec=grid_spec,
        out_shape=jax.ShapeDtypeStruct((n_blocks * EXPERT_TILE * SUBLANES, LANES), f_in.dtype),
        compiler_params=_params(("arbitrary",)),
        name="dispatch",
    )(pad_info, dest, f_in)


def _swiglu_hidden(x, wg_ref, wu_ref):
    gv = jnp.dot(x, wg_ref[...], preferred_element_type=F32)
    uv = jnp.dot(x, wu_ref[...], preferred_element_type=F32)
    return (_silu(gv) * uv).astype(BF16)


def _expert_kernel(be_ref, nv_ref, chain_ref, x_ref, wg_hbm, wu_hbm, wd_hbm, y_ref,
                   wg32, wu32, wd32, wgb, wub, wdb, sems, *, layer):
    i = pl.program_id(0)
    valid = i < nv_ref[0]
    e_cur = be_ref[i]
    first = jnp.logical_or(i == 0, e_cur != be_ref[jnp.maximum(i - 1, 0)])
    d, de = wgb.shape
    n_chunks = 8

    def weight_copies(e_idx, slot):
        return [pltpu.make_async_copy(w_hbm.at[layer, e_idx], buf.at[slot], sems.at[slot, j])
                for j, (w_hbm, buf) in enumerate(((wg_hbm, wg32), (wu_hbm, wu32), (wd_hbm, wd32)))]

    @pl.when(jnp.logical_and(valid, first))
    def _():
        slot = chain_ref[N_EXPERTS + e_cur] & 1
        e_next = chain_ref[e_cur]

        @pl.when(i == 0)
        def _():
            for cp in weight_copies(e_cur, slot):
                cp.start()

        for cp in weight_copies(e_cur, slot):
            cp.wait()

        @pl.when(e_next >= 0)
        def _():
            for cp in weight_copies(e_next, 1 - slot):
                cp.start()

        def cast_chunk(c, carry):
            r0 = pl.multiple_of(c * (d // n_chunks), d // n_chunks)
            wgb[pl.ds(r0, d // n_chunks), :] = wg32[slot, pl.ds(r0, d // n_chunks), :].astype(BF16)
            wub[pl.ds(r0, d // n_chunks), :] = wu32[slot, pl.ds(r0, d // n_chunks), :].astype(BF16)
            r1 = pl.multiple_of(c * (de // n_chunks), de // n_chunks)
            wdb[pl.ds(r1, de // n_chunks), :] = wd32[slot, pl.ds(r1, de // n_chunks), :].astype(BF16)
            return carry

        lax.fori_loop(0, n_chunks, cast_chunk, 0)

    @pl.when(valid)
    def _():
        act = _swiglu_hidden(_load_packed_rows_bf16(x_ref), wgb, wub)
        _store_packed_rows(y_ref, jnp.dot(act, wdb[...], preferred_element_type=F32))


def _expert_ffn(xs, blk_e, n_valid, chain, w_gate, w_up, w_down, layer):
    n_rows = xs.shape[0] // SUBLANES
    d, de = w_gate.shape[2], w_gate.shape[3]
    tm = EXPERT_TILE
    n_blocks = n_rows // tm

    def xmap(i, be, nv, ch):
        return (jnp.minimum(i, nv[0] - 1), 0)

    grid_spec = pltpu.PrefetchScalarGridSpec(
        num_scalar_prefetch=3,
        grid=(n_blocks,),
        in_specs=[pl.BlockSpec((tm * SUBLANES, LANES), xmap),
                  pl.BlockSpec(memory_space=pl.ANY),
                  pl.BlockSpec(memory_space=pl.ANY),
                  pl.BlockSpec(memory_space=pl.ANY)],
        out_specs=pl.BlockSpec((tm * SUBLANES, LANES), xmap),
        scratch_shapes=[pltpu.VMEM((2, d, de), F32), pltpu.VMEM((2, d, de), F32), pltpu.VMEM((2, de, d), F32),
                        pltpu.VMEM((d, de), BF16), pltpu.VMEM((d, de), BF16), pltpu.VMEM((de, d), BF16),
                        pltpu.SemaphoreType.DMA((2, 3))],
    )
    return pl.pallas_call(
        functools.partial(_expert_kernel, layer=layer),
        grid_spec=grid_spec,
        out_shape=jax.ShapeDtypeStruct((n_rows * SUBLANES, LANES), jnp.uint32),
        compiler_params=_params(("arbitrary",)),
        name="expert_ffn",
    )(blk_e, n_valid, chain, xs, w_gate, w_up, w_down)


def _ffn_out_kernel(dcur_ref, dnext_ref, f_ref, gk_ref, wg_ref, wu_ref, wd_ref, h_ref, gate_ref, lng_ref, lnb_ref,
                    y_ref, o_ref, ybuf, sems):
    i = pl.program_id(0)
    n_steps = pl.num_programs(0)
    tm = f_ref.shape[0] // SUBLANES
    slot = lax.rem(i, 2)

    def gather_token(dref, s, t):
        for k in range(TOP_K):
            row = dref[t * TOP_K + k]
            pltpu.make_async_copy(_row_tile(y_ref, row), _row_tile(ybuf.at[s, k], t),
                                  sems.at[s]).start(priority=k % 2)

    def wait_gathers(s):
        for k in range(TOP_K):
            pltpu.make_async_copy(_row_tile(y_ref, 0, tm), ybuf.at[s, k], sems.at[s]).wait()

    @pl.when(i == 0)
    def _():
        def tok(t, carry):
            gather_token(dcur_ref, slot, t)
            return carry

        lax.fori_loop(0, tm, tok, 0)

    n_pieces = 2 + SUBLANES
    piece = [(p * tm) // n_pieces for p in range(n_pieces + 1)]

    def issue_piece(p):
        for t in range(piece[p], piece[p + 1]):
            gather_token(dnext_ref, 1 - slot, t)

    issue_piece(0)
    act = _swiglu_hidden(_load_packed_rows_bf16(f_ref), wg_ref, wu_ref)
    issue_piece(1)
    f = jnp.dot(act, wd_ref[...], preferred_element_type=F32)

    gk = gk_ref[...]
    wait_gathers(slot)
    gcols = [jnp.broadcast_to(gk[:, k:k + 1], (tm, LANES)) for k in range(TOP_K)]
    parts = []
    for s_ in range(SUBLANES):
        issue_piece(2 + s_)
        r_lo = jnp.zeros((tm, LANES), F32)
        r_hi = jnp.zeros((tm, LANES), F32)
        for k in range(TOP_K):
            y_lo, y_hi = _load_packed_group(ybuf.at[slot, k], s_)
            r_lo = r_lo + gcols[k] * y_lo
            r_hi = r_hi + gcols[k] * y_hi
        parts += [r_lo, r_hi]
    f = f + jnp.concatenate(parts, axis=1)
    o_ref[...] = _layer_norm(DEEPNORM_ALPHA * h_ref[...] + gate_ref[0] * f, lng_ref[...], lnb_ref[...])

    @pl.when(i == n_steps - 1)
    def _():
        wait_gathers(1 - slot)


def _ffn_out(f_in, y, dest, gates, ws_gate, ws_up, ws_down, h, mods, layer, ln_g, ln_b, n_rows, seq):
    d = D_MODEL
    ds_ = ws_gate.shape[1]
    tm = COMBINE_TILE
    n_steps = n_rows // tm
    row = lambda i: (i, 0)
    const = lambda i: (0, 0)
    return pl.pallas_call(
        _ffn_out_kernel,
        grid=(n_steps,),
        in_specs=[pl.BlockSpec((tm * TOP_K,), lambda i: (i,), memory_space=pltpu.SMEM),
                  pl.BlockSpec((tm * TOP_K,), lambda i: (jnp.minimum(i + 1, n_steps - 1),), memory_space=pltpu.SMEM),
                  pl.BlockSpec((tm * SUBLANES, LANES), row), pl.BlockSpec((tm, TOP_K), row),
                  pl.BlockSpec((d, ds_), const), pl.BlockSpec((d, ds_), const), pl.BlockSpec((ds_, d), const),
                  pl.BlockSpec((tm, d), row), _mod_spec(layer, G_F, tm, seq),
                  pl.BlockSpec((1, d), const), pl.BlockSpec((1, d), const),
                  pl.BlockSpec(memory_space=pl.ANY)],
        out_specs=pl.BlockSpec((tm, d), row),
        out_shape=jax.ShapeDtypeStruct((n_rows, d), F32),
        scratch_shapes=[pltpu.VMEM((2, TOP_K, tm * SUBLANES, LANES), jnp.uint32), pltpu.SemaphoreType.DMA((2,))],
        compiler_params=_params(("arbitrary",)),
        name="ffn_out",
    )(dest, dest, f_in, gates, ws_gate, ws_up, ws_down, h, mods, ln_g.reshape(1, d), ln_b.reshape(1, d), y)


def _moe_and_norm(h, f_in, scores, mods, router_bias, we_gate, we_up, we_down, ws_gate, ws_up, ws_down,
                  ln_g, ln_b, n_rows, seq, layer):
    e_k, r_k, gates, counts = _router(scores, router_bias, n_rows)
    dest, blk_e, n_valid, pad_info, chain, n_blocks = _layout(e_k, r_k, counts, EXPERT_TILE)
    xs = _dispatch(f_in, dest, pad_info, n_rows, n_blocks)
    y = _expert_ffn(xs, blk_e, n_valid, chain, we_gate, we_up, we_down, layer)
    return _ffn_out(f_in, y, dest, gates, ws_gate.astype(BF16), ws_up.astype(BF16), ws_down.astype(BF16),
                    h, mods, layer, ln_g, ln_b, n_rows, seq)


def kernel(x, c, ctx, c_ctx, mod_w, mod_b, ln_mix_g, ln_mix_b, ln_ffn_g, ln_ffn_b, na_w_qkv, na_rpb, na_w_o,
           cv_w_pw1, cv_b_pw1, cv_w_dw, cv_b_dw, cv_ln_g, cv_ln_b, cv_w_pw2, cv_b_pw2, moe_w_router,
           moe_router_bias, moe_w_gate, moe_w_up, moe_w_down, sh_w_gate, sh_w_up, sh_w_down):
    n_batch, seq, d = x.shape
    ctx_len = ctx.shape[1]
    n_lat = n_batch * seq
    n_tok = n_lat + n_batch * ctx_len

    cvec = jnp.zeros((MOD_ROWS, d), F32).at[:n_batch].set(c).at[n_batch].set(c_ctx)
    mods = _mod_matmul(cvec, mod_w, mod_b).reshape(DEPTH * MOD_ROWS * MOD_PIECES, 1, d)
    x_lat = x.reshape(n_lat, d)
    x_ctx = ctx.reshape(n_batch * ctx_len, d)

    qkv = _mod_matmul_qkv(x_lat, x_ctx, mods, 0, na_w_qkv[0].astype(BF16), seq)
    attn = _na_attention(qkv, _na_bias_table(na_rpb[0]), n_batch, seq, ctx_len, n_tok)
    attn = _ctx_attention(qkv, attn, n_batch, seq, ctx_len)
    h, f_in, scores = _proj_ln(attn, na_w_o[0].astype(BF16), None, x_lat, x_ctx, mods, 0, ln_mix_g[0], ln_mix_b[0],
                               moe_w_router[0].astype(BF16), n_tok, seq)
    h = _moe_and_norm(h, f_in, scores, mods, moe_router_bias[0], moe_w_gate, moe_w_up, moe_w_down,
                      sh_w_gate[0], sh_w_up[0], sh_w_down[0], ln_ffn_g[0], ln_ffn_b[0], n_tok, seq, 0)

    glu = _mod_matmul_glu(h, mods, 1, cv_w_pw1[0].astype(BF16), cv_b_pw1[0], n_lat, seq)
    conv = _dwconv(glu, cv_w_dw[0], cv_b_dw[0], n_batch, seq)
    h, f_in, scores = _proj_ln(conv, cv_w_pw2[0].astype(BF16), cv_b_pw2[0], h, None, mods, 1, ln_mix_g[1],
                               ln_mix_b[1], moe_w_router[1].astype(BF16), n_lat, seq,
                               pre_ln=(cv_ln_g[0], cv_ln_b[0]))
    h = _moe_and_norm(h, f_in, scores, mods, moe_router_bias[1], moe_w_gate, moe_w_up, moe_w_down,
                      sh_w_gate[1], sh_w_up[1], sh_w_down[1], ln_ffn_g[1], ln_ffn_b[1], n_lat, seq, 1)
    return h.reshape(n_batch, seq, d)
```

```python
import functools

import jax
import jax.numpy as jnp
from jax import lax
from jax.experimental import pallas as pl
from jax.experimental.pallas import tpu as pltpu

D_MODEL = 2048
GRID_W = 64
N_HEADS = 16
HEAD_DIM = D_MODEL // N_HEADS
WIN_ROWS = 8
WIN_COLS = 16
CONV_WIDTH = 31
N_EXPERTS = 64
TOP_K = 8
ROUTED_SCALE = 2.5
LN_EPS = 1e-6
SUBLANES = 8
NEG_INF = -1e30
DEPTH = 2
DEEPNORM_ALPHA = (2 * DEPTH) ** 0.25

F32 = jnp.float32
BF16 = jnp.bfloat16

VMEM_LIMIT_BYTES = 56 * 1024 * 1024
ROW_TILE = 256
MM_ROW_TILE = 512
EXPERT_TILE = 512
COMBINE_TILE = 128
ROWS_PER_STEP = 16
NA_BLOCK_ROWS = 2
NA_KEY_ROWS = WIN_ROWS + NA_BLOCK_ROWS
CONV_CH = 256
CONV_TT = 128


def _params(sem):
    return pltpu.CompilerParams(dimension_semantics=sem, vmem_limit_bytes=VMEM_LIMIT_BYTES)


def _layer_norm(z, g, b):
    mu = jnp.mean(z, axis=-1, keepdims=True)
    zc = z - mu
    var = jnp.mean(zc * zc, axis=-1, keepdims=True)
    return zc * lax.rsqrt(var + LN_EPS) * g + b


def _silu(x):
    return x * jax.nn.sigmoid(x)


LANES = 128
ROW_WORDS = D_MODEL // 2
assert ROW_WORDS == SUBLANES * LANES


def _store_packed_rows(ref, x):
    rows = x.shape[0]
    for s_ in range(SUBLANES):
        base = 2 * LANES * s_
        lo = pltpu.bitcast(x[:, base:base + LANES].astype(BF16).astype(F32), jnp.uint32)
        hi = pltpu.bitcast(x[:, base + LANES:base + 2 * LANES].astype(BF16).astype(F32), jnp.uint32)
        ref[pl.ds(s_, rows, stride=SUBLANES), :] = (lo >> 16) | (hi & jnp.uint32(0xFFFF0000))


def _load_packed_group(ref, s_):
    w = ref[pl.ds(s_, ref.shape[0] // SUBLANES, stride=SUBLANES), :]
    return pltpu.bitcast(w << 16, F32), pltpu.bitcast(w & jnp.uint32(0xFFFF0000), F32)


def _row_tile(ref, t, n=1):
    return ref.at[pl.ds(pl.multiple_of(t * SUBLANES, SUBLANES), n * SUBLANES)]


def _load_packed_rows_bf16(ref):
    parts = []
    for s_ in range(SUBLANES):
        lo, hi = _load_packed_group(ref, s_)
        parts += [lo.astype(BF16), hi.astype(BF16)]
    return jnp.concatenate(parts, axis=1)


def _mod_kernel(c_ref, w_ref, b_ref, o_ref):
    a = _silu(c_ref[...]).astype(BF16)
    o_ref[0] = jnp.dot(a, w_ref[0].astype(BF16), preferred_element_type=F32) + b_ref[0]


def _mod_matmul(cvec, w, b):
    m, k = cvec.shape
    layers, _, n = w.shape
    tn = 1024
    return pl.pallas_call(
        _mod_kernel,
        grid=(layers, n // tn),
        in_specs=[pl.BlockSpec((m, k), lambda l, j: (0, 0)),
                  pl.BlockSpec((1, k, tn), lambda l, j: (l, 0, j)),
                  pl.BlockSpec((1, 1, tn), lambda l, j: (l, 0, j))],
        out_specs=pl.BlockSpec((1, m, tn), lambda l, j: (l, 0, j)),
        out_shape=jax.ShapeDtypeStruct((layers, m, n), F32),
        compiler_params=_params(("parallel", "arbitrary")),
        name="mod_matmul",
    )(cvec, w, b.reshape(layers, 1, n))


def _modmm_kernel(x_ref, c_ref, sc_ref, sh_ref, w_ref, o_ref, a_scr, *, n_lat_tiles):
    @pl.when(pl.program_id(1) == 0)
    def _():
        h = jnp.where(pl.program_id(0) < n_lat_tiles, x_ref[...], c_ref[...])
        a_scr[...] = (h * (1.0 + sc_ref[0]) + sh_ref[0]).astype(BF16)

    o_ref[...] = jnp.dot(a_scr[...], w_ref[...], preferred_element_type=F32).astype(o_ref.dtype)


def _modglu_kernel(h_ref, sc_ref, sh_ref, wa_ref, wg_ref, ba_ref, bg_ref, o_ref, a_scr):
    @pl.when(pl.program_id(1) == 0)
    def _():
        a_scr[...] = (h_ref[...] * (1.0 + sc_ref[0]) + sh_ref[0]).astype(BF16)

    a = a_scr[...]
    va = jnp.dot(a, wa_ref[...], preferred_element_type=F32) + ba_ref[...]
    vg = jnp.dot(a, wg_ref[...], preferred_element_type=F32) + bg_ref[...]
    o_ref[...] = va * jax.nn.sigmoid(vg)


MOD_ROWS = 8
MOD_PIECES = 6
SH_A, SC_A, G_A, SH_F, SC_F, G_F = range(MOD_PIECES)


def _mod_spec(layer, piece, tile_rows, seq):
    def index(i, *_):
        return ((layer * MOD_ROWS + (i * tile_rows) // seq) * MOD_PIECES + piece, 0, 0)

    return pl.BlockSpec((1, 1, D_MODEL), index)


def _mod_matmul_qkv(x_lat, x_ctx, mods, layer, w, seq):
    n_lat, k = x_lat.shape
    n_rows = n_lat + x_ctx.shape[0]
    n = w.shape[1]
    tm, tn = MM_ROW_TILE, 2048
    n_lat_tiles = n_lat // tm
    return pl.pallas_call(
        functools.partial(_modmm_kernel, n_lat_tiles=n_lat_tiles),
        grid=(n_rows // tm, n // tn),
        in_specs=[pl.BlockSpec((tm, k), lambda i, j: (jnp.minimum(i, n_lat_tiles - 1), 0)),
                  pl.BlockSpec((tm, k), lambda i, j: (jnp.maximum(i - n_lat_tiles, 0), 0)),
                  _mod_spec(layer, SC_A, tm, seq), _mod_spec(layer, SH_A, tm, seq),
                  pl.BlockSpec((k, tn), lambda i, j: (0, j))],
        out_specs=pl.BlockSpec((tm, tn), lambda i, j: (i, j)),
        out_shape=jax.ShapeDtypeStruct((n_rows, n), BF16),
        scratch_shapes=[pltpu.VMEM((tm, k), BF16)],
        compiler_params=_params(("parallel", "arbitrary")),
        name="mod_qkv",
    )(x_lat, x_ctx, mods, mods, w)


def _mod_matmul_glu(h, mods, layer, w, b, n_rows, seq):
    k = h.shape[1]
    n = w.shape[1] // 2
    tm, tn = MM_ROW_TILE, 1024
    nj = n // tn
    b2 = b.reshape(1, 2 * n)
    return pl.pallas_call(
        _modglu_kernel,
        grid=(n_rows // tm, nj),
        in_specs=[pl.BlockSpec((tm, k), lambda i, j: (i, 0)),
                  _mod_spec(layer, SC_A, tm, seq), _mod_spec(layer, SH_A, tm, seq),
                  pl.BlockSpec((k, tn), lambda i, j: (0, j)),
                  pl.BlockSpec((k, tn), lambda i, j: (0, j + nj)),
                  pl.BlockSpec((1, tn), lambda i, j: (0, j)),
                  pl.BlockSpec((1, tn), lambda i, j: (0, j + nj))],
        out_specs=pl.BlockSpec((tm, tn), lambda i, j: (i, j)),
        out_shape=jax.ShapeDtypeStruct((n_rows, n), F32),
        scratch_shapes=[pltpu.VMEM((tm, k), BF16)],
        compiler_params=_params(("parallel", "arbitrary")),
        name="mod_pw1_glu",
    )(h, mods, mods, w, w, b2, b2)


def _na_block_window(r0, rows):
    return jnp.clip(r0 - WIN_ROWS // 2, 0, rows - NA_KEY_ROWS)


def _na_block_kinds(rows):
    def geometry(r0):
        k0 = min(max(r0 - WIN_ROWS // 2, 0), rows - NA_KEY_ROWS)
        out = []
        for r in range(r0, r0 + NA_BLOCK_ROWS):
            rs = min(max(r - WIN_ROWS // 2, 0), rows - WIN_ROWS)
            assert 0 <= rs - k0 <= NA_KEY_ROWS - WIN_ROWS
            out.append((rs - k0, r - rs))
        return tuple(out)

    kinds, kind_of = [], {}
    for r0 in range(0, rows, NA_BLOCK_ROWS):
        geo = geometry(r0)
        if geo not in kinds:
            kinds.append(geo)
        kind_of[r0] = kinds.index(geo)
    return kinds, kind_of


def _na_kernel(q_ref, k_ref, v_ref, kc_ref, vc_ref, bias_ref, o_ref, *, rows):
    scale = HEAD_DIM ** -0.5
    contract_last = (((1,), (1,)), ((), ()))
    row0 = pl.program_id(2) * ROWS_PER_STEP
    n_keys = NA_KEY_ROWS * GRID_W
    bq = NA_BLOCK_ROWS * GRID_W
    n_blk = ROWS_PER_STEP // NA_BLOCK_ROWS
    _, kind_of = _na_block_kinds(rows)
    kc = kc_ref[...]
    vc = vc_ref[...]

    def block_kind(r0):
        common = max(set(kind_of.values()), key=list(kind_of.values()).count)
        kind = jnp.int32(common)
        for r_static, kd in kind_of.items():
            if kd != common:
                kind = jnp.where(r0 == r_static, kd, kind)
        return kind

    def key_start(b):
        return pl.multiple_of(_na_block_window(row0 + b * NA_BLOCK_ROWS, rows) * GRID_W, GRID_W)

    def scores(b):
        q = q_ref[b * bq:(b + 1) * bq, :]
        kw = k_ref[pl.ds(key_start(b), n_keys), :]
        bias = bias_ref[0, block_kind(row0 + b * NA_BLOCK_ROWS)]
        s_loc = lax.dot_general(q, kw, contract_last, preferred_element_type=F32) * scale + bias
        s_ctx = lax.dot_general(q, kc, contract_last, preferred_element_type=F32) * scale
        return s_loc, s_ctx

    def finish(b, s_loc, s_ctx):
        vw = v_ref[pl.ds(key_start(b), n_keys), :]
        m = jnp.maximum(jnp.max(s_loc, axis=-1, keepdims=True), jnp.max(s_ctx, axis=-1, keepdims=True))
        p_loc = jnp.exp(s_loc - m)
        p_ctx = jnp.exp(s_ctx - m)
        denom = jnp.sum(p_loc, axis=-1, keepdims=True) + jnp.sum(p_ctx, axis=-1, keepdims=True)
        o = (jnp.dot(p_loc.astype(BF16), vw, preferred_element_type=F32)
             + jnp.dot(p_ctx.astype(BF16), vc, preferred_element_type=F32))
        o_ref[b * bq:(b + 1) * bq, :] = (o / denom).astype(o_ref.dtype)

    s_next = scores(0)
    for b in range(n_blk):
        s_cur = s_next
        if b + 1 < n_blk:
            s_next = scores(b + 1)
        finish(b, *s_cur)


def _na_block_bias(row_table, rows):
    n_heads = row_table.shape[0]
    kinds, _ = _na_block_kinds(rows)
    per_kind = []
    for geo in kinds:
        per_row = []
        for rel, off in geo:
            before = jnp.full((n_heads, GRID_W, rel * GRID_W), NEG_INF, F32)
            after = jnp.full((n_heads, GRID_W, (NA_KEY_ROWS - WIN_ROWS - rel) * GRID_W), NEG_INF, F32)
            per_row.append(jnp.concatenate([before, row_table[:, off], after], axis=-1))
        per_kind.append(jnp.concatenate(per_row, axis=1))
    return jnp.stack(per_kind, axis=1)


def _na_bias_table(rpb):
    col = jnp.arange(GRID_W)
    cstart = jnp.clip(col - WIN_COLS // 2, 0, GRID_W - WIN_COLS)
    col_ok = (col[None, :] >= cstart[:, None]) & (col[None, :] < cstart[:, None] + WIN_COLS)
    cidx = jnp.clip(col[None, :] - col[:, None] + WIN_COLS - 1, 0, 2 * WIN_COLS - 2)
    onehot = (cidx[:, :, None] == jnp.arange(2 * WIN_COLS - 1)).astype(F32)
    t = jnp.einsum('hrc,qkc->hrqk', rpb.astype(F32), onehot, precision=lax.Precision.HIGHEST)
    t = jnp.where(col_ok[None, None], t, NEG_INF)
    t = jnp.stack([t[:, WIN_ROWS - 1 - off:2 * WIN_ROWS - 1 - off] for off in range(WIN_ROWS)], axis=1)
    t = t.transpose(0, 1, 3, 2, 4)
    return t.reshape(rpb.shape[0], WIN_ROWS, GRID_W, WIN_ROWS * GRID_W)


def _na_attention(qkv, bias_table, n_batch, seq, ctx_len, n_rows_out):
    rows = seq // GRID_W
    groups = rows // ROWS_PER_STEP
    tq = ROWS_PER_STEP * GRID_W
    ctx_blk0 = (n_batch * seq) // ctx_len
    h_ = N_HEADS
    block_bias = _na_block_bias(bias_table, rows)
    n_kinds = block_bias.shape[1]
    return pl.pallas_call(
        functools.partial(_na_kernel, rows=rows),
        grid=(n_batch, h_, groups),
        in_specs=[pl.BlockSpec((tq, HEAD_DIM), lambda b, h, g: (b * groups + g, h)),
                  pl.BlockSpec((seq, HEAD_DIM), lambda b, h, g: (b, h_ + h)),
                  pl.BlockSpec((seq, HEAD_DIM), lambda b, h, g: (b, 2 * h_ + h)),
                  pl.BlockSpec((ctx_len, HEAD_DIM), lambda b, h, g: (ctx_blk0 + b, h_ + h)),
                  pl.BlockSpec((ctx_len, HEAD_DIM), lambda b, h, g: (ctx_blk0 + b, 2 * h_ + h)),
                  pl.BlockSpec((1, n_kinds, NA_BLOCK_ROWS * GRID_W, NA_KEY_ROWS * GRID_W),
                               lambda b, h, g: (h, 0, 0, 0))],
        out_specs=pl.BlockSpec((tq, HEAD_DIM), lambda b, h, g: (b * groups + g, h)),
        out_shape=jax.ShapeDtypeStruct((n_rows_out, D_MODEL), BF16),
        compiler_params=_params(("parallel", "parallel", "arbitrary")),
        name="na_attention",
    )(qkv, qkv, qkv, qkv, qkv, block_bias)


def _ctx_attn_kernel(q_ref, k_ref, v_ref, prev_ref, o_ref):
    del prev_ref
    scale = HEAD_DIM ** -0.5
    s = lax.dot_general(q_ref[...], k_ref[...], (((1,), (1,)), ((), ())), preferred_element_type=F32) * scale
    m = jnp.max(s, axis=-1, keepdims=True)
    p = jnp.exp(s - m)
    denom = jnp.sum(p, axis=-1, keepdims=True)
    o = jnp.dot(p.astype(BF16), v_ref[...], preferred_element_type=F32)
    o_ref[...] = (o / denom).astype(o_ref.dtype)


def _ctx_attention(qkv, attn_out, n_batch, seq, ctx_len):
    ctx_blk0 = (n_batch * seq) // ctx_len
    h_ = N_HEADS
    return pl.pallas_call(
        _ctx_attn_kernel,
        grid=(n_batch, h_),
        in_specs=[pl.BlockSpec((ctx_len, HEAD_DIM), lambda b, h: (ctx_blk0 + b, h)),
                  pl.BlockSpec((ctx_len, HEAD_DIM), lambda b, h: (ctx_blk0 + b, h_ + h)),
                  pl.BlockSpec((ctx_len, HEAD_DIM), lambda b, h: (ctx_blk0 + b, 2 * h_ + h)),
                  pl.BlockSpec(memory_space=pl.ANY)],
        out_specs=pl.BlockSpec((ctx_len, HEAD_DIM), lambda b, h: (ctx_blk0 + b, h)),
        out_shape=jax.ShapeDtypeStruct(attn_out.shape, attn_out.dtype),
        input_output_aliases={3: 0},
        compiler_params=_params(("parallel", "parallel")),
        name="ctx_attention",
    )(qkv, qkv, qkv, attn_out)


def _dwconv_kernel(x_ref, w_ref, b_ref, o_ref, xpad, *, seq):
    half = CONV_WIDTH // 2
    lead = 2 * SUBLANES
    ch = x_ref.shape[1]
    xpad[pl.ds(0, lead), :] = jnp.zeros((lead, ch), F32)
    xpad[pl.ds(lead + seq, lead), :] = jnp.zeros((lead, ch), F32)
    xpad[pl.ds(lead, seq), :] = x_ref[...]
    w = w_ref[...]
    bias = b_ref[...]
    win_rows = CONV_TT + 2 * lead

    def body(c, carry):
        t0 = pl.multiple_of(c * CONV_TT, CONV_TT)
        win = xpad[pl.ds(t0, win_rows), :]
        acc = jnp.broadcast_to(bias, (CONV_TT, ch))
        for p in range(SUBLANES):
            shifted = win if p == 0 else pltpu.roll(win, win_rows - p, axis=0)
            for k in range(CONV_WIDTH):
                o = lead - half + k
                if o % SUBLANES == p:
                    acc = acc + shifted[o - p:o - p + CONV_TT, :] * w[k:k + 1, :]
        o_ref[pl.ds(t0, CONV_TT), :] = acc
        return carry

    lax.fori_loop(0, seq // CONV_TT, body, 0)


def _dwconv(x, w_dw, b_dw, n_batch, seq):
    d = x.shape[1]
    nc = d // CONV_CH
    return pl.pallas_call(
        functools.partial(_dwconv_kernel, seq=seq),
        grid=(n_batch, nc),
        in_specs=[pl.BlockSpec((seq, CONV_CH), lambda b, c: (b, c)),
                  pl.BlockSpec((CONV_WIDTH, CONV_CH), lambda b, c: (0, c)),
                  pl.BlockSpec((1, CONV_CH), lambda b, c: (0, c))],
        out_specs=pl.BlockSpec((seq, CONV_CH), lambda b, c: (b, c)),
        out_shape=jax.ShapeDtypeStruct((n_batch * seq, d), F32),
        scratch_shapes=[pltpu.VMEM((seq + 4 * SUBLANES, CONV_CH), F32)],
        compiler_params=_params(("parallel", "parallel")),
        name="dwconv",
    )(x, w_dw, b_dw.reshape(1, d))


def _proj_ln_kernel(*refs, pre_ln, has_bias, n_lat_tiles):
    it = iter(refs)
    a_ref = next(it)
    pre_g = next(it) if pre_ln else None
    pre_b = next(it) if pre_ln else None
    w_ref = next(it)
    bias_ref = next(it) if has_bias else None
    h_ref = next(it)
    hc_ref = next(it) if n_lat_tiles is not None else None
    gate_ref, lng_ref, lnb_ref, scf_ref, shf_ref, wr_ref = (next(it) for _ in range(6))
    hnew_ref, fin_ref, score_ref = (next(it) for _ in range(3))

    a = a_ref[...]
    if pre_ln:
        a = _silu(_layer_norm(a, pre_g[...], pre_b[...]))
    m = jnp.dot(a.astype(BF16), w_ref[...], preferred_element_type=F32)
    if has_bias:
        m = m + bias_ref[...]
    h_old = h_ref[...]
    if n_lat_tiles is not None:
        h_old = jnp.where(pl.program_id(0) < n_lat_tiles, h_old, hc_ref[...])
    hn = _layer_norm(DEEPNORM_ALPHA * h_old + gate_ref[0] * m, lng_ref[...], lnb_ref[...])
    hnew_ref[...] = hn
    f = hn * (1.0 + scf_ref[0]) + shf_ref[0]
    _store_packed_rows(fin_ref, f)
    score_ref[...] = jax.nn.sigmoid(jnp.dot(f.astype(BF16), wr_ref[...], preferred_element_type=F32))


def _proj_ln(a, w, bias, h, h_ctx, mods, layer, ln_g, ln_b, w_router, n_rows, seq, pre_ln=None):
    d = D_MODEL
    tm = ROW_TILE
    row = lambda i: (i, 0)
    const = lambda i: (0, 0)
    n_lat_tiles = None if h_ctx is None else h.shape[0] // tm
    args, specs = [a], [pl.BlockSpec((tm, d), row)]
    if pre_ln is not None:
        args += [pre_ln[0].reshape(1, d), pre_ln[1].reshape(1, d)]
        specs += [pl.BlockSpec((1, d), const)] * 2
    args.append(w)
    specs.append(pl.BlockSpec((d, d), const))
    if bias is not None:
        args.append(bias.reshape(1, d))
        specs.append(pl.BlockSpec((1, d), const))
    if h_ctx is None:
        args.append(h)
        specs.append(pl.BlockSpec((tm, d), row))
    else:
        args += [h, h_ctx]
        specs += [pl.BlockSpec((tm, d), lambda i: (jnp.minimum(i, n_lat_tiles - 1), 0)),
                  pl.BlockSpec((tm, d), lambda i: (jnp.maximum(i - n_lat_tiles, 0), 0))]
    args += [mods, ln_g.reshape(1, d), ln_b.reshape(1, d), mods, mods, w_router]
    specs += [_mod_spec(layer, G_A, tm, seq), pl.BlockSpec((1, d), const), pl.BlockSpec((1, d), const),
              _mod_spec(layer, SC_F, tm, seq), _mod_spec(layer, SH_F, tm, seq),
              pl.BlockSpec((d, N_EXPERTS), const)]
    return pl.pallas_call(
        functools.partial(_proj_ln_kernel, pre_ln=pre_ln is not None, has_bias=bias is not None,
                          n_lat_tiles=n_lat_tiles),
        grid=(n_rows // tm,),
        in_specs=specs,
        out_specs=[pl.BlockSpec((tm, d), row), pl.BlockSpec((tm * SUBLANES, LANES), row),
                   pl.BlockSpec((tm, N_EXPERTS), row)],
        out_shape=[jax.ShapeDtypeStruct((n_rows, d), F32),
                   jax.ShapeDtypeStruct((n_rows * SUBLANES, LANES), jnp.uint32),
                   jax.ShapeDtypeStruct((n_rows, N_EXPERTS), F32)],
        compiler_params=_params(("parallel",)),
        name="proj_ln",
    )(*args)


def _router_kernel(s_ref, b_ref, ek_ref, rk_ref, gk_ref, cnt_ref, carry):
    tm, e = s_ref.shape

    @pl.when(pl.program_id(0) == 0)
    def _():
        carry[...] = jnp.zeros_like(carry)

    s = s_ref[...]
    lane = lax.broadcasted_iota(jnp.int32, (tm, e), 1).astype(F32)
    slot = lax.broadcasted_iota(jnp.int32, (tm, 128), 1)
    work = s + b_ref[...]
    sel = jnp.zeros((tm, e), F32)
    e_acc = jnp.zeros((tm, 128), F32)
    g_acc = jnp.zeros((tm, 128), F32)
    g_sum = jnp.zeros((tm, 1), F32)
    picks = []
    for k in range(TOP_K):
        m = jnp.max(work, axis=-1, keepdims=True)
        idx = jnp.min(jnp.where(work == m, lane, float(e)), axis=-1, keepdims=True)
        onehot = lane == idx
        gk = jnp.sum(jnp.where(onehot, s, 0.0), axis=-1, keepdims=True)
        sel = jnp.where(onehot, 1.0, sel)
        work = jnp.where(onehot, -jnp.inf, work)
        e_acc = jnp.where(slot == k, idx, e_acc)
        g_acc = jnp.where(slot == k, gk, g_acc)
        g_sum = g_sum + gk
        picks.append(idx)

    r_i = lax.broadcasted_iota(jnp.int32, (tm, tm), 0)
    c_i = lax.broadcasted_iota(jnp.int32, (tm, tm), 1)
    tri = jnp.where(c_i < r_i, 1.0, 0.0).astype(BF16)
    rank = jnp.dot(tri, sel.astype(BF16), preferred_element_type=F32) + carry[...]
    r_acc = jnp.zeros((tm, 128), F32)
    for k in range(TOP_K):
        rk = jnp.sum(jnp.where(lane == picks[k], rank, 0.0), axis=-1, keepdims=True)
        r_acc = jnp.where(slot == k, rk, r_acc)

    carry[...] = carry[...] + jnp.sum(sel, axis=0, keepdims=True)
    ek_ref[...] = e_acc[:, :TOP_K].astype(jnp.int32)
    rk_ref[...] = r_acc[:, :TOP_K].astype(jnp.int32)
    gk_ref[...] = (g_acc / g_sum * ROUTED_SCALE)[:, :TOP_K]
    cnt_ref[...] = carry[...].astype(jnp.int32)


def _router(scores, router_bias, n_rows):
    e = N_EXPERTS
    tm = ROW_TILE
    row = lambda i: (i, 0)
    return pl.pallas_call(
        _router_kernel,
        grid=(n_rows // tm,),
        in_specs=[pl.BlockSpec((tm, e), row), pl.BlockSpec((1, e), lambda i: (0, 0))],
        out_specs=[pl.BlockSpec((tm, TOP_K), row), pl.BlockSpec((tm, TOP_K), row), pl.BlockSpec((tm, TOP_K), row),
                   pl.BlockSpec((1, e), lambda i: (0, 0))],
        out_shape=[jax.ShapeDtypeStruct((n_rows, TOP_K), jnp.int32), jax.ShapeDtypeStruct((n_rows, TOP_K), jnp.int32),
                   jax.ShapeDtypeStruct((n_rows, TOP_K), F32), jax.ShapeDtypeStruct((1, e), jnp.int32)],
        scratch_shapes=[pltpu.VMEM((1, e), F32)],
        compiler_params=_params(("arbitrary",)),
        name="router",
    )(scores, router_bias.reshape(1, e).astype(F32))


def _layout(e_k, r_k, counts, tm_e):
    e = N_EXPERTS
    n_assign = e_k.size
    counts = counts.reshape(e)
    padded = (counts + tm_e - 1) // tm_e * tm_e
    pends = jnp.cumsum(padded)
    pstarts = pends - padded
    experts = jnp.arange(e, dtype=jnp.int32)
    start_of = jnp.sum(jnp.where(e_k[..., None] == experts, pstarts, 0), axis=-1)
    dest = (start_of + r_k).reshape(n_assign).astype(jnp.int32)
    n_blocks = (n_assign + e * (tm_e - 1) + tm_e - 1) // tm_e
    blk_row0 = jnp.arange(n_blocks, dtype=jnp.int32) * tm_e
    blk_e = jnp.minimum(jnp.sum(pends[None, :] <= blk_row0[:, None], axis=-1), e - 1).astype(jnp.int32)
    n_valid = (pends[-1] // tm_e).astype(jnp.int32).reshape(1)
    pad_info = jnp.concatenate([pstarts + counts, padded - counts]).astype(jnp.int32)
    later = (experts[None, :] > experts[:, None]) & (counts[None, :] > 0)
    next_e = jnp.min(jnp.where(later, experts[None, :], e), axis=1)
    next_e = jnp.where(next_e == e, -1, next_e)
    order_e = jnp.sum((experts[None, :] < experts[:, None]) & (counts[None, :] > 0), axis=1)
    chain = jnp.concatenate([next_e, order_e]).astype(jnp.int32)
    return dest, blk_e, n_valid, pad_info, chain, n_blocks


def _dispatch_kernel(pad_ref, dest_ref, f_ref, xs_ref, zbuf, sem, zsem):
    tm = f_ref.shape[0] // SUBLANES
    pad_bits = EXPERT_TILE.bit_length() - 1

    def zero_copies(e_idx):
        start = pad_ref[e_idx]
        pad = pad_ref[N_EXPERTS + e_idx]
        return [(((pad >> b) & 1) == 1, pltpu.make_async_copy(
            _row_tile(zbuf, 0, 1 << b), _row_tile(xs_ref, start + (pad & ((1 << b) - 1)), 1 << b), zsem))
            for b in range(pad_bits)]

    @pl.when(pl.program_id(0) == 0)
    def _():
        zbuf[...] = jnp.zeros_like(zbuf)

        def start_e(e_idx, carry):
            for cond, cp in zero_copies(e_idx):
                pl.when(cond)(cp.start)
            return carry

        def wait_e(e_idx, carry):
            for cond, cp in zero_copies(e_idx):
                pl.when(cond)(cp.wait)
            return carry

        lax.fori_loop(0, N_EXPERTS, start_e, 0)
        lax.fori_loop(0, N_EXPERTS, wait_e, 0)

    def tok(t, carry):
        for k in range(TOP_K):
            d = dest_ref[t * TOP_K + k]
            pltpu.make_async_copy(_row_tile(f_ref, t), _row_tile(xs_ref, d), sem).start(priority=k % 2)
        return carry

    lax.fori_loop(0, tm, tok, 0)
    for k in range(TOP_K):
        pltpu.make_async_copy(f_ref, _row_tile(xs_ref, 0, tm), sem).wait()


def _dispatch(f_in, dest, pad_info, n_rows, n_blocks):
    tm = ROW_TILE
    grid_spec = pltpu.PrefetchScalarGridSpec(
        num_scalar_prefetch=1,
        grid=(n_rows // tm,),
        in_specs=[pl.BlockSpec((tm * TOP_K,), lambda i, p: (i,), memory_space=pltpu.SMEM),
                  pl.BlockSpec((tm * SUBLANES, LANES), lambda i, p: (i, 0))],
        out_specs=pl.BlockSpec(memory_space=pl.ANY),
        scratch_shapes=[pltpu.VMEM((EXPERT_TILE // 2 * SUBLANES, LANES), f_in.dtype), pltpu.SemaphoreType.DMA,
                        pltpu.SemaphoreType.DMA],
    )
    return pl.pallas_call(
        _dispatch_kernel,
        grid_spec=grid_spec,
        out_shape=jax.ShapeDtypeStruct((n_blocks * EXPERT_TILE * SUBLANES, LANES), f_in.dtype),
        compiler_params=_params(("arbitrary",)),
        name="dispatch",
    )(pad_info, dest, f_in)


def _swiglu_hidden(x, wg_ref, wu_ref):
    gv = jnp.dot(x, wg_ref[...], preferred_element_type=F32)
    uv = jnp.dot(x, wu_ref[...], preferred_element_type=F32)
    return (_silu(gv) * uv).astype(BF16)


def _expert_kernel(be_ref, nv_ref, chain_ref, x_ref, wg_hbm, wu_hbm, wd_hbm, y_ref,
                   wg32, wu32, wd32, wgb, wub, wdb, sems, *, layer):
    i = pl.program_id(0)
    valid = i < nv_ref[0]
    e_cur = be_ref[i]
    first = jnp.logical_or(i == 0, e_cur != be_ref[jnp.maximum(i - 1, 0)])
    d, de = wgb.shape
    n_chunks = 8

    def weight_copies(e_idx, slot):
        return [pltpu.make_async_copy(w_hbm.at[layer, e_idx], buf.at[slot], sems.at[slot, j])
                for j, (w_hbm, buf) in enumerate(((wg_hbm, wg32), (wu_hbm, wu32), (wd_hbm, wd32)))]

    @pl.when(jnp.logical_and(valid, first))
    def _():
        slot = chain_ref[N_EXPERTS + e_cur] & 1
        e_next = chain_ref[e_cur]

        @pl.when(i == 0)
        def _():
            for cp in weight_copies(e_cur, slot):
                cp.start()

        for cp in weight_copies(e_cur, slot):
            cp.wait()

        @pl.when(e_next >= 0)
        def _():
            for cp in weight_copies(e_next, 1 - slot):
                cp.start()

        def cast_chunk(c, carry):
            r0 = pl.multiple_of(c * (d // n_chunks), d // n_chunks)
            wgb[pl.ds(r0, d // n_chunks), :] = wg32[slot, pl.ds(r0, d // n_chunks), :].astype(BF16)
            wub[pl.ds(r0, d // n_chunks), :] = wu32[slot, pl.ds(r0, d // n_chunks), :].astype(BF16)
            r1 = pl.multiple_of(c * (de // n_chunks), de // n_chunks)
            wdb[pl.ds(r1, de // n_chunks), :] = wd32[slot, pl.ds(r1, de // n_chunks), :].astype(BF16)
            return carry

        lax.fori_loop(0, n_chunks, cast_chunk, 0)

    @pl.when(valid)
    def _():
        act = _swiglu_hidden(_load_packed_rows_bf16(x_ref), wgb, wub)
        _store_packed_rows(y_ref, jnp.dot(act, wdb[...], preferred_element_type=F32))


def _expert_ffn(xs, blk_e, n_valid, chain, w_gate, w_up, w_down, layer):
    n_rows = xs.shape[0] // SUBLANES
    d, de = w_gate.shape[2], w_gate.shape[3]
    tm = EXPERT_TILE
    n_blocks = n_rows // tm

    def xmap(i, be, nv, ch):
        return (jnp.minimum(i, nv[0] - 1), 0)

    grid_spec = pltpu.PrefetchScalarGridSpec(
        num_scalar_prefetch=3,
        grid=(n_blocks,),
        in_specs=[pl.BlockSpec((tm * SUBLANES, LANES), xmap),
                  pl.BlockSpec(memory_space=pl.ANY),
                  pl.BlockSpec(memory_space=pl.ANY),
                  pl.BlockSpec(memory_space=pl.ANY)],
        out_specs=pl.BlockSpec((tm * SUBLANES, LANES), xmap),
        scratch_shapes=[pltpu.VMEM((2, d, de), F32), pltpu.VMEM((2, d, de), F32), pltpu.VMEM((2, de, d), F32),
                        pltpu.VMEM((d, de), BF16), pltpu.VMEM((d, de), BF16), pltpu.VMEM((de, d), BF16),
                        pltpu.SemaphoreType.DMA((2, 3))],
    )
    return pl.pallas_call(
        functools.partial(_expert_kernel, layer=layer),
        grid_spec=grid_spec,
        out_shape=jax.ShapeDtypeStruct((n_rows * SUBLANES, LANES), jnp.uint32),
        compiler_params=_params(("arbitrary",)),
        name="expert_ffn",
    )(blk_e, n_valid, chain, xs, w_gate, w_up, w_down)


def _ffn_out_kernel(dcur_ref, dnext_ref, f_ref, gk_ref, wg_ref, wu_ref, wd_ref, h_ref, gate_ref, lng_ref, lnb_ref,
                    y_ref, o_ref, ybuf, sems):
    i = pl.program_id(0)
    n_steps = pl.num_programs(0)
    tm = f_ref.shape[0] // SUBLANES
    slot = lax.rem(i, 2)

    def gather_token(dref, s, t):
        for k in range(TOP_K):
            row = dref[t * TOP_K + k]
            pltpu.make_async_copy(_row_tile(y_ref, row), _row_tile(ybuf.at[s, k], t),
                                  sems.at[s]).start(priority=k % 2)

    def wait_gathers(s):
        for k in range(TOP_K):
            pltpu.make_async_copy(_row_tile(y_ref, 0, tm), ybuf.at[s, k], sems.at[s]).wait()

    @pl.when(i == 0)
    def _():
        def tok(t, carry):
            gather_token(dcur_ref, slot, t)
            return carry

        lax.fori_loop(0, tm, tok, 0)

    n_pieces = 2 + SUBLANES
    piece = [(p * tm) // n_pieces for p in range(n_pieces + 1)]

    def issue_piece(p):
        for t in range(piece[p], piece[p + 1]):
            gather_token(dnext_ref, 1 - slot, t)

    issue_piece(0)
    act = _swiglu_hidden(_load_packed_rows_bf16(f_ref), wg_ref, wu_ref)
    issue_piece(1)
    f = jnp.dot(act, wd_ref[...], preferred_element_type=F32)

    gk = gk_ref[...]
    wait_gathers(slot)
    gcols = [jnp.broadcast_to(gk[:, k:k + 1], (tm, LANES)) for k in range(TOP_K)]
    parts = []
    for s_ in range(SUBLANES):
        issue_piece(2 + s_)
        r_lo = jnp.zeros((tm, LANES), F32)
        r_hi = jnp.zeros((tm, LANES), F32)
        for k in range(TOP_K):
            y_lo, y_hi = _load_packed_group(ybuf.at[slot, k], s_)
            r_lo = r_lo + gcols[k] * y_lo
            r_hi = r_hi + gcols[k] * y_hi
        parts += [r_lo, r_hi]
    f = f + jnp.concatenate(parts, axis=1)
    o_ref[...] = _layer_norm(DEEPNORM_ALPHA * h_ref[...] + gate_ref[0] * f, lng_ref[...], lnb_ref[...])

    @pl.when(i == n_steps - 1)
    def _():
        wait_gathers(1 - slot)


def _ffn_out(f_in, y, dest, gates, ws_gate, ws_up, ws_down, h, mods, layer, ln_g, ln_b, n_rows, seq):
    d = D_MODEL
    ds_ = ws_gate.shape[1]
    tm = COMBINE_TILE
    n_steps = n_rows // tm
    row = lambda i: (i, 0)
    const = lambda i: (0, 0)
    return pl.pallas_call(
        _ffn_out_kernel,
        grid=(n_steps,),
        in_specs=[pl.BlockSpec((tm * TOP_K,), lambda i: (i,), memory_space=pltpu.SMEM),
                  pl.BlockSpec((tm * TOP_K,), lambda i: (jnp.minimum(i + 1, n_steps - 1),), memory_space=pltpu.SMEM),
                  pl.BlockSpec((tm * SUBLANES, LANES), row), pl.BlockSpec((tm, TOP_K), row),
                  pl.BlockSpec((d, ds_), const), pl.BlockSpec((d, ds_), const), pl.BlockSpec((ds_, d), const),
                  pl.BlockSpec((tm, d), row), _mod_spec(layer, G_F, tm, seq),
                  pl.BlockSpec((1, d), const), pl.BlockSpec((1, d), const),
                  pl.BlockSpec(memory_space=pl.ANY)],
        out_specs=pl.BlockSpec((tm, d), row),
        out_shape=jax.ShapeDtypeStruct((n_rows, d), F32),
        scratch_shapes=[pltpu.VMEM((2, TOP_K, tm * SUBLANES, LANES), jnp.uint32), pltpu.SemaphoreType.DMA((2,))],
        compiler_params=_params(("arbitrary",)),
        name="ffn_out",
    )(dest, dest, f_in, gates, ws_gate, ws_up, ws_down, h, mods, ln_g.reshape(1, d), ln_b.reshape(1, d), y)


def _moe_and_norm(h, f_in, scores, mods, router_bias, we_gate, we_up, we_down, ws_gate, ws_up, ws_down,
                  ln_g, ln_b, n_rows, seq, layer):
    e_k, r_k, gates, counts = _router(scores, router_bias, n_rows)
    dest, blk_e, n_valid, pad_info, chain, n_blocks = _layout(e_k, r_k, counts, EXPERT_TILE)
    xs = _dispatch(f_in, dest, pad_info, n_rows, n_blocks)
    y = _expert_ffn(xs, blk_e, n_valid, chain, we_gate, we_up, we_down, layer)
    return _ffn_out(f_in, y, dest, gates, ws_gate.astype(BF16), ws_up.astype(BF16), ws_down.astype(BF16),
                    h, mods, layer, ln_g, ln_b, n_rows, seq)


def kernel(x, c, ctx, c_ctx, mod_w, mod_b, ln_mix_g, ln_mix_b, ln_ffn_g, ln_ffn_b, na_w_qkv, na_rpb, na_w_o,
           cv_w_pw1, cv_b_pw1, cv_w_dw, cv_b_dw, cv_ln_g, cv_ln_b, cv_w_pw2, cv_b_pw2, moe_w_router,
           moe_router_bias, moe_w_gate, moe_w_up, moe_w_down, sh_w_gate, sh_w_up, sh_w_down):
    n_batch, seq, d = x.shape
    ctx_len = ctx.shape[1]
    n_lat = n_batch * seq
    n_tok = n_lat + n_batch * ctx_len

    cvec = jnp.zeros((MOD_ROWS, d), F32).at[:n_batch].set(c).at[n_batch].set(c_ctx)
    mods = _mod_matmul(cvec, mod_w, mod_b).reshape(DEPTH * MOD_ROWS * MOD_PIECES, 1, d)
    x_lat = x.reshape(n_lat, d)
    x_ctx = ctx.reshape(n_batch * ctx_len, d)

    qkv = _mod_matmul_qkv(x_lat, x_ctx, mods, 0, na_w_qkv[0].astype(BF16), seq)
    attn = _na_attention(qkv, _na_bias_table(na_rpb[0]), n_batch, seq, ctx_len, n_tok)
    attn = _ctx_attention(qkv, attn, n_batch, seq, ctx_len)
    h, f_in, scores = _proj_ln(attn, na_w_o[0].astype(BF16), None, x_lat, x_ctx, mods, 0, ln_mix_g[0], ln_mix_b[0],
                               moe_w_router[0].astype(BF16), n_tok, seq)
    h = _moe_and_norm(h, f_in, scores, mods, moe_router_bias[0], moe_w_gate, moe_w_up, moe_w_down,
                      sh_w_gate[0], sh_w_up[0], sh_w_down[0], ln_ffn_g[0], ln_ffn_b[0], n_tok, seq, 0)

    glu = _mod_matmul_glu(h, mods, 1, cv_w_pw1[0].astype(BF16), cv_b_pw1[0], n_lat, seq)
    conv = _dwconv(glu, cv_w_dw[0], cv_b_dw[0], n_batch, seq)
    h, f_in, scores = _proj_ln(conv, cv_w_pw2[0].astype(BF16), cv_b_pw2[0], h, None, mods, 1, ln_mix_g[1],
                               ln_mix_b[1], moe_w_router[1].astype(BF16), n_lat, seq,
                               pre_ln=(cv_ln_g[0], cv_ln_b[0]))
    h = _moe_and_norm(h, f_in, scores, mods, moe_router_bias[1], moe_w_gate, moe_w_up, moe_w_down,
                      sh_w_gate[1], sh_w_up[1], sh_w_down[1], ln_ffn_g[1], ln_ffn_b[1], n_lat, seq, 1)
    return h.reshape(n_batch, seq, d)
```

```python
import functools

import jax
import jax.numpy as jnp
from jax import lax
from jax.experimental import pallas as pl
from jax.experimental.pallas import tpu as pltpu

D_MODEL = 2048
GRID_W = 64
N_HEADS = 16
HEAD_DIM = D_MODEL // N_HEADS
WIN_ROWS = 8
WIN_COLS = 16
CONV_WIDTH = 31
N_EXPERTS = 64
TOP_K = 8
ROUTED_SCALE = 2.5
LN_EPS = 1e-6
SUBLANES = 8
NEG_INF = -1e30
DEPTH = 2
DEEPNORM_ALPHA = (2 * DEPTH) ** 0.25

F32 = jnp.float32
BF16 = jnp.bfloat16

VMEM_LIMIT_BYTES = 56 * 1024 * 1024
ROW_TILE = 256
PROJ_TILE = 512
PROJ_SPLIT = 2
ROUTER_TILE = 512
MM_ROW_TILE = 512
EXPERT_TILE = 512
COMBINE_TILE = 128
ROWS_PER_STEP = 16
NA_BLOCK_ROWS = 2
NA_KEY_ROWS = WIN_ROWS + NA_BLOCK_ROWS
CONV_CH = 256
CONV_TT = 128


def _params(sem):
    return pltpu.CompilerParams(dimension_semantics=sem, vmem_limit_bytes=VMEM_LIMIT_BYTES)


def _layer_norm(z, g, b):
    mu = jnp.mean(z, axis=-1, keepdims=True)
    zc = z - mu
    var = jnp.mean(zc * zc, axis=-1, keepdims=True)
    return zc * lax.rsqrt(var + LN_EPS) * g + b


def _silu(x):
    return x * jax.nn.sigmoid(x)


LANES = 128
ROW_WORDS = D_MODEL // 2
assert ROW_WORDS == SUBLANES * LANES


def _store_packed_rows(ref, x):
    rows = x.shape[0]
    for s_ in range(SUBLANES):
        base = 2 * LANES * s_
        lo = pltpu.bitcast(x[:, base:base + LANES].astype(BF16).astype(F32), jnp.uint32)
        hi = pltpu.bitcast(x[:, base + LANES:base + 2 * LANES].astype(BF16).astype(F32), jnp.uint32)
        ref[pl.ds(s_, rows, stride=SUBLANES), :] = (lo >> 16) | (hi & jnp.uint32(0xFFFF0000))


def _load_packed_group(ref, s_):
    w = ref[pl.ds(s_, ref.shape[0] // SUBLANES, stride=SUBLANES), :]
    return pltpu.bitcast(w << 16, F32), pltpu.bitcast(w & jnp.uint32(0xFFFF0000), F32)


def _row_tile(ref, t, n=1):
    return ref.at[pl.ds(pl.multiple_of(t * SUBLANES, SUBLANES), n * SUBLANES)]


def _load_packed_rows_bf16(ref):
    parts = []
    for s_ in range(SUBLANES):
        lo, hi = _load_packed_group(ref, s_)
        parts += [lo.astype(BF16), hi.astype(BF16)]
    return jnp.concatenate(parts, axis=1)


def _mod_kernel(c_ref, w_ref, b_ref, o_ref):
    a = _silu(c_ref[...]).astype(BF16)
    o_ref[0] = jnp.dot(a, w_ref[0].astype(BF16), preferred_element_type=F32) + b_ref[0]


def _mod_matmul(cvec, w, b):
    m, k = cvec.shape
    layers, _, n = w.shape
    tn = 1024
    return pl.pallas_call(
        _mod_kernel,
        grid=(layers, n // tn),
        in_specs=[pl.BlockSpec((m, k), lambda l, j: (0, 0)),
                  pl.BlockSpec((1, k, tn), lambda l, j: (l, 0, j)),
                  pl.BlockSpec((1, 1, tn), lambda l, j: (l, 0, j))],
        out_specs=pl.BlockSpec((1, m, tn), lambda l, j: (l, 0, j)),
        out_shape=jax.ShapeDtypeStruct((layers, m, n), F32),
        compiler_params=_params(("parallel", "arbitrary")),
        name="mod_matmul",
    )(cvec, w, b.reshape(layers, 1, n))


def _modmm_kernel(x_ref, c_ref, sc_ref, sh_ref, w_ref, o_ref, a_scr, *, n_lat_tiles):
    @pl.when(pl.program_id(1) == 0)
    def _():
        h = jnp.where(pl.program_id(0) < n_lat_tiles, x_ref[...], c_ref[...])
        a_scr[...] = (h * (1.0 + sc_ref[0]) + sh_ref[0]).astype(BF16)

    o_ref[...] = jnp.dot(a_scr[...], w_ref[...], preferred_element_type=F32).astype(o_ref.dtype)


def _modglu_kernel(h_ref, sc_ref, sh_ref, wa_ref, wg_ref, ba_ref, bg_ref, o_ref, a_scr):
    @pl.when(pl.program_id(1) == 0)
    def _():
        a_scr[...] = (h_ref[...] * (1.0 + sc_ref[0]) + sh_ref[0]).astype(BF16)

    a = a_scr[...]
    va = jnp.dot(a, wa_ref[...], preferred_element_type=F32) + ba_ref[...]
    vg = jnp.dot(a, wg_ref[...], preferred_element_type=F32) + bg_ref[...]
    o_ref[...] = va * jax.nn.sigmoid(vg)


MOD_ROWS = 8
MOD_PIECES = 6
SH_A, SC_A, G_A, SH_F, SC_F, G_F = range(MOD_PIECES)


def _mod_spec(layer, piece, tile_rows, seq):
    def index(i, *_):
        return ((layer * MOD_ROWS + (i * tile_rows) // seq) * MOD_PIECES + piece, 0, 0)

    return pl.BlockSpec((1, 1, D_MODEL), index)


def _mod_matmul_qkv(x_lat, x_ctx, mods, layer, w, seq):
    n_lat, k = x_lat.shape
    n_rows = n_lat + x_ctx.shape[0]
    n = w.shape[1]
    tm, tn = MM_ROW_TILE, 2048
    n_lat_tiles = n_lat // tm
    return pl.pallas_call(
        functools.partial(_modmm_kernel, n_lat_tiles=n_lat_tiles),
        grid=(n_rows // tm, n // tn),
        in_specs=[pl.BlockSpec((tm, k), lambda i, j: (jnp.minimum(i, n_lat_tiles - 1), 0)),
                  pl.BlockSpec((tm, k), lambda i, j: (jnp.maximum(i - n_lat_tiles, 0), 0)),
                  _mod_spec(layer, SC_A, tm, seq), _mod_spec(layer, SH_A, tm, seq),
                  pl.BlockSpec((k, tn), lambda i, j: (0, j))],
        out_specs=pl.BlockSpec((tm, tn), lambda i, j: (i, j)),
        out_shape=jax.ShapeDtypeStruct((n_rows, n), BF16),
        scratch_shapes=[pltpu.VMEM((tm, k), BF16)],
        compiler_params=_params(("parallel", "arbitrary")),
        name="mod_qkv",
    )(x_lat, x_ctx, mods, mods, w)


def _mod_matmul_glu(h, mods, layer, w, b, n_rows, seq):
    k = h.shape[1]
    n = w.shape[1] // 2
    tm, tn = MM_ROW_TILE, 1024
    nj = n // tn
    b2 = b.reshape(1, 2 * n)
    return pl.pallas_call(
        _modglu_kernel,
        grid=(n_rows // tm, nj),
        in_specs=[pl.BlockSpec((tm, k), lambda i, j: (i, 0)),
                  _mod_spec(layer, SC_A, tm, seq), _mod_spec(layer, SH_A, tm, seq),
                  pl.BlockSpec((k, tn), lambda i, j: (0, j)),
                  pl.BlockSpec((k, tn), lambda i, j: (0, j + nj)),
                  pl.BlockSpec((1, tn), lambda i, j: (0, j)),
                  pl.BlockSpec((1, tn), lambda i, j: (0, j + nj))],
        out_specs=pl.BlockSpec((tm, tn), lambda i, j: (i, j)),
        out_shape=jax.ShapeDtypeStruct((n_rows, n), F32),
        scratch_shapes=[pltpu.VMEM((tm, k), BF16)],
        compiler_params=_params(("parallel", "arbitrary")),
        name="mod_pw1_glu",
    )(h, mods, mods, w, w, b2, b2)


def _na_block_window(r0, rows):
    return jnp.clip(r0 - WIN_ROWS // 2, 0, rows - NA_KEY_ROWS)


def _na_block_kinds(rows):
    def geometry(r0):
        k0 = min(max(r0 - WIN_ROWS // 2, 0), rows - NA_KEY_ROWS)
        out = []
        for r in range(r0, r0 + NA_BLOCK_ROWS):
            rs = min(max(r - WIN_ROWS // 2, 0), rows - WIN_ROWS)
            assert 0 <= rs - k0 <= NA_KEY_ROWS - WIN_ROWS
            out.append((rs - k0, r - rs))
        return tuple(out)

    kinds, kind_of = [], {}
    for r0 in range(0, rows, NA_BLOCK_ROWS):
        geo = geometry(r0)
        if geo not in kinds:
            kinds.append(geo)
        kind_of[r0] = kinds.index(geo)
    return kinds, kind_of


def _na_kernel(q_ref, k_ref, v_ref, kc_ref, vc_ref, bias_ref, o_ref, *, rows):
    scale = HEAD_DIM ** -0.5
    contract_last = (((1,), (1,)), ((), ()))
    row0 = pl.program_id(2) * ROWS_PER_STEP
    n_keys = NA_KEY_ROWS * GRID_W
    bq = NA_BLOCK_ROWS * GRID_W
    n_blk = ROWS_PER_STEP // NA_BLOCK_ROWS
    _, kind_of = _na_block_kinds(rows)
    kc = kc_ref[...]
    vc = vc_ref[...]

    def block_kind(r0):
        common = max(set(kind_of.values()), key=list(kind_of.values()).count)
        kind = jnp.int32(common)
        for r_static, kd in kind_of.items():
            if kd != common:
                kind = jnp.where(r0 == r_static, kd, kind)
        return kind

    def key_start(b):
        return pl.multiple_of(_na_block_window(row0 + b * NA_BLOCK_ROWS, rows) * GRID_W, GRID_W)

    def scores(b):
        q = q_ref[b * bq:(b + 1) * bq, :]
        kw = k_ref[pl.ds(key_start(b), n_keys), :]
        bias = bias_ref[0, block_kind(row0 + b * NA_BLOCK_ROWS)]
        s_loc = lax.dot_general(q, kw, contract_last, preferred_element_type=F32) * scale + bias
        s_ctx = lax.dot_general(q, kc, contract_last, preferred_element_type=F32) * scale
        return s_loc, s_ctx

    def finish(b, s_loc, s_ctx):
        vw = v_ref[pl.ds(key_start(b), n_keys), :]
        m = jnp.maximum(jnp.max(s_loc, axis=-1, keepdims=True), jnp.max(s_ctx, axis=-1, keepdims=True))
        p_loc = jnp.exp(s_loc - m)
        p_ctx = jnp.exp(s_ctx - m)
        denom = jnp.sum(p_loc, axis=-1, keepdims=True) + jnp.sum(p_ctx, axis=-1, keepdims=True)
        o = (jnp.dot(p_loc.astype(BF16), vw, preferred_element_type=F32)
             + jnp.dot(p_ctx.astype(BF16), vc, preferred_element_type=F32))
        o_ref[b * bq:(b + 1) * bq, :] = (o / denom).astype(o_ref.dtype)

    s_next = scores(0)
    for b in range(n_blk):
        s_cur = s_next
        if b + 1 < n_blk:
            s_next = scores(b + 1)
        finish(b, *s_cur)


def _na_block_bias(row_table, rows):
    n_heads = row_table.shape[0]
    kinds, _ = _na_block_kinds(rows)
    per_kind = []
    for geo in kinds:
        per_row = []
        for rel, off in geo:
            before = jnp.full((n_heads, GRID_W, rel * GRID_W), NEG_INF, F32)
            after = jnp.full((n_heads, GRID_W, (NA_KEY_ROWS - WIN_ROWS - rel) * GRID_W), NEG_INF, F32)
            per_row.append(jnp.concatenate([before, row_table[:, off], after], axis=-1))
        per_kind.append(jnp.concatenate(per_row, axis=1))
    return jnp.stack(per_kind, axis=1)


def _na_bias_table(rpb):
    col = jnp.arange(GRID_W)
    cstart = jnp.clip(col - WIN_COLS // 2, 0, GRID_W - WIN_COLS)
    col_ok = (col[None, :] >= cstart[:, None]) & (col[None, :] < cstart[:, None] + WIN_COLS)
    cidx = jnp.clip(col[None, :] - col[:, None] + WIN_COLS - 1, 0, 2 * WIN_COLS - 2)
    onehot = (cidx[:, :, None] == jnp.arange(2 * WIN_COLS - 1)).astype(F32)
    t = jnp.einsum('hrc,qkc->hrqk', rpb.astype(F32), onehot, precision=lax.Precision.HIGHEST)
    t = jnp.where(col_ok[None, None], t, NEG_INF)
    t = jnp.stack([t[:, WIN_ROWS - 1 - off:2 * WIN_ROWS - 1 - off] for off in range(WIN_ROWS)], axis=1)
    t = t.transpose(0, 1, 3, 2, 4)
    return t.reshape(rpb.shape[0], WIN_ROWS, GRID_W, WIN_ROWS * GRID_W)


def _na_attention(qkv, bias_table, n_batch, seq, ctx_len, n_rows_out):
    rows = seq // GRID_W
    groups = rows // ROWS_PER_STEP
    tq = ROWS_PER_STEP * GRID_W
    ctx_blk0 = (n_batch * seq) // ctx_len
    h_ = N_HEADS
    block_bias = _na_block_bias(bias_table, rows)
    n_kinds = block_bias.shape[1]
    return pl.pallas_call(
        functools.partial(_na_kernel, rows=rows),
        grid=(n_batch, h_, groups),
        in_specs=[pl.BlockSpec((tq, HEAD_DIM), lambda b, h, g: (b * groups + g, h)),
                  pl.BlockSpec((seq, HEAD_DIM), lambda b, h, g: (b, h_ + h)),
                  pl.BlockSpec((seq, HEAD_DIM), lambda b, h, g: (b, 2 * h_ + h)),
                  pl.BlockSpec((ctx_len, HEAD_DIM), lambda b, h, g: (ctx_blk0 + b, h_ + h)),
                  pl.BlockSpec((ctx_len, HEAD_DIM), lambda b, h, g: (ctx_blk0 + b, 2 * h_ + h)),
                  pl.BlockSpec((1, n_kinds, NA_BLOCK_ROWS * GRID_W, NA_KEY_ROWS * GRID_W),
                               lambda b, h, g: (h, 0, 0, 0))],
        out_specs=pl.BlockSpec((tq, HEAD_DIM), lambda b, h, g: (b * groups + g, h)),
        out_shape=jax.ShapeDtypeStruct((n_rows_out, D_MODEL), BF16),
        compiler_params=_params(("parallel", "parallel", "arbitrary")),
        name="na_attention",
    )(qkv, qkv, qkv, qkv, qkv, block_bias)


def _ctx_attn_kernel(q_ref, k_ref, v_ref, prev_ref, o_ref):
    del prev_ref
    scale = HEAD_DIM ** -0.5
    s = lax.dot_general(q_ref[...], k_ref[...], (((1,), (1,)), ((), ())), preferred_element_type=F32) * scale
    m = jnp.max(s, axis=-1, keepdims=True)
    p = jnp.exp(s - m)
    denom = jnp.sum(p, axis=-1, keepdims=True)
    o = jnp.dot(p.astype(BF16), v_ref[...], preferred_element_type=F32)
    o_ref[...] = (o / denom).astype(o_ref.dtype)


def _ctx_attention(qkv, attn_out, n_batch, seq, ctx_len):
    ctx_blk0 = (n_batch * seq) // ctx_len
    h_ = N_HEADS
    return pl.pallas_call(
        _ctx_attn_kernel,
        grid=(n_batch, h_),
        in_specs=[pl.BlockSpec((ctx_len, HEAD_DIM), lambda b, h: (ctx_blk0 + b, h)),
                  pl.BlockSpec((ctx_len, HEAD_DIM), lambda b, h: (ctx_blk0 + b, h_ + h)),
                  pl.BlockSpec((ctx_len, HEAD_DIM), lambda b, h: (ctx_blk0 + b, 2 * h_ + h)),
                  pl.BlockSpec(memory_space=pl.ANY)],
        out_specs=pl.BlockSpec((ctx_len, HEAD_DIM), lambda b, h: (ctx_blk0 + b, h)),
        out_shape=jax.ShapeDtypeStruct(attn_out.shape, attn_out.dtype),
        input_output_aliases={3: 0},
        compiler_params=_params(("parallel", "parallel")),
        name="ctx_attention",
    )(qkv, qkv, qkv, attn_out)


def _dwconv_kernel(x_ref, w_ref, b_ref, o_ref, xpad, *, seq):
    half = CONV_WIDTH // 2
    lead = 2 * SUBLANES
    ch = x_ref.shape[1]
    xpad[pl.ds(0, lead), :] = jnp.zeros((lead, ch), F32)
    xpad[pl.ds(lead + seq, lead), :] = jnp.zeros((lead, ch), F32)
    xpad[pl.ds(lead, seq), :] = x_ref[...]
    w = w_ref[...]
    bias = b_ref[...]
    win_rows = CONV_TT + 2 * lead

    def body(c, carry):
        t0 = pl.multiple_of(c * CONV_TT, CONV_TT)
        win = xpad[pl.ds(t0, win_rows), :]
        acc = jnp.broadcast_to(bias, (CONV_TT, ch))
        for p in range(SUBLANES):
            shifted = win if p == 0 else pltpu.roll(win, win_rows - p, axis=0)
            for k in range(CONV_WIDTH):
                o = lead - half + k
                if o % SUBLANES == p:
                    acc = acc + shifted[o - p:o - p + CONV_TT, :] * w[k:k + 1, :]
        o_ref[pl.ds(t0, CONV_TT), :] = acc
        return carry

    lax.fori_loop(0, seq // CONV_TT, body, 0)


def _dwconv(x, w_dw, b_dw, n_batch, seq):
    d = x.shape[1]
    nc = d // CONV_CH
    return pl.pallas_call(
        functools.partial(_dwconv_kernel, seq=seq),
        grid=(n_batch, nc),
        in_specs=[pl.BlockSpec((seq, CONV_CH), lambda b, c: (b, c)),
                  pl.BlockSpec((CONV_WIDTH, CONV_CH), lambda b, c: (0, c)),
                  pl.BlockSpec((1, CONV_CH), lambda b, c: (0, c))],
        out_specs=pl.BlockSpec((seq, CONV_CH), lambda b, c: (b, c)),
        out_shape=jax.ShapeDtypeStruct((n_batch * seq, d), F32),
        scratch_shapes=[pltpu.VMEM((seq + 4 * SUBLANES, CONV_CH), F32)],
        compiler_params=_params(("parallel", "parallel")),
        name="dwconv",
    )(x, w_dw, b_dw.reshape(1, d))


def _proj_ln_kernel(*refs, pre_ln, has_bias, n_lat_tiles):
    it = iter(refs)
    a_ref = next(it)
    pre_g = next(it) if pre_ln else None
    pre_b = next(it) if pre_ln else None
    w_ref = next(it)
    bias_ref = next(it) if has_bias else None
    h_ref = next(it)
    hc_ref = next(it) if n_lat_tiles is not None else None
    gate_ref, lng_ref, lnb_ref, scf_ref, shf_ref, wr_ref = (next(it) for _ in range(6))
    hnew_ref, fin_ref, score_ref = (next(it) for _ in range(3))

    tm = a_ref.shape[0]
    half = tm // PROJ_SPLIT

    def project(p):
        a = a_ref[p * half:(p + 1) * half, :]
        if pre_ln:
            a = _silu(_layer_norm(a, pre_g[...], pre_b[...]))
        return jnp.dot(a.astype(BF16), w_ref[...], preferred_element_type=F32)

    def finish(p, m):
        rows = slice(p * half, (p + 1) * half)
        if has_bias:
            m = m + bias_ref[...]
        h_old = h_ref[rows, :]
        if n_lat_tiles is not None:
            h_old = jnp.where(pl.program_id(0) < n_lat_tiles, h_old, hc_ref[rows, :])
        hn = _layer_norm(DEEPNORM_ALPHA * h_old + gate_ref[0] * m, lng_ref[...], lnb_ref[...])
        hnew_ref[rows, :] = hn
        f = hn * (1.0 + scf_ref[0]) + shf_ref[0]
        _store_packed_rows(fin_ref.at[pl.ds(p * half * SUBLANES, half * SUBLANES)], f)
        score_ref[rows, :] = jax.nn.sigmoid(jnp.dot(f.astype(BF16), wr_ref[...], preferred_element_type=F32))

    m_next = project(0)
    for p in range(PROJ_SPLIT):
        m_cur = m_next
        if p + 1 < PROJ_SPLIT:
            m_next = project(p + 1)
        finish(p, m_cur)


def _proj_ln(a, w, bias, h, h_ctx, mods, layer, ln_g, ln_b, w_router, n_rows, seq, pre_ln=None):
    d = D_MODEL
    tm = PROJ_TILE
    row = lambda i: (i, 0)
    const = lambda i: (0, 0)
    n_lat_tiles = None if h_ctx is None else h.shape[0] // tm
    args, specs = [a], [pl.BlockSpec((tm, d), row)]
    if pre_ln is not None:
        args += [pre_ln[0].reshape(1, d), pre_ln[1].reshape(1, d)]
        specs += [pl.BlockSpec((1, d), const)] * 2
    args.append(w)
    specs.append(pl.BlockSpec((d, d), const))
    if bias is not None:
        args.append(bias.reshape(1, d))
        specs.append(pl.BlockSpec((1, d), const))
    if h_ctx is None:
        args.append(h)
        specs.append(pl.BlockSpec((tm, d), row))
    else:
        args += [h, h_ctx]
        specs += [pl.BlockSpec((tm, d), lambda i: (jnp.minimum(i, n_lat_tiles - 1), 0)),
                  pl.BlockSpec((tm, d), lambda i: (jnp.maximum(i - n_lat_tiles, 0), 0))]
    args += [mods, ln_g.reshape(1, d), ln_b.reshape(1, d), mods, mods, w_router]
    specs += [_mod_spec(layer, G_A, tm, seq), pl.BlockSpec((1, d), const), pl.BlockSpec((1, d), const),
              _mod_spec(layer, SC_F, tm, seq), _mod_spec(layer, SH_F, tm, seq),
              pl.BlockSpec((d, N_EXPERTS), const)]
    return pl.pallas_call(
        functools.partial(_proj_ln_kernel, pre_ln=pre_ln is not None, has_bias=bias is not None,
                          n_lat_tiles=n_lat_tiles),
        grid=(n_rows // tm,),
        in_specs=specs,
        out_specs=[pl.BlockSpec((tm, d), row), pl.BlockSpec((tm * SUBLANES, LANES), row),
                   pl.BlockSpec((tm, N_EXPERTS), row)],
        out_shape=[jax.ShapeDtypeStruct((n_rows, d), F32),
                   jax.ShapeDtypeStruct((n_rows * SUBLANES, LANES), jnp.uint32),
                   jax.ShapeDtypeStruct((n_rows, N_EXPERTS), F32)],
        compiler_params=_params(("parallel",)),
        name="proj_ln",
    )(*args)


def _router_kernel(s_ref, b_ref, ek_ref, rk_ref, gk_ref, cnt_ref, carry):
    tm, e = s_ref.shape

    @pl.when(pl.program_id(0) == 0)
    def _():
        carry[...] = jnp.zeros_like(carry)

    s = s_ref[...]
    lane = lax.broadcasted_iota(jnp.int32, (tm, e), 1).astype(F32)
    slot = lax.broadcasted_iota(jnp.int32, (tm, 128), 1)
    work = s + b_ref[...]
    sel = jnp.zeros((tm, e), F32)
    e_acc = jnp.zeros((tm, 128), F32)
    g_acc = jnp.zeros((tm, 128), F32)
    g_sum = jnp.zeros((tm, 1), F32)
    picks = []
    for k in range(TOP_K):
        m = jnp.max(work, axis=-1, keepdims=True)
        idx = jnp.min(jnp.where(work == m, lane, float(e)), axis=-1, keepdims=True)
        onehot = lane == idx
        gk = jnp.sum(jnp.where(onehot, s, 0.0), axis=-1, keepdims=True)
        sel = jnp.where(onehot, 1.0, sel)
        work = jnp.where(onehot, -jnp.inf, work)
        e_acc = jnp.where(slot == k, idx, e_acc)
        g_acc = jnp.where(slot == k, gk, g_acc)
        g_sum = g_sum + gk
        picks.append(idx)

    r_i = lax.broadcasted_iota(jnp.int32, (tm, tm), 0)
    c_i = lax.broadcasted_iota(jnp.int32, (tm, tm), 1)
    tri = jnp.where(c_i < r_i, 1.0, 0.0).astype(BF16)
    rank = jnp.dot(tri, sel.astype(BF16), preferred_element_type=F32) + carry[...]
    r_acc = jnp.zeros((tm, 128), F32)
    for k in range(TOP_K):
        rk = jnp.sum(jnp.where(lane == picks[k], rank, 0.0), axis=-1, keepdims=True)
        r_acc = jnp.where(slot == k, rk, r_acc)

    carry[...] = carry[...] + jnp.sum(sel, axis=0, keepdims=True)
    ek_ref[...] = e_acc[:, :TOP_K].astype(jnp.int32)
    rk_ref[...] = r_acc[:, :TOP_K].astype(jnp.int32)
    gk_ref[...] = (g_acc / g_sum * ROUTED_SCALE)[:, :TOP_K]
    cnt_ref[...] = carry[...].astype(jnp.int32)


def _router(scores, router_bias, n_rows):
    e = N_EXPERTS
    tm = ROUTER_TILE
    row = lambda i: (i, 0)
    return pl.pallas_call(
        _router_kernel,
        grid=(n_rows // tm,),
        in_specs=[pl.BlockSpec((tm, e), row), pl.BlockSpec((1, e), lambda i: (0, 0))],
        out_specs=[pl.BlockSpec((tm, TOP_K), row), pl.BlockSpec((tm, TOP_K), row), pl.BlockSpec((tm, TOP_K), row),
                   pl.BlockSpec((1, e), lambda i: (0, 0))],
        out_shape=[jax.ShapeDtypeStruct((n_rows, TOP_K), jnp.int32), jax.ShapeDtypeStruct((n_rows, TOP_K), jnp.int32),
                   jax.ShapeDtypeStruct((n_rows, TOP_K), F32), jax.ShapeDtypeStruct((1, e), jnp.int32)],
        scratch_shapes=[pltpu.VMEM((1, e), F32)],
        compiler_params=_params(("arbitrary",)),
        name="router",
    )(scores, router_bias.reshape(1, e).astype(F32))


def _layout(e_k, r_k, counts, tm_e):
    e = N_EXPERTS
    n_assign = e_k.size
    counts = counts.reshape(e)
    padded = (counts + tm_e - 1) // tm_e * tm_e
    pends = jnp.cumsum(padded)
    pstarts = pends - padded
    experts = jnp.arange(e, dtype=jnp.int32)
    start_of = jnp.sum(jnp.where(e_k[..., None] == experts, pstarts, 0), axis=-1)
    dest = (start_of + r_k).reshape(n_assign).astype(jnp.int32)
    n_blocks = (n_assign + e * (tm_e - 1) + tm_e - 1) // tm_e
    blk_row0 = jnp.arange(n_blocks, dtype=jnp.int32) * tm_e
    blk_e = jnp.minimum(jnp.sum(pends[None, :] <= blk_row0[:, None], axis=-1), e - 1).astype(jnp.int32)
    n_valid = (pends[-1] // tm_e).astype(jnp.int32).reshape(1)
    pad_info = jnp.concatenate([pstarts + counts, padded - counts]).astype(jnp.int32)
    later = (experts[None, :] > experts[:, None]) & (counts[None, :] > 0)
    next_e = jnp.min(jnp.where(later, experts[None, :], e), axis=1)
    next_e = jnp.where(next_e == e, -1, next_e)
    order_e = jnp.sum((experts[None, :] < experts[:, None]) & (counts[None, :] > 0), axis=1)
    chain = jnp.concatenate([next_e, order_e]).astype(jnp.int32)
    return dest, blk_e, n_valid, pad_info, chain, n_blocks


def _dispatch_kernel(pad_ref, dest_ref, f_ref, xs_ref, zbuf, sem, zsem):
    tm = f_ref.shape[0] // SUBLANES
    pad_bits = EXPERT_TILE.bit_length() - 1

    def zero_copies(e_idx):
        start = pad_ref[e_idx]
        pad = pad_ref[N_EXPERTS + e_idx]
        return [(((pad >> b) & 1) == 1, pltpu.make_async_copy(
            _row_tile(zbuf, 0, 1 << b), _row_tile(xs_ref, start + (pad & ((1 << b) - 1)), 1 << b), zsem))
            for b in range(pad_bits)]

    @pl.when(pl.program_id(0) == 0)
    def _():
        zbuf[...] = jnp.zeros_like(zbuf)

        def start_e(e_idx, carry):
            for cond, cp in zero_copies(e_idx):
                pl.when(cond)(cp.start)
            return carry

        def wait_e(e_idx, carry):
            for cond, cp in zero_copies(e_idx):
                pl.when(cond)(cp.wait)
            return carry

        lax.fori_loop(0, N_EXPERTS, start_e, 0)
        lax.fori_loop(0, N_EXPERTS, wait_e, 0)

    def tok(t, carry):
        for k in range(TOP_K):
            d = dest_ref[t * TOP_K + k]
            pltpu.make_async_copy(_row_tile(f_ref, t), _row_tile(xs_ref, d), sem).start(priority=k % 2)
        return carry

    lax.fori_loop(0, tm, tok, 0)
    for k in range(TOP_K):
        pltpu.make_async_copy(f_ref, _row_tile(xs_ref, 0, tm), sem).wait()


def _dispatch(f_in, dest, pad_info, n_rows, n_blocks):
    tm = ROW_TILE
    grid_spec = pltpu.PrefetchScalarGridSpec(
        num_scalar_prefetch=1,
        grid=(n_rows // tm,),
        in_specs=[pl.BlockSpec((tm * TOP_K,), lambda i, p: (i,), memory_space=pltpu.SMEM),
                  pl.BlockSpec((tm * SUBLANES, LANES), lambda i, p: (i, 0))],
        out_specs=pl.BlockSpec(memory_space=pl.ANY),
        scratch_shapes=[pltpu.VMEM((EXPERT_TILE // 2 * SUBLANES, LANES), f_in.dtype), pltpu.SemaphoreType.DMA,
                        pltpu.SemaphoreType.DMA],
    )
    return pl.pallas_call(
        _dispatch_kernel,
        grid_spec=grid_spec,
        out_shape=jax.ShapeDtypeStruct((n_blocks * EXPERT_TILE * SUBLANES, LANES), f_in.dtype),
        compiler_params=_params(("arbitrary",)),
        name="dispatch",
    )(pad_info, dest, f_in)


def _swiglu_hidden(x, wg_ref, wu_ref):
    gv = jnp.dot(x, wg_ref[...], preferred_element_type=F32)
    uv = jnp.dot(x, wu_ref[...], preferred_element_type=F32)
    return (_silu(gv) * uv).astype(BF16)


def _expert_kernel(be_ref, nv_ref, chain_ref, x_ref, wg_hbm, wu_hbm, wd_hbm, y_ref,
                   wg32, wu32, wd32, wgb, wub, wdb, sems, *, layer):
    i = pl.program_id(0)
    valid = i < nv_ref[0]
    e_cur = be_ref[i]
    first = jnp.logical_or(i == 0, e_cur != be_ref[jnp.maximum(i - 1, 0)])
    d, de = wgb.shape
    n_chunks = 8

    def weight_copies(e_idx, slot):
        return [pltpu.make_async_copy(w_hbm.at[layer, e_idx], buf.at[slot], sems.at[slot, j])
                for j, (w_hbm, buf) in enumerate(((wg_hbm, wg32), (wu_hbm, wu32), (wd_hbm, wd32)))]

    @pl.when(jnp.logical_and(valid, first))
    def _():
        slot = chain_ref[N_EXPERTS + e_cur] & 1
        e_next = chain_ref[e_cur]

        @pl.when(i == 0)
        def _():
            for cp in weight_copies(e_cur, slot):
                cp.start()

        for cp in weight_copies(e_cur, slot):
            cp.wait()

        @pl.when(e_next >= 0)
        def _():
            for cp in weight_copies(e_next, 1 - slot):
                cp.start()

        def cast_chunk(c, carry):
            r0 = pl.multiple_of(c * (d // n_chunks), d // n_chunks)
            wgb[pl.ds(r0, d // n_chunks), :] = wg32[slot, pl.ds(r0, d // n_chunks), :].astype(BF16)
            wub[pl.ds(r0, d // n_chunks), :] = wu32[slot, pl.ds(r0, d // n_chunks), :].astype(BF16)
            r1 = pl.multiple_of(c * (de // n_chunks), de // n_chunks)
            wdb[pl.ds(r1, de // n_chunks), :] = wd32[slot, pl.ds(r1, de // n_chunks), :].astype(BF16)
            return carry

        lax.fori_loop(0, n_chunks, cast_chunk, 0)

    @pl.when(valid)
    def _():
        act = _swiglu_hidden(_load_packed_rows_bf16(x_ref), wgb, wub)
        _store_packed_rows(y_ref, jnp.dot(act, wdb[...], preferred_element_type=F32))


def _expert_ffn(xs, blk_e, n_valid, chain, w_gate, w_up, w_down, layer):
    n_rows = xs.shape[0] // SUBLANES
    d, de = w_gate.shape[2], w_gate.shape[3]
    tm = EXPERT_TILE
    n_blocks = n_rows // tm

    def xmap(i, be, nv, ch):
        return (jnp.minimum(i, nv[0] - 1), 0)

    grid_spec = pltpu.PrefetchScalarGridSpec(
        num_scalar_prefetch=3,
        grid=(n_blocks,),
        in_specs=[pl.BlockSpec((tm * SUBLANES, LANES), xmap),
                  pl.BlockSpec(memory_space=pl.ANY),
                  pl.BlockSpec(memory_space=pl.ANY),
                  pl.BlockSpec(memory_space=pl.ANY)],
        out_specs=pl.BlockSpec((tm * SUBLANES, LANES), xmap),
        scratch_shapes=[pltpu.VMEM((2, d, de), F32), pltpu.VMEM((2, d, de), F32), pltpu.VMEM((2, de, d), F32),
                        pltpu.VMEM((d, de), BF16), pltpu.VMEM((d, de), BF16), pltpu.VMEM((de, d), BF16),
                        pltpu.SemaphoreType.DMA((2, 3))],
    )
    return pl.pallas_call(
        functools.partial(_expert_kernel, layer=layer),
        grid_spec=grid_spec,
        out_shape=jax.ShapeDtypeStruct((n_rows * SUBLANES, LANES), jnp.uint32),
        compiler_params=_params(("arbitrary",)),
        name="expert_ffn",
    )(blk_e, n_valid, chain, xs, w_gate, w_up, w_down)


def _ffn_out_kernel(dcur_ref, dnext_ref, f_ref, gk_ref, wg_ref, wu_ref, wd_ref, h_ref, gate_ref, lng_ref, lnb_ref,
                    y_ref, o_ref, ybuf, sems):
    i = pl.program_id(0)
    n_steps = pl.num_programs(0)
    tm = f_ref.shape[0] // SUBLANES
    slot = lax.rem(i, 2)

    def gather_token(dref, s, t):
        for k in range(TOP_K):
            row = dref[t * TOP_K + k]
            pltpu.make_async_copy(_row_tile(y_ref, row), _row_tile(ybuf.at[s, k], t),
                                  sems.at[s]).start(priority=k % 2)

    def wait_gathers(s):
        for k in range(TOP_K):
            pltpu.make_async_copy(_row_tile(y_ref, 0, tm), ybuf.at[s, k], sems.at[s]).wait()

    @pl.when(i == 0)
    def _():
        def tok(t, carry):
            gather_token(dcur_ref, slot, t)
            return carry

        lax.fori_loop(0, tm, tok, 0)

    n_pieces = 2 + SUBLANES
    piece = [(p * tm) // n_pieces for p in range(n_pieces + 1)]

    def issue_piece(p):
        for t in range(piece[p], piece[p + 1]):
            gather_token(dnext_ref, 1 - slot, t)

    issue_piece(0)
    act = _swiglu_hidden(_load_packed_rows_bf16(f_ref), wg_ref, wu_ref)
    issue_piece(1)
    f = jnp.dot(act, wd_ref[...], preferred_element_type=F32)

    gk = gk_ref[...]
    wait_gathers(slot)
    gcols = [jnp.broadcast_to(gk[:, k:k + 1], (tm, LANES)) for k in range(TOP_K)]
    parts = []
    for s_ in range(SUBLANES):
        issue_piece(2 + s_)
        r_lo = jnp.zeros((tm, LANES), F32)
        r_hi = jnp.zeros((tm, LANES), F32)
        for k in range(TOP_K):
            y_lo, y_hi = _load_packed_group(ybuf.at[slot, k], s_)
            r_lo = r_lo + gcols[k] * y_lo
            r_hi = r_hi + gcols[k] * y_hi
        parts += [r_lo, r_hi]
    f = f + jnp.concatenate(parts, axis=1)
    o_ref[...] = _layer_norm(DEEPNORM_ALPHA * h_ref[...] + gate_ref[0] * f, lng_ref[...], lnb_ref[...])

    @pl.when(i == n_steps - 1)
    def _():
        wait_gathers(1 - slot)


def _ffn_out(f_in, y, dest, gates, ws_gate, ws_up, ws_down, h, mods, layer, ln_g, ln_b, n_rows, seq):
    d = D_MODEL
    ds_ = ws_gate.shape[1]
    tm = COMBINE_TILE
    n_steps = n_rows // tm
    row = lambda i: (i, 0)
    const = lambda i: (0, 0)
    return pl.pallas_call(
        _ffn_out_kernel,
        grid=(n_steps,),
        in_specs=[pl.BlockSpec((tm * TOP_K,), lambda i: (i,), memory_space=pltpu.SMEM),
                  pl.BlockSpec((tm * TOP_K,), lambda i: (jnp.minimum(i + 1, n_steps - 1),), memory_space=pltpu.SMEM),
                  pl.BlockSpec((tm * SUBLANES, LANES), row), pl.BlockSpec((tm, TOP_K), row),
                  pl.BlockSpec((d, ds_), const), pl.BlockSpec((d, ds_), const), pl.BlockSpec((ds_, d), const),
                  pl.BlockSpec((tm, d), row), _mod_spec(layer, G_F, tm, seq),
                  pl.BlockSpec((1, d), const), pl.BlockSpec((1, d), const),
                  pl.BlockSpec(memory_space=pl.ANY)],
        out_specs=pl.BlockSpec((tm, d), row),
        out_shape=jax.ShapeDtypeStruct((n_rows, d), F32),
        scratch_shapes=[pltpu.VMEM((2, TOP_K, tm * SUBLANES, LANES), jnp.uint32), pltpu.SemaphoreType.DMA((2,))],
        compiler_params=_params(("arbitrary",)),
        name="ffn_out",
    )(dest, dest, f_in, gates, ws_gate, ws_up, ws_down, h, mods, ln_g.reshape(1, d), ln_b.reshape(1, d), y)


def _moe_and_norm(h, f_in, scores, mods, router_bias, we_gate, we_up, we_down, ws_gate, ws_up, ws_down,
                  ln_g, ln_b, n_rows, seq, layer):
    e_k, r_k, gates, counts = _router(scores, router_bias, n_rows)
    dest, blk_e, n_valid, pad_info, chain, n_blocks = _layout(e_k, r_k, counts, EXPERT_TILE)
    xs = _dispatch(f_in, dest, pad_info, n_rows, n_blocks)
    y = _expert_ffn(xs, blk_e, n_valid, chain, we_gate, we_up, we_down, layer)
    return _ffn_out(f_in, y, dest, gates, ws_gate.astype(BF16), ws_up.astype(BF16), ws_down.astype(BF16),
                    h, mods, layer, ln_g, ln_b, n_rows, seq)


def kernel(x, c, ctx, c_ctx, mod_w, mod_b, ln_mix_g, ln_mix_b, ln_ffn_g, ln_ffn_b, na_w_qkv, na_rpb, na_w_o,
           cv_w_pw1, cv_b_pw1, cv_w_dw, cv_b_dw, cv_ln_g, cv_ln_b, cv_w_pw2, cv_b_pw2, moe_w_router,
           moe_router_bias, moe_w_gate, moe_w_up, moe_w_down, sh_w_gate, sh_w_up, sh_w_down):
    n_batch, seq, d = x.shape
    ctx_len = ctx.shape[1]
    n_lat = n_batch * seq
    n_tok = n_lat + n_batch * ctx_len

    cvec = jnp.zeros((MOD_ROWS, d), F32).at[:n_batch].set(c).at[n_batch].set(c_ctx)
    mods = _mod_matmul(cvec, mod_w, mod_b).reshape(DEPTH * MOD_ROWS * MOD_PIECES, 1, d)
    x_lat = x.reshape(n_lat, d)
    x_ctx = ctx.reshape(n_batch * ctx_len, d)

    qkv = _mod_matmul_qkv(x_lat, x_ctx, mods, 0, na_w_qkv[0].astype(BF16), seq)
    attn = _na_attention(qkv, _na_bias_table(na_rpb[0]), n_batch, seq, ctx_len, n_tok)
    attn = _ctx_attention(qkv, attn, n_batch, seq, ctx_len)
    h, f_in, scores = _proj_ln(attn, na_w_o[0].astype(BF16), None, x_lat, x_ctx, mods, 0, ln_mix_g[0], ln_mix_b[0],
                               moe_w_router[0].astype(BF16), n_tok, seq)
    h = _moe_and_norm(h, f_in, scores, mods, moe_router_bias[0], moe_w_gate, moe_w_up, moe_w_down,
                      sh_w_gate[0], sh_w_up[0], sh_w_down[0], ln_ffn_g[0], ln_ffn_b[0], n_tok, seq, 0)

    glu = _mod_matmul_glu(h, mods, 1, cv_w_pw1[0].astype(BF16), cv_b_pw1[0], n_lat, seq)
    conv = _dwconv(glu, cv_w_dw[0], cv_b_dw[0], n_batch, seq)
    h, f_in, scores = _proj_ln(conv, cv_w_pw2[0].astype(BF16), cv_b_pw2[0], h, None, mods, 1, ln_mix_g[1],
                               ln_mix_b[1], moe_w_router[1].astype(BF16), n_lat, seq,
                               pre_ln=(cv_ln_g[0], cv_ln_b[0]))
    h = _moe_and_norm(h, f_in, scores, mods, moe_router_bias[1], moe_w_gate, moe_w_up, moe_w_down,
                      sh_w_gate[1], sh_w_up[1], sh_w_down[1], ln_ffn_g[1], ln_ffn_b[1], n_lat, seq, 1)
    return h.reshape(n_batch, seq, d)
```

```python
import functools

import jax
import jax.numpy as jnp
from jax import lax
from jax.experimental import pallas as pl
from jax.experimental.pallas import tpu as pltpu

D_MODEL = 2048
GRID_W = 64
N_HEADS = 16
HEAD_DIM = D_MODEL // N_HEADS
WIN_ROWS = 8
WIN_COLS = 16
CONV_WIDTH = 31
N_EXPERTS = 64
TOP_K = 8
ROUTED_SCALE = 2.5
LN_EPS = 1e-6
SUBLANES = 8
NEG_INF = -1e30
DEPTH = 2
DEEPNORM_ALPHA = (2 * DEPTH) ** 0.25

F32 = jnp.float32
BF16 = jnp.bfloat16

VMEM_LIMIT_BYTES = 56 * 1024 * 1024
DISPATCH_TILE = 512
PROJ_TILE = 512
PROJ_SPLIT = 2
ROUTER_TILE = 512
MM_ROW_TILE = 512
MOD_COL_TILE = 1024
QKV_COL_TILE = 2048
GLU_COL_TILE = 1024
EXPERT_TILE = 512
COMBINE_TILE = 128
ROWS_PER_STEP = 16
NA_BLOCK_ROWS = 2
NA_KEY_ROWS = WIN_ROWS + NA_BLOCK_ROWS
CONV_CH = 256
CONV_TT = 128


def _params(sem):
    return pltpu.CompilerParams(dimension_semantics=sem, vmem_limit_bytes=VMEM_LIMIT_BYTES)


def _layer_norm(z, g, b):
    mu = jnp.mean(z, axis=-1, keepdims=True)
    zc = z - mu
    var = jnp.mean(zc * zc, axis=-1, keepdims=True)
    return zc * lax.rsqrt(var + LN_EPS) * g + b


def _silu(x):
    return x * jax.nn.sigmoid(x)


LANES = 128
ROW_WORDS = D_MODEL // 2
assert ROW_WORDS == SUBLANES * LANES


def _store_packed_rows(ref, x):
    rows = x.shape[0]
    for s_ in range(SUBLANES):
        base = 2 * LANES * s_
        lo = pltpu.bitcast(x[:, base:base + LANES].astype(BF16).astype(F32), jnp.uint32)
        hi = pltpu.bitcast(x[:, base + LANES:base + 2 * LANES].astype(BF16).astype(F32), jnp.uint32)
        ref[pl.ds(s_, rows, stride=SUBLANES), :] = (lo >> 16) | (hi & jnp.uint32(0xFFFF0000))


def _load_packed_group(ref, s_):
    w = ref[pl.ds(s_, ref.shape[0] // SUBLANES, stride=SUBLANES), :]
    return pltpu.bitcast(w << 16, F32), pltpu.bitcast(w & jnp.uint32(0xFFFF0000), F32)


def _row_tile(ref, t, n=1):
    return ref.at[pl.ds(pl.multiple_of(t * SUBLANES, SUBLANES), n * SUBLANES)]


def _load_packed_rows_bf16(ref):
    parts = []
    for s_ in range(SUBLANES):
        lo, hi = _load_packed_group(ref, s_)
        parts += [lo.astype(BF16), hi.astype(BF16)]
    return jnp.concatenate(parts, axis=1)


def _mod_kernel(c_ref, w_ref, b_ref, o_ref):
    a = _silu(c_ref[...]).astype(BF16)
    o_ref[0] = jnp.dot(a, w_ref[0].astype(BF16), preferred_element_type=F32) + b_ref[0]


def _mod_matmul(cvec, w, b):
    m, k = cvec.shape
    layers, _, n = w.shape
    tn = MOD_COL_TILE
    return pl.pallas_call(
        _mod_kernel,
        grid=(layers, n // tn),
        in_specs=[pl.BlockSpec((m, k), lambda l, j: (0, 0)),
                  pl.BlockSpec((1, k, tn), lambda l, j: (l, 0, j)),
                  pl.BlockSpec((1, 1, tn), lambda l, j: (l, 0, j))],
        out_specs=pl.BlockSpec((1, m, tn), lambda l, j: (l, 0, j)),
        out_shape=jax.ShapeDtypeStruct((layers, m, n), F32),
        compiler_params=_params(("parallel", "arbitrary")),
        name="mod_matmul",
    )(cvec, w, b.reshape(layers, 1, n))


def _modmm_kernel(x_ref, c_ref, sc_ref, sh_ref, w_ref, o_ref, a_scr, *, n_lat_tiles):
    @pl.when(pl.program_id(1) == 0)
    def _():
        h = jnp.where(pl.program_id(0) < n_lat_tiles, x_ref[...], c_ref[...])
        a_scr[...] = (h * (1.0 + sc_ref[0]) + sh_ref[0]).astype(BF16)

    o_ref[...] = jnp.dot(a_scr[...], w_ref[...], preferred_element_type=F32).astype(o_ref.dtype)


def _modglu_kernel(h_ref, sc_ref, sh_ref, wa_ref, wg_ref, ba_ref, bg_ref, o_ref, a_scr):
    @pl.when(pl.program_id(1) == 0)
    def _():
        a_scr[...] = (h_ref[...] * (1.0 + sc_ref[0]) + sh_ref[0]).astype(BF16)

    a = a_scr[...]
    va = jnp.dot(a, wa_ref[...], preferred_element_type=F32) + ba_ref[...]
    vg = jnp.dot(a, wg_ref[...], preferred_element_type=F32) + bg_ref[...]
    o_ref[...] = va * jax.nn.sigmoid(vg)


MOD_ROWS = 8
MOD_PIECES = 6
SH_A, SC_A, G_A, SH_F, SC_F, G_F = range(MOD_PIECES)


def _mod_spec(layer, piece, tile_rows, seq):
    def index(i, *_):
        return ((layer * MOD_ROWS + (i * tile_rows) // seq) * MOD_PIECES + piece, 0, 0)

    return pl.BlockSpec((1, 1, D_MODEL), index)


def _mod_matmul_qkv(x_lat, x_ctx, mods, layer, w, seq):
    n_lat, k = x_lat.shape
    n_rows = n_lat + x_ctx.shape[0]
    n = w.shape[1]
    tm, tn = MM_ROW_TILE, QKV_COL_TILE
    n_lat_tiles = n_lat // tm
    return pl.pallas_call(
        functools.partial(_modmm_kernel, n_lat_tiles=n_lat_tiles),
        grid=(n_rows // tm, n // tn),
        in_specs=[pl.BlockSpec((tm, k), lambda i, j: (jnp.minimum(i, n_lat_tiles - 1), 0)),
                  pl.BlockSpec((tm, k), lambda i, j: (jnp.maximum(i - n_lat_tiles, 0), 0)),
                  _mod_spec(layer, SC_A, tm, seq), _mod_spec(layer, SH_A, tm, seq),
                  pl.BlockSpec((k, tn), lambda i, j: (0, j))],
        out_specs=pl.BlockSpec((tm, tn), lambda i, j: (i, j)),
        out_shape=jax.ShapeDtypeStruct((n_rows, n), BF16),
        scratch_shapes=[pltpu.VMEM((tm, k), BF16)],
        compiler_params=_params(("parallel", "arbitrary")),
        name="mod_qkv",
    )(x_lat, x_ctx, mods, mods, w)


def _mod_matmul_glu(h, mods, layer, w, b, n_rows, seq):
    k = h.shape[1]
    n = w.shape[1] // 2
    tm, tn = MM_ROW_TILE, GLU_COL_TILE
    nj = n // tn
    b2 = b.reshape(1, 2 * n)
    return pl.pallas_call(
        _modglu_kernel,
        grid=(n_rows // tm, nj),
        in_specs=[pl.BlockSpec((tm, k), lambda i, j: (i, 0)),
                  _mod_spec(layer, SC_A, tm, seq), _mod_spec(layer, SH_A, tm, seq),
                  pl.BlockSpec((k, tn), lambda i, j: (0, j)),
                  pl.BlockSpec((k, tn), lambda i, j: (0, j + nj)),
                  pl.BlockSpec((1, tn), lambda i, j: (0, j)),
                  pl.BlockSpec((1, tn), lambda i, j: (0, j + nj))],
        out_specs=pl.BlockSpec((tm, tn), lambda i, j: (i, j)),
        out_shape=jax.ShapeDtypeStruct((n_rows, n), F32),
        scratch_shapes=[pltpu.VMEM((tm, k), BF16)],
        compiler_params=_params(("parallel", "arbitrary")),
        name="mod_pw1_glu",
    )(h, mods, mods, w, w, b2, b2)


def _na_block_window(r0, rows):
    return jnp.clip(r0 - WIN_ROWS // 2, 0, rows - NA_KEY_ROWS)


def _na_block_kinds(rows):
    def geometry(r0):
        k0 = min(max(r0 - WIN_ROWS // 2, 0), rows - NA_KEY_ROWS)
        out = []
        for r in range(r0, r0 + NA_BLOCK_ROWS):
            rs = min(max(r - WIN_ROWS // 2, 0), rows - WIN_ROWS)
            assert 0 <= rs - k0 <= NA_KEY_ROWS - WIN_ROWS
            out.append((rs - k0, r - rs))
        return tuple(out)

    kinds, kind_of = [], {}
    for r0 in range(0, rows, NA_BLOCK_ROWS):
        geo = geometry(r0)
        if geo not in kinds:
            kinds.append(geo)
        kind_of[r0] = kinds.index(geo)
    return kinds, kind_of


def _na_kernel(q_ref, k_ref, v_ref, kc_ref, vc_ref, bias_ref, o_ref, *, rows):
    scale = HEAD_DIM ** -0.5
    contract_last = (((1,), (1,)), ((), ()))
    row0 = pl.program_id(2) * ROWS_PER_STEP
    n_keys = NA_KEY_ROWS * GRID_W
    bq = NA_BLOCK_ROWS * GRID_W
    n_blk = ROWS_PER_STEP // NA_BLOCK_ROWS
    _, kind_of = _na_block_kinds(rows)
    kc = kc_ref[...]
    vc = vc_ref[...]

    def block_kind(r0):
        common = max(set(kind_of.values()), key=list(kind_of.values()).count)
        kind = jnp.int32(common)
        for r_static, kd in kind_of.items():
            if kd != common:
                kind = jnp.where(r0 == r_static, kd, kind)
        return kind

    def key_start(b):
        return pl.multiple_of(_na_block_window(row0 + b * NA_BLOCK_ROWS, rows) * GRID_W, GRID_W)

    def scores(b):
        q = q_ref[b * bq:(b + 1) * bq, :]
        kw = k_ref[pl.ds(key_start(b), n_keys), :]
        bias = bias_ref[0, block_kind(row0 + b * NA_BLOCK_ROWS)]
        s_loc = lax.dot_general(q, kw, contract_last, preferred_element_type=F32) * scale + bias
        s_ctx = lax.dot_general(q, kc, contract_last, preferred_element_type=F32) * scale
        return s_loc, s_ctx

    def finish(b, s_loc, s_ctx):
        vw = v_ref[pl.ds(key_start(b), n_keys), :]
        m = jnp.maximum(jnp.max(s_loc, axis=-1, keepdims=True), jnp.max(s_ctx, axis=-1, keepdims=True))
        p_loc = jnp.exp(s_loc - m)
        p_ctx = jnp.exp(s_ctx - m)
        denom = jnp.sum(p_loc, axis=-1, keepdims=True) + jnp.sum(p_ctx, axis=-1, keepdims=True)
        o = (jnp.dot(p_loc.astype(BF16), vw, preferred_element_type=F32)
             + jnp.dot(p_ctx.astype(BF16), vc, preferred_element_type=F32))
        o_ref[b * bq:(b + 1) * bq, :] = (o / denom).astype(o_ref.dtype)

    s_next = scores(0)
    for b in range(n_blk):
        s_cur = s_next
        if b + 1 < n_blk:
            s_next = scores(b + 1)
        finish(b, *s_cur)


def _na_block_bias(row_table, rows):
    n_heads = row_table.shape[0]
    kinds, _ = _na_block_kinds(rows)
    per_kind = []
    for geo in kinds:
        per_row = []
        for rel, off in geo:
            before = jnp.full((n_heads, GRID_W, rel * GRID_W), NEG_INF, F32)
            after = jnp.full((n_heads, GRID_W, (NA_KEY_ROWS - WIN_ROWS - rel) * GRID_W), NEG_INF, F32)
            per_row.append(jnp.concatenate([before, row_table[:, off], after], axis=-1))
        per_kind.append(jnp.concatenate(per_row, axis=1))
    return jnp.stack(per_kind, axis=1)


def _na_bias_table(rpb):
    col = jnp.arange(GRID_W)
    cstart = jnp.clip(col - WIN_COLS // 2, 0, GRID_W - WIN_COLS)
    col_ok = (col[None, :] >= cstart[:, None]) & (col[None, :] < cstart[:, None] + WIN_COLS)
    cidx = jnp.clip(col[None, :] - col[:, None] + WIN_COLS - 1, 0, 2 * WIN_COLS - 2)
    onehot = (cidx[:, :, None] == jnp.arange(2 * WIN_COLS - 1)).astype(F32)
    t = jnp.einsum('hrc,qkc->hrqk', rpb.astype(F32), onehot, precision=lax.Precision.HIGHEST)
    t = jnp.where(col_ok[None, None], t, NEG_INF)
    t = jnp.stack([t[:, WIN_ROWS - 1 - off:2 * WIN_ROWS - 1 - off] for off in range(WIN_ROWS)], axis=1)
    t = t.transpose(0, 1, 3, 2, 4)
    return t.reshape(rpb.shape[0], WIN_ROWS, GRID_W, WIN_ROWS * GRID_W)


def _na_attention(qkv, bias_table, n_batch, seq, ctx_len, n_rows_out):
    rows = seq // GRID_W
    groups = rows // ROWS_PER_STEP
    tq = ROWS_PER_STEP * GRID_W
    ctx_blk0 = (n_batch * seq) // ctx_len
    h_ = N_HEADS
    block_bias = _na_block_bias(bias_table, rows)
    n_kinds = block_bias.shape[1]
    return pl.pallas_call(
        functools.partial(_na_kernel, rows=rows),
        grid=(n_batch, h_, groups),
        in_specs=[pl.BlockSpec((tq, HEAD_DIM), lambda b, h, g: (b * groups + g, h)),
                  pl.BlockSpec((seq, HEAD_DIM), lambda b, h, g: (b, h_ + h)),
                  pl.BlockSpec((seq, HEAD_DIM), lambda b, h, g: (b, 2 * h_ + h)),
                  pl.BlockSpec((ctx_len, HEAD_DIM), lambda b, h, g: (ctx_blk0 + b, h_ + h)),
                  pl.BlockSpec((ctx_len, HEAD_DIM), lambda b, h, g: (ctx_blk0 + b, 2 * h_ + h)),
                  pl.BlockSpec((1, n_kinds, NA_BLOCK_ROWS * GRID_W, NA_KEY_ROWS * GRID_W),
                               lambda b, h, g: (h, 0, 0, 0))],
        out_specs=pl.BlockSpec((tq, HEAD_DIM), lambda b, h, g: (b * groups + g, h)),
        out_shape=jax.ShapeDtypeStruct((n_rows_out, D_MODEL), BF16),
        compiler_params=_params(("parallel", "parallel", "arbitrary")),
        name="na_attention",
    )(qkv, qkv, qkv, qkv, qkv, block_bias)


def _ctx_attn_kernel(q_ref, k_ref, v_ref, prev_ref, o_ref):
    del prev_ref
    scale = HEAD_DIM ** -0.5
    s = lax.dot_general(q_ref[...], k_ref[...], (((1,), (1,)), ((), ())), preferred_element_type=F32) * scale
    m = jnp.max(s, axis=-1, keepdims=True)
    p = jnp.exp(s - m)
    denom = jnp.sum(p, axis=-1, keepdims=True)
    o = jnp.dot(p.astype(BF16), v_ref[...], preferred_element_type=F32)
    o_ref[...] = (o / denom).astype(o_ref.dtype)


def _ctx_attention(qkv, attn_out, n_batch, seq, ctx_len):
    ctx_blk0 = (n_batch * seq) // ctx_len
    h_ = N_HEADS
    return pl.pallas_call(
        _ctx_attn_kernel,
        grid=(n_batch, h_),
        in_specs=[pl.BlockSpec((ctx_len, HEAD_DIM), lambda b, h: (ctx_blk0 + b, h)),
                  pl.BlockSpec((ctx_len, HEAD_DIM), lambda b, h: (ctx_blk0 + b, h_ + h)),
                  pl.BlockSpec((ctx_len, HEAD_DIM), lambda b, h: (ctx_blk0 + b, 2 * h_ + h)),
                  pl.BlockSpec(memory_space=pl.ANY)],
        out_specs=pl.BlockSpec((ctx_len, HEAD_DIM), lambda b, h: (ctx_blk0 + b, h)),
        out_shape=jax.ShapeDtypeStruct(attn_out.shape, attn_out.dtype),
        input_output_aliases={3: 0},
        compiler_params=_params(("parallel", "parallel")),
        name="ctx_attention",
    )(qkv, qkv, qkv, attn_out)


def _dwconv_kernel(x_ref, w_ref, b_ref, o_ref, xpad, *, seq):
    half = CONV_WIDTH // 2
    lead = 2 * SUBLANES
    ch = x_ref.shape[1]
    xpad[pl.ds(0, lead), :] = jnp.zeros((lead, ch), F32)
    xpad[pl.ds(lead + seq, lead), :] = jnp.zeros((lead, ch), F32)
    xpad[pl.ds(lead, seq), :] = x_ref[...]
    w = w_ref[...]
    bias = b_ref[...]
    win_rows = CONV_TT + 2 * lead

    def body(c, carry):
        t0 = pl.multiple_of(c * CONV_TT, CONV_TT)
        win = xpad[pl.ds(t0, win_rows), :]
        acc = jnp.broadcast_to(bias, (CONV_TT, ch))
        for p in range(SUBLANES):
            shifted = win if p == 0 else pltpu.roll(win, win_rows - p, axis=0)
            for k in range(CONV_WIDTH):
                o = lead - half + k
                if o % SUBLANES == p:
                    acc = acc + shifted[o - p:o - p + CONV_TT, :] * w[k:k + 1, :]
        o_ref[pl.ds(t0, CONV_TT), :] = acc
        return carry

    lax.fori_loop(0, seq // CONV_TT, body, 0)


def _dwconv(x, w_dw, b_dw, n_batch, seq):
    d = x.shape[1]
    nc = d // CONV_CH
    return pl.pallas_call(
        functools.partial(_dwconv_kernel, seq=seq),
        grid=(n_batch, nc),
        in_specs=[pl.BlockSpec((seq, CONV_CH), lambda b, c: (b, c)),
                  pl.BlockSpec((CONV_WIDTH, CONV_CH), lambda b, c: (0, c)),
                  pl.BlockSpec((1, CONV_CH), lambda b, c: (0, c))],
        out_specs=pl.BlockSpec((seq, CONV_CH), lambda b, c: (b, c)),
        out_shape=jax.ShapeDtypeStruct((n_batch * seq, d), F32),
        scratch_shapes=[pltpu.VMEM((seq + 4 * SUBLANES, CONV_CH), F32)],
        compiler_params=_params(("parallel", "parallel")),
        name="dwconv",
    )(x, w_dw, b_dw.reshape(1, d))


def _proj_ln_kernel(*refs, pre_ln, has_bias, n_lat_tiles):
    it = iter(refs)
    a_ref = next(it)
    pre_g = next(it) if pre_ln else None
    pre_b = next(it) if pre_ln else None
    w_ref = next(it)
    bias_ref = next(it) if has_bias else None
    h_ref = next(it)
    hc_ref = next(it) if n_lat_tiles is not None else None
    gate_ref, lng_ref, lnb_ref, scf_ref, shf_ref, wr_ref = (next(it) for _ in range(6))
    hnew_ref, fin_ref, score_ref = (next(it) for _ in range(3))

    tm = a_ref.shape[0]
    half = tm // PROJ_SPLIT

    def project(p):
        a = a_ref[p * half:(p + 1) * half, :]
        if pre_ln:
            a = _silu(_layer_norm(a, pre_g[...], pre_b[...]))
        return jnp.dot(a.astype(BF16), w_ref[...], preferred_element_type=F32)

    def finish(p, m):
        rows = slice(p * half, (p + 1) * half)
        if has_bias:
            m = m + bias_ref[...]
        h_old = h_ref[rows, :]
        if n_lat_tiles is not None:
            h_old = jnp.where(pl.program_id(0) < n_lat_tiles, h_old, hc_ref[rows, :])
        hn = _layer_norm(DEEPNORM_ALPHA * h_old + gate_ref[0] * m, lng_ref[...], lnb_ref[...])
        hnew_ref[rows, :] = hn
        f = hn * (1.0 + scf_ref[0]) + shf_ref[0]
        _store_packed_rows(fin_ref.at[pl.ds(p * half * SUBLANES, half * SUBLANES)], f)
        score_ref[rows, :] = jax.nn.sigmoid(jnp.dot(f.astype(BF16), wr_ref[...], preferred_element_type=F32))

    m_next = project(0)
    for p in range(PROJ_SPLIT):
        m_cur = m_next
        if p + 1 < PROJ_SPLIT:
            m_next = project(p + 1)
        finish(p, m_cur)


def _proj_ln(a, w, bias, h, h_ctx, mods, layer, ln_g, ln_b, w_router, n_rows, seq, pre_ln=None):
    d = D_MODEL
    tm = PROJ_TILE
    row = lambda i: (i, 0)
    const = lambda i: (0, 0)
    n_lat_tiles = None if h_ctx is None else h.shape[0] // tm
    args, specs = [a], [pl.BlockSpec((tm, d), row)]
    if pre_ln is not None:
        args += [pre_ln[0].reshape(1, d), pre_ln[1].reshape(1, d)]
        specs += [pl.BlockSpec((1, d), const)] * 2
    args.append(w)
    specs.append(pl.BlockSpec((d, d), const))
    if bias is not None:
        args.append(bias.reshape(1, d))
        specs.append(pl.BlockSpec((1, d), const))
    if h_ctx is None:
        args.append(h)
        specs.append(pl.BlockSpec((tm, d), row))
    else:
        args += [h, h_ctx]
        specs += [pl.BlockSpec((tm, d), lambda i: (jnp.minimum(i, n_lat_tiles - 1), 0)),
                  pl.BlockSpec((tm, d), lambda i: (jnp.maximum(i - n_lat_tiles, 0), 0))]
    args += [mods, ln_g.reshape(1, d), ln_b.reshape(1, d), mods, mods, w_router]
    specs += [_mod_spec(layer, G_A, tm, seq), pl.BlockSpec((1, d), const), pl.BlockSpec((1, d), const),
              _mod_spec(layer, SC_F, tm, seq), _mod_spec(layer, SH_F, tm, seq),
              pl.BlockSpec((d, N_EXPERTS), const)]
    return pl.pallas_call(
        functools.partial(_proj_ln_kernel, pre_ln=pre_ln is not None, has_bias=bias is not None,
                          n_lat_tiles=n_lat_tiles),
        grid=(n_rows // tm,),
        in_specs=specs,
        out_specs=[pl.BlockSpec((tm, d), row), pl.BlockSpec((tm * SUBLANES, LANES), row),
                   pl.BlockSpec((tm, N_EXPERTS), row)],
        out_shape=[jax.ShapeDtypeStruct((n_rows, d), F32),
                   jax.ShapeDtypeStruct((n_rows * SUBLANES, LANES), jnp.uint32),
                   jax.ShapeDtypeStruct((n_rows, N_EXPERTS), F32)],
        compiler_params=_params(("parallel",)),
        name="proj_ln",
    )(*args)


def _router_kernel(s_ref, b_ref, ek_ref, rk_ref, gk_ref, cnt_ref, carry):
    tm, e = s_ref.shape

    @pl.when(pl.program_id(0) == 0)
    def _():
        carry[...] = jnp.zeros_like(carry)

    s = s_ref[...]
    lane = lax.broadcasted_iota(jnp.int32, (tm, e), 1).astype(F32)
    slot = lax.broadcasted_iota(jnp.int32, (tm, LANES), 1)
    work = s + b_ref[...]
    sel = jnp.zeros((tm, e), F32)
    e_acc = jnp.zeros((tm, LANES), F32)
    g_acc = jnp.zeros((tm, LANES), F32)
    g_sum = jnp.zeros((tm, 1), F32)
    picks = []
    for k in range(TOP_K):
        m = jnp.max(work, axis=-1, keepdims=True)
        idx = jnp.min(jnp.where(work == m, lane, float(e)), axis=-1, keepdims=True)
        onehot = lane == idx
        gk = jnp.sum(jnp.where(onehot, s, 0.0), axis=-1, keepdims=True)
        sel = jnp.where(onehot, 1.0, sel)
        work = jnp.where(onehot, -jnp.inf, work)
        e_acc = jnp.where(slot == k, idx, e_acc)
        g_acc = jnp.where(slot == k, gk, g_acc)
        g_sum = g_sum + gk
        picks.append(idx)

    r_i = lax.broadcasted_iota(jnp.int32, (tm, tm), 0)
    c_i = lax.broadcasted_iota(jnp.int32, (tm, tm), 1)
    tri = jnp.where(c_i < r_i, 1.0, 0.0).astype(BF16)
    rank = jnp.dot(tri, sel.astype(BF16), preferred_element_type=F32) + carry[...]
    r_acc = jnp.zeros((tm, LANES), F32)
    for k in range(TOP_K):
        rk = jnp.sum(jnp.where(lane == picks[k], rank, 0.0), axis=-1, keepdims=True)
        r_acc = jnp.where(slot == k, rk, r_acc)

    carry[...] = carry[...] + jnp.sum(sel, axis=0, keepdims=True)
    ek_ref[...] = e_acc[:, :TOP_K].astype(jnp.int32)
    rk_ref[...] = r_acc[:, :TOP_K].astype(jnp.int32)
    gk_ref[...] = (g_acc / g_sum * ROUTED_SCALE)[:, :TOP_K]
    cnt_ref[...] = carry[...].astype(jnp.int32)


def _router(scores, router_bias, n_rows):
    e = N_EXPERTS
    tm = ROUTER_TILE
    row = lambda i: (i, 0)
    return pl.pallas_call(
        _router_kernel,
        grid=(n_rows // tm,),
        in_specs=[pl.BlockSpec((tm, e), row), pl.BlockSpec((1, e), lambda i: (0, 0))],
        out_specs=[pl.BlockSpec((tm, TOP_K), row), pl.BlockSpec((tm, TOP_K), row), pl.BlockSpec((tm, TOP_K), row),
                   pl.BlockSpec((1, e), lambda i: (0, 0))],
        out_shape=[jax.ShapeDtypeStruct((n_rows, TOP_K), jnp.int32), jax.ShapeDtypeStruct((n_rows, TOP_K), jnp.int32),
                   jax.ShapeDtypeStruct((n_rows, TOP_K), F32), jax.ShapeDtypeStruct((1, e), jnp.int32)],
        scratch_shapes=[pltpu.VMEM((1, e), F32)],
        compiler_params=_params(("arbitrary",)),
        name="router",
    )(scores, router_bias.reshape(1, e).astype(F32))


def _layout(e_k, r_k, counts, tm_e):
    e = N_EXPERTS
    n_assign = e_k.size
    counts = counts.reshape(e)
    padded = (counts + tm_e - 1) // tm_e * tm_e
    pends = jnp.cumsum(padded)
    pstarts = pends - padded
    experts = jnp.arange(e, dtype=jnp.int32)
    start_of = jnp.sum(jnp.where(e_k[..., None] == experts, pstarts, 0), axis=-1)
    dest = (start_of + r_k).reshape(n_assign).astype(jnp.int32)
    n_blocks = (n_assign + e * (tm_e - 1) + tm_e - 1) // tm_e
    blk_row0 = jnp.arange(n_blocks, dtype=jnp.int32) * tm_e
    blk_e = jnp.minimum(jnp.sum(pends[None, :] <= blk_row0[:, None], axis=-1), e - 1).astype(jnp.int32)
    n_valid = (pends[-1] // tm_e).astype(jnp.int32).reshape(1)
    pad_info = jnp.concatenate([pstarts + counts, padded - counts]).astype(jnp.int32)
    later = (experts[None, :] > experts[:, None]) & (counts[None, :] > 0)
    next_e = jnp.min(jnp.where(later, experts[None, :], e), axis=1)
    next_e = jnp.where(next_e == e, -1, next_e)
    order_e = jnp.sum((experts[None, :] < experts[:, None]) & (counts[None, :] > 0), axis=1)
    chain = jnp.concatenate([next_e, order_e]).astype(jnp.int32)
    return dest, blk_e, n_valid, pad_info, chain, n_blocks


def _dispatch_kernel(pad_ref, dest_ref, f_ref, xs_ref, zbuf, sem, zsem):
    tm = f_ref.shape[0] // SUBLANES
    pad_bits = EXPERT_TILE.bit_length() - 1

    def zero_copies(e_idx):
        start = pad_ref[e_idx]
        pad = pad_ref[N_EXPERTS + e_idx]
        return [(((pad >> b) & 1) == 1, pltpu.make_async_copy(
            _row_tile(zbuf, 0, 1 << b), _row_tile(xs_ref, start + (pad & ((1 << b) - 1)), 1 << b), zsem))
            for b in range(pad_bits)]

    @pl.when(pl.program_id(0) == 0)
    def _():
        zbuf[...] = jnp.zeros_like(zbuf)

        def start_e(e_idx, carry):
            for cond, cp in zero_copies(e_idx):
                pl.when(cond)(cp.start)
            return carry

        def wait_e(e_idx, carry):
            for cond, cp in zero_copies(e_idx):
                pl.when(cond)(cp.wait)
            return carry

        lax.fori_loop(0, N_EXPERTS, start_e, 0)
        lax.fori_loop(0, N_EXPERTS, wait_e, 0)

    def tok(t, carry):
        for k in range(TOP_K):
            d = dest_ref[t * TOP_K + k]
            pltpu.make_async_copy(_row_tile(f_ref, t), _row_tile(xs_ref, d), sem).start(priority=k % 2)
        return carry

    lax.fori_loop(0, tm, tok, 0)
    for k in range(TOP_K):
        pltpu.make_async_copy(f_ref, _row_tile(xs_ref, 0, tm), sem).wait()


def _dispatch(f_in, dest, pad_info, n_rows, n_blocks):
    tm = DISPATCH_TILE
    grid_spec = pltpu.PrefetchScalarGridSpec(
        num_scalar_prefetch=1,
        grid=(n_rows // tm,),
        in_specs=[pl.BlockSpec((tm * TOP_K,), lambda i, p: (i,), memory_space=pltpu.SMEM),
                  pl.BlockSpec((tm * SUBLANES, LANES), lambda i, p: (i, 0))],
        out_specs=pl.BlockSpec(memory_space=pl.ANY),
        scratch_shapes=[pltpu.VMEM((EXPERT_TILE // 2 * SUBLANES, LANES), f_in.dtype), pltpu.SemaphoreType.DMA,
                        pltpu.SemaphoreType.DMA],
    )
    return pl.pallas_call(
        _dispatch_kernel,
        grid_spec=grid_spec,
        out_shape=jax.ShapeDtypeStruct((n_blocks * EXPERT_TILE * SUBLANES, LANES), f_in.dtype),
        compiler_params=_params(("arbitrary",)),
        name="dispatch",
    )(pad_info, dest, f_in)


def _swiglu_hidden(x, wg_ref, wu_ref):
    gv = jnp.dot(x, wg_ref[...], preferred_element_type=F32)
    uv = jnp.dot(x, wu_ref[...], preferred_element_type=F32)
    return (_silu(gv) * uv).astype(BF16)


def _expert_kernel(be_ref, nv_ref, chain_ref, x_ref, wg_hbm, wu_hbm, wd_hbm, y_ref,
                   wg32, wu32, wd32, wgb, wub, wdb, sems, *, layer):
    i = pl.program_id(0)
    valid = i < nv_ref[0]
    e_cur = be_ref[i]
    first = jnp.logical_or(i == 0, e_cur != be_ref[jnp.maximum(i - 1, 0)])
    d, de = wgb.shape
    n_chunks = 8

    def weight_copies(e_idx, slot):
        return [pltpu.make_async_copy(w_hbm.at[layer, e_idx], buf.at[slot], sems.at[slot, j])
                for j, (w_hbm, buf) in enumerate(((wg_hbm, wg32), (wu_hbm, wu32), (wd_hbm, wd32)))]

    @pl.when(jnp.logical_and(valid, first))
    def _():
        slot = chain_ref[N_EXPERTS + e_cur] & 1
        e_next = chain_ref[e_cur]

        @pl.when(i == 0)
        def _():
            for cp in weight_copies(e_cur, slot):
                cp.start()

        for cp in weight_copies(e_cur, slot):
            cp.wait()

        @pl.when(e_next >= 0)
        def _():
            for cp in weight_copies(e_next, 1 - slot):
                cp.start()

        def cast_chunk(c, carry):
            r0 = pl.multiple_of(c * (d // n_chunks), d // n_chunks)
            wgb[pl.ds(r0, d // n_chunks), :] = wg32[slot, pl.ds(r0, d // n_chunks), :].astype(BF16)
            wub[pl.ds(r0, d // n_chunks), :] = wu32[slot, pl.ds(r0, d // n_chunks), :].astype(BF16)
            r1 = pl.multiple_of(c * (de // n_chunks), de // n_chunks)
            wdb[pl.ds(r1, de // n_chunks), :] = wd32[slot, pl.ds(r1, de // n_chunks), :].astype(BF16)
            return carry

        lax.fori_loop(0, n_chunks, cast_chunk, 0)

    @pl.when(valid)
    def _():
        act = _swiglu_hidden(_load_packed_rows_bf16(x_ref), wgb, wub)
        _store_packed_rows(y_ref, jnp.dot(act, wdb[...], preferred_element_type=F32))


def _expert_ffn(xs, blk_e, n_valid, chain, w_gate, w_up, w_down, layer):
    n_rows = xs.shape[0] // SUBLANES
    d, de = w_gate.shape[2], w_gate.shape[3]
    tm = EXPERT_TILE
    n_blocks = n_rows // tm

    def xmap(i, be, nv, ch):
        return (jnp.minimum(i, nv[0] - 1), 0)

    grid_spec = pltpu.PrefetchScalarGridSpec(
        num_scalar_prefetch=3,
        grid=(n_blocks,),
        in_specs=[pl.BlockSpec((tm * SUBLANES, LANES), xmap),
                  pl.BlockSpec(memory_space=pl.ANY),
                  pl.BlockSpec(memory_space=pl.ANY),
                  pl.BlockSpec(memory_space=pl.ANY)],
        out_specs=pl.BlockSpec((tm * SUBLANES, LANES), xmap),
        scratch_shapes=[pltpu.VMEM((2, d, de), F32), pltpu.VMEM((2, d, de), F32), pltpu.VMEM((2, de, d), F32),
                        pltpu.VMEM((d, de), BF16), pltpu.VMEM((d, de), BF16), pltpu.VMEM((de, d), BF16),
                        pltpu.SemaphoreType.DMA((2, 3))],
    )
    return pl.pallas_call(
        functools.partial(_expert_kernel, layer=layer),
        grid_spec=grid_spec,
        out_shape=jax.ShapeDtypeStruct((n_rows * SUBLANES, LANES), jnp.uint32),
        compiler_params=_params(("arbitrary",)),
        name="expert_ffn",
    )(blk_e, n_valid, chain, xs, w_gate, w_up, w_down)


def _ffn_out_kernel(dcur_ref, dnext_ref, f_ref, gk_ref, wg_ref, wu_ref, wd_ref, h_ref, gate_ref, lng_ref, lnb_ref,
                    y_ref, o_ref, ybuf, sems):
    i = pl.program_id(0)
    n_steps = pl.num_programs(0)
    tm = f_ref.shape[0] // SUBLANES
    slot = lax.rem(i, 2)

    def gather_token(dref, s, t):
        for k in range(TOP_K):
            row = dref[t * TOP_K + k]
            pltpu.make_async_copy(_row_tile(y_ref, row), _row_tile(ybuf.at[s, k], t),
                                  sems.at[s]).start(priority=k % 2)

    def wait_gathers(s):
        for k in range(TOP_K):
            pltpu.make_async_copy(_row_tile(y_ref, 0, tm), ybuf.at[s, k], sems.at[s]).wait()

    @pl.when(i == 0)
    def _():
        def tok(t, carry):
            gather_token(dcur_ref, slot, t)
            return carry

        lax.fori_loop(0, tm, tok, 0)

    n_pieces = 2 + SUBLANES
    piece = [(p * tm) // n_pieces for p in range(n_pieces + 1)]

    def issue_piece(p):
        for t in range(piece[p], piece[p + 1]):
            gather_token(dnext_ref, 1 - slot, t)

    issue_piece(0)
    act = _swiglu_hidden(_load_packed_rows_bf16(f_ref), wg_ref, wu_ref)
    issue_piece(1)
    f = jnp.dot(act, wd_ref[...], preferred_element_type=F32)

    gk = gk_ref[...]
    wait_gathers(slot)
    gcols = [jnp.broadcast_to(gk[:, k:k + 1], (tm, LANES)) for k in range(TOP_K)]
    parts = []
    for s_ in range(SUBLANES):
        issue_piece(2 + s_)
        r_lo = jnp.zeros((tm, LANES), F32)
        r_hi = jnp.zeros((tm, LANES), F32)
        for k in range(TOP_K):
            y_lo, y_hi = _load_packed_group(ybuf.at[slot, k], s_)
            r_lo = r_lo + gcols[k] * y_lo
            r_hi = r_hi + gcols[k] * y_hi
        parts += [r_lo, r_hi]
    f = f + jnp.concatenate(parts, axis=1)
    o_ref[...] = _layer_norm(DEEPNORM_ALPHA * h_ref[...] + gate_ref[0] * f, lng_ref[...], lnb_ref[...])

    @pl.when(i == n_steps - 1)
    def _():
        wait_gathers(1 - slot)


def _ffn_out(f_in, y, dest, gates, ws_gate, ws_up, ws_down, h, mods, layer, ln_g, ln_b, n_rows, seq):
    d = D_MODEL
    ds_ = ws_gate.shape[1]
    tm = COMBINE_TILE
    n_steps = n_rows // tm
    row = lambda i: (i, 0)
    const = lambda i: (0, 0)
    return pl.pallas_call(
        _ffn_out_kernel,
        grid=(n_steps,),
        in_specs=[pl.BlockSpec((tm * TOP_K,), lambda i: (i,), memory_space=pltpu.SMEM),
                  pl.BlockSpec((tm * TOP_K,), lambda i: (jnp.minimum(i + 1, n_steps - 1),), memory_space=pltpu.SMEM),
                  pl.BlockSpec((tm * SUBLANES, LANES), row), pl.BlockSpec((tm, TOP_K), row),
                  pl.BlockSpec((d, ds_), const), pl.BlockSpec((d, ds_), const), pl.BlockSpec((ds_, d), const),
                  pl.BlockSpec((tm, d), row), _mod_spec(layer, G_F, tm, seq),
                  pl.BlockSpec((1, d), const), pl.BlockSpec((1, d), const),
                  pl.BlockSpec(memory_space=pl.ANY)],
        out_specs=pl.BlockSpec((tm, d), row),
        out_shape=jax.ShapeDtypeStruct((n_rows, d), F32),
        scratch_shapes=[pltpu.VMEM((2, TOP_K, tm * SUBLANES, LANES), jnp.uint32), pltpu.SemaphoreType.DMA((2,))],
        compiler_params=_params(("arbitrary",)),
        name="ffn_out",
    )(dest, dest, f_in, gates, ws_gate, ws_up, ws_down, h, mods, ln_g.reshape(1, d), ln_b.reshape(1, d), y)


def _moe_and_norm(h, f_in, scores, mods, router_bias, we_gate, we_up, we_down, ws_gate, ws_up, ws_down,
                  ln_g, ln_b, n_rows, seq, layer):
    e_k, r_k, gates, counts = _router(scores, router_bias, n_rows)
    dest, blk_e, n_valid, pad_info, chain, n_blocks = _layout(e_k, r_k, counts, EXPERT_TILE)
    xs = _dispatch(f_in, dest, pad_info, n_rows, n_blocks)
    y = _expert_ffn(xs, blk_e, n_valid, chain, we_gate, we_up, we_down, layer)
    return _ffn_out(f_in, y, dest, gates, ws_gate.astype(BF16), ws_up.astype(BF16), ws_down.astype(BF16),
                    h, mods, layer, ln_g, ln_b, n_rows, seq)


def kernel(x, c, ctx, c_ctx, mod_w, mod_b, ln_mix_g, ln_mix_b, ln_ffn_g, ln_ffn_b, na_w_qkv, na_rpb, na_w_o,
           cv_w_pw1, cv_b_pw1, cv_w_dw, cv_b_dw, cv_ln_g, cv_ln_b, cv_w_pw2, cv_b_pw2, moe_w_router,
           moe_router_bias, moe_w_gate, moe_w_up, moe_w_down, sh_w_gate, sh_w_up, sh_w_down):
    n_batch, seq, d = x.shape
    ctx_len = ctx.shape[1]
    n_lat = n_batch * seq
    n_tok = n_lat + n_batch * ctx_len

    cvec = jnp.zeros((MOD_ROWS, d), F32).at[:n_batch].set(c).at[n_batch].set(c_ctx)
    mods = _mod_matmul(cvec, mod_w, mod_b).reshape(DEPTH * MOD_ROWS * MOD_PIECES, 1, d)
    x_lat = x.reshape(n_lat, d)
    x_ctx = ctx.reshape(n_batch * ctx_len, d)

    qkv = _mod_matmul_qkv(x_lat, x_ctx, mods, 0, na_w_qkv[0].astype(BF16), seq)
    attn = _na_attention(qkv, _na_bias_table(na_rpb[0]), n_batch, seq, ctx_len, n_tok)
    attn = _ctx_attention(qkv, attn, n_batch, seq, ctx_len)
    h, f_in, scores = _proj_ln(attn, na_w_o[0].astype(BF16), None, x_lat, x_ctx, mods, 0, ln_mix_g[0], ln_mix_b[0],
                               moe_w_router[0].astype(BF16), n_tok, seq)
    h = _moe_and_norm(h, f_in, scores, mods, moe_router_bias[0], moe_w_gate, moe_w_up, moe_w_down,
                      sh_w_gate[0], sh_w_up[0], sh_w_down[0], ln_ffn_g[0], ln_ffn_b[0], n_tok, seq, 0)

    glu = _mod_matmul_glu(h, mods, 1, cv_w_pw1[0].astype(BF16), cv_b_pw1[0], n_lat, seq)
    conv = _dwconv(glu, cv_w_dw[0], cv_b_dw[0], n_batch, seq)
    h, f_in, scores = _proj_ln(conv, cv_w_pw2[0].astype(BF16), cv_b_pw2[0], h, None, mods, 1, ln_mix_g[1],
                               ln_mix_b[1], moe_w_router[1].astype(BF16), n_lat, seq,
                               pre_ln=(cv_ln_g[0], cv_ln_b[0]))
    h = _moe_and_norm(h, f_in, scores, mods, moe_router_bias[1], moe_w_gate, moe_w_up, moe_w_down,
                      sh_w_gate[1], sh_w_up[1], sh_w_down[1], ln_ffn_g[1], ln_ffn_b[1], n_lat, seq, 1)
    return h.reshape(n_batch, seq, d)
```

```python
import functools

import jax
import jax.numpy as jnp
from jax import lax
from jax.experimental import pallas as pl
from jax.experimental.pallas import tpu as pltpu

D_MODEL = 2048
GRID_W = 64
N_HEADS = 16
HEAD_DIM = D_MODEL // N_HEADS
WIN_ROWS = 8
WIN_COLS = 16
CONV_WIDTH = 31
N_EXPERTS = 64
TOP_K = 8
ROUTED_SCALE = 2.5
LN_EPS = 1e-6
SUBLANES = 8
NEG_INF = -1e30
DEPTH = 2
DEEPNORM_ALPHA = (2 * DEPTH) ** 0.25

F32 = jnp.float32
BF16 = jnp.bfloat16

VMEM_LIMIT_BYTES = 56 * 1024 * 1024
DISPATCH_TILE = 512
PROJ_TILE = 512
PROJ_SPLIT = 2
ROUTER_TILE = 512
MM_ROW_TILE = 512
MOD_COL_TILE = 1024
QKV_COL_TILE = 2048
GLU_COL_TILE = 1024
EXPERT_TILE = 512
COMBINE_TILE = 128
ROWS_PER_STEP = 16
NA_BLOCK_ROWS = 2
NA_KEY_ROWS = WIN_ROWS + NA_BLOCK_ROWS
CONV_CH = 256
CONV_TT = 128


def _params(sem):
    return pltpu.CompilerParams(dimension_semantics=sem, vmem_limit_bytes=VMEM_LIMIT_BYTES)


def _layer_norm(z, g, b):
    mu = jnp.mean(z, axis=-1, keepdims=True)
    zc = z - mu
    var = jnp.mean(zc * zc, axis=-1, keepdims=True)
    return zc * lax.rsqrt(var + LN_EPS) * g + b


def _silu(x):
    return x * jax.nn.sigmoid(x)


LANES = 128
ROW_WORDS = D_MODEL // 2
assert ROW_WORDS == SUBLANES * LANES


def _store_packed_rows(ref, x):
    rows = x.shape[0]
    for s_ in range(SUBLANES):
        base = 2 * LANES * s_
        lo = pltpu.bitcast(x[:, base:base + LANES].astype(BF16).astype(F32), jnp.uint32)
        hi = pltpu.bitcast(x[:, base + LANES:base + 2 * LANES].astype(BF16).astype(F32), jnp.uint32)
        ref[pl.ds(s_, rows, stride=SUBLANES), :] = (lo >> 16) | (hi & jnp.uint32(0xFFFF0000))


def _load_packed_group(ref, s_):
    w = ref[pl.ds(s_, ref.shape[0] // SUBLANES, stride=SUBLANES), :]
    return pltpu.bitcast(w << 16, F32), pltpu.bitcast(w & jnp.uint32(0xFFFF0000), F32)


def _row_tile(ref, t, n=1):
    return ref.at[pl.ds(pl.multiple_of(t * SUBLANES, SUBLANES), n * SUBLANES)]


def _load_packed_rows_bf16(ref):
    parts = []
    for s_ in range(SUBLANES):
        lo, hi = _load_packed_group(ref, s_)
        parts += [lo.astype(BF16), hi.astype(BF16)]
    return jnp.concatenate(parts, axis=1)


def _mod_kernel(c_ref, w_ref, b_ref, o_ref):
    a = _silu(c_ref[...]).astype(BF16)
    o_ref[0] = jnp.dot(a, w_ref[0].astype(BF16), preferred_element_type=F32) + b_ref[0]


def _mod_matmul(cvec, w, b):
    m, k = cvec.shape
    layers, _, n = w.shape
    tn = MOD_COL_TILE
    return pl.pallas_call(
        _mod_kernel,
        grid=(layers, n // tn),
        in_specs=[pl.BlockSpec((m, k), lambda l, j: (0, 0)),
                  pl.BlockSpec((1, k, tn), lambda l, j: (l, 0, j)),
                  pl.BlockSpec((1, 1, tn), lambda l, j: (l, 0, j))],
        out_specs=pl.BlockSpec((1, m, tn), lambda l, j: (l, 0, j)),
        out_shape=jax.ShapeDtypeStruct((layers, m, n), F32),
        compiler_params=_params(("parallel", "arbitrary")),
        name="mod_matmul",
    )(cvec, w, b.reshape(layers, 1, n))


def _modmm_kernel(x_ref, c_ref, sc_ref, sh_ref, w_ref, o_ref, a_scr, *, n_lat_tiles):
    @pl.when(pl.program_id(1) == 0)
    def _():
        h = jnp.where(pl.program_id(0) < n_lat_tiles, x_ref[...], c_ref[...])
        a_scr[...] = (h * (1.0 + sc_ref[0]) + sh_ref[0]).astype(BF16)

    o_ref[...] = jnp.dot(a_scr[...], w_ref[...], preferred_element_type=F32).astype(o_ref.dtype)


def _modglu_kernel(h_ref, sc_ref, sh_ref, wa_ref, wg_ref, ba_ref, bg_ref, o_ref, a_scr):
    @pl.when(pl.program_id(1) == 0)
    def _():
        a_scr[...] = (h_ref[...] * (1.0 + sc_ref[0]) + sh_ref[0]).astype(BF16)

    a = a_scr[...]
    va = jnp.dot(a, wa_ref[...], preferred_element_type=F32) + ba_ref[...]
    vg = jnp.dot(a, wg_ref[...], preferred_element_type=F32) + bg_ref[...]
    o_ref[...] = va * jax.nn.sigmoid(vg)


MOD_ROWS = 8
MOD_PIECES = 6
SH_A, SC_A, G_A, SH_F, SC_F, G_F = range(MOD_PIECES)


def _mod_spec(layer, piece, tile_rows, seq):
    def index(i, *_):
        return ((layer * MOD_ROWS + (i * tile_rows) // seq) * MOD_PIECES + piece, 0, 0)

    return pl.BlockSpec((1, 1, D_MODEL), index)


def _mod_matmul_qkv(x_lat, x_ctx, mods, layer, w, seq):
    n_lat, k = x_lat.shape
    n_rows = n_lat + x_ctx.shape[0]
    n = w.shape[1]
    tm, tn = MM_ROW_TILE, QKV_COL_TILE
    n_lat_tiles = n_lat // tm
    return pl.pallas_call(
        functools.partial(_modmm_kernel, n_lat_tiles=n_lat_tiles),
        grid=(n_rows // tm, n // tn),
        in_specs=[pl.BlockSpec((tm, k), lambda i, j: (jnp.minimum(i, n_lat_tiles - 1), 0)),
                  pl.BlockSpec((tm, k), lambda i, j: (jnp.maximum(i - n_lat_tiles, 0), 0)),
                  _mod_spec(layer, SC_A, tm, seq), _mod_spec(layer, SH_A, tm, seq),
                  pl.BlockSpec((k, tn), lambda i, j: (0, j))],
        out_specs=pl.BlockSpec((tm, tn), lambda i, j: (i, j)),
        out_shape=jax.ShapeDtypeStruct((n_rows, n), BF16),
        scratch_shapes=[pltpu.VMEM((tm, k), BF16)],
        compiler_params=_params(("parallel", "arbitrary")),
        name="mod_qkv",
    )(x_lat, x_ctx, mods, mods, w)


def _mod_matmul_glu(h, mods, layer, w, b, n_rows, seq):
    k = h.shape[1]
    n = w.shape[1] // 2
    tm, tn = MM_ROW_TILE, GLU_COL_TILE
    nj = n // tn
    b2 = b.reshape(1, 2 * n)
    return pl.pallas_call(
        _modglu_kernel,
        grid=(n_rows // tm, nj),
        in_specs=[pl.BlockSpec((tm, k), lambda i, j: (i, 0)),
                  _mod_spec(layer, SC_A, tm, seq), _mod_spec(layer, SH_A, tm, seq),
                  pl.BlockSpec((k, tn), lambda i, j: (0, j)),
                  pl.BlockSpec((k, tn), lambda i, j: (0, j + nj)),
                  pl.BlockSpec((1, tn), lambda i, j: (0, j)),
                  pl.BlockSpec((1, tn), lambda i, j: (0, j + nj))],
        out_specs=pl.BlockSpec((tm, tn), lambda i, j: (i, j)),
        out_shape=jax.ShapeDtypeStruct((n_rows, n), F32),
        scratch_shapes=[pltpu.VMEM((tm, k), BF16)],
        compiler_params=_params(("parallel", "arbitrary")),
        name="mod_pw1_glu",
    )(h, mods, mods, w, w, b2, b2)


def _na_block_window(r0, rows):
    return jnp.clip(r0 - WIN_ROWS // 2, 0, rows - NA_KEY_ROWS)


def _na_block_kinds(rows):
    def geometry(r0):
        k0 = min(max(r0 - WIN_ROWS // 2, 0), rows - NA_KEY_ROWS)
        out = []
        for r in range(r0, r0 + NA_BLOCK_ROWS):
            rs = min(max(r - WIN_ROWS // 2, 0), rows - WIN_ROWS)
            assert 0 <= rs - k0 <= NA_KEY_ROWS - WIN_ROWS
            out.append((rs - k0, r - rs))
        return tuple(out)

    kinds, kind_of = [], {}
    for r0 in range(0, rows, NA_BLOCK_ROWS):
        geo = geometry(r0)
        if geo not in kinds:
            kinds.append(geo)
        kind_of[r0] = kinds.index(geo)
    return kinds, kind_of


def _na_kernel(q_ref, k_ref, v_ref, kc_ref, vc_ref, bias_ref, o_ref, *, rows):
    scale = HEAD_DIM ** -0.5
    contract_last = (((1,), (1,)), ((), ()))
    row0 = pl.program_id(2) * ROWS_PER_STEP
    n_keys = NA_KEY_ROWS * GRID_W
    bq = NA_BLOCK_ROWS * GRID_W
    n_blk = ROWS_PER_STEP // NA_BLOCK_ROWS
    _, kind_of = _na_block_kinds(rows)
    kc = kc_ref[...]
    vc = vc_ref[...]

    def block_kind(r0):
        common = max(set(kind_of.values()), key=list(kind_of.values()).count)
        kind = jnp.int32(common)
        for r_static, kd in kind_of.items():
            if kd != common:
                kind = jnp.where(r0 == r_static, kd, kind)
        return kind

    def key_start(b):
        return pl.multiple_of(_na_block_window(row0 + b * NA_BLOCK_ROWS, rows) * GRID_W, GRID_W)

    def scores(b):
        q = q_ref[b * bq:(b + 1) * bq, :]
        kw = k_ref[pl.ds(key_start(b), n_keys), :]
        bias = bias_ref[0, block_kind(row0 + b * NA_BLOCK_ROWS)]
        s_loc = lax.dot_general(q, kw, contract_last, preferred_element_type=F32) * scale + bias
        s_ctx = lax.dot_general(q, kc, contract_last, preferred_element_type=F32) * scale
        return s_loc, s_ctx

    def finish(b, s_loc, s_ctx):
        vw = v_ref[pl.ds(key_start(b), n_keys), :]
        m = jnp.maximum(jnp.max(s_loc, axis=-1, keepdims=True), jnp.max(s_ctx, axis=-1, keepdims=True))
        p_loc = jnp.exp(s_loc - m)
        p_ctx = jnp.exp(s_ctx - m)
        denom = jnp.sum(p_loc, axis=-1, keepdims=True) + jnp.sum(p_ctx, axis=-1, keepdims=True)
        o = (jnp.dot(p_loc.astype(BF16), vw, preferred_element_type=F32)
             + jnp.dot(p_ctx.astype(BF16), vc, preferred_element_type=F32))
        o_ref[b * bq:(b + 1) * bq, :] = (o / denom).astype(o_ref.dtype)

    s_next = scores(0)
    for b in range(n_blk):
        s_cur = s_next
        if b + 1 < n_blk:
            s_next = scores(b + 1)
        finish(b, *s_cur)


def _na_block_bias(row_table, rows):
    n_heads = row_table.shape[0]
    kinds, _ = _na_block_kinds(rows)
    per_kind = []
    for geo in kinds:
        per_row = []
        for rel, off in geo:
            before = jnp.full((n_heads, GRID_W, rel * GRID_W), NEG_INF, F32)
            after = jnp.full((n_heads, GRID_W, (NA_KEY_ROWS - WIN_ROWS - rel) * GRID_W), NEG_INF, F32)
            per_row.append(jnp.concatenate([before, row_table[:, off], after], axis=-1))
        per_kind.append(jnp.concatenate(per_row, axis=1))
    return jnp.stack(per_kind, axis=1)


def _na_bias_table(rpb):
    col = jnp.arange(GRID_W)
    cstart = jnp.clip(col - WIN_COLS // 2, 0, GRID_W - WIN_COLS)
    col_ok = (col[None, :] >= cstart[:, None]) & (col[None, :] < cstart[:, None] + WIN_COLS)
    cidx = jnp.clip(col[None, :] - col[:, None] + WIN_COLS - 1, 0, 2 * WIN_COLS - 2)
    onehot = (cidx[:, :, None] == jnp.arange(2 * WIN_COLS - 1)).astype(F32)
    t = jnp.einsum('hrc,qkc->hrqk', rpb.astype(F32), onehot, precision=lax.Precision.HIGHEST)
    t = jnp.where(col_ok[None, None], t, NEG_INF)
    t = jnp.stack([t[:, WIN_ROWS - 1 - off:2 * WIN_ROWS - 1 - off] for off in range(WIN_ROWS)], axis=1)
    t = t.transpose(0, 1, 3, 2, 4)
    return t.reshape(rpb.shape[0], WIN_ROWS, GRID_W, WIN_ROWS * GRID_W)


def _na_attention(qkv, bias_table, n_batch, seq, ctx_len, n_rows_out):
    rows = seq // GRID_W
    groups = rows // ROWS_PER_STEP
    tq = ROWS_PER_STEP * GRID_W
    ctx_blk0 = (n_batch * seq) // ctx_len
    h_ = N_HEADS
    block_bias = _na_block_bias(bias_table, rows)
    n_kinds = block_bias.shape[1]
    return pl.pallas_call(
        functools.partial(_na_kernel, rows=rows),
        grid=(n_batch, h_, groups),
        in_specs=[pl.BlockSpec((tq, HEAD_DIM), lambda b, h, g: (b * groups + g, h)),
                  pl.BlockSpec((seq, HEAD_DIM), lambda b, h, g: (b, h_ + h)),
                  pl.BlockSpec((seq, HEAD_DIM), lambda b, h, g: (b, 2 * h_ + h)),
                  pl.BlockSpec((ctx_len, HEAD_DIM), lambda b, h, g: (ctx_blk0 + b, h_ + h)),
                  pl.BlockSpec((ctx_len, HEAD_DIM), lambda b, h, g: (ctx_blk0 + b, 2 * h_ + h)),
                  pl.BlockSpec((1, n_kinds, NA_BLOCK_ROWS * GRID_W, NA_KEY_ROWS * GRID_W),
                               lambda b, h, g: (h, 0, 0, 0))],
        out_specs=pl.BlockSpec((tq, HEAD_DIM), lambda b, h, g: (b * groups + g, h)),
        out_shape=jax.ShapeDtypeStruct((n_rows_out, D_MODEL), BF16),
        compiler_params=_params(("parallel", "parallel", "arbitrary")),
        name="na_attention",
    )(qkv, qkv, qkv, qkv, qkv, block_bias)


def _ctx_attn_kernel(q_ref, k_ref, v_ref, prev_ref, o_ref):
    del prev_ref
    scale = HEAD_DIM ** -0.5
    s = lax.dot_general(q_ref[...], k_ref[...], (((1,), (1,)), ((), ())), preferred_element_type=F32) * scale
    m = jnp.max(s, axis=-1, keepdims=True)
    p = jnp.exp(s - m)
    denom = jnp.sum(p, axis=-1, keepdims=True)
    o = jnp.dot(p.astype(BF16), v_ref[...], preferred_element_type=F32)
    o_ref[...] = (o / denom).astype(o_ref.dtype)


def _ctx_attention(qkv, attn_out, n_batch, seq, ctx_len):
    ctx_blk0 = (n_batch * seq) // ctx_len
    h_ = N_HEADS
    return pl.pallas_call(
        _ctx_attn_kernel,
        grid=(n_batch, h_),
        in_specs=[pl.BlockSpec((ctx_len, HEAD_DIM), lambda b, h: (ctx_blk0 + b, h)),
                  pl.BlockSpec((ctx_len, HEAD_DIM), lambda b, h: (ctx_blk0 + b, h_ + h)),
                  pl.BlockSpec((ctx_len, HEAD_DIM), lambda b, h: (ctx_blk0 + b, 2 * h_ + h)),
                  pl.BlockSpec(memory_space=pl.ANY)],
        out_specs=pl.BlockSpec((ctx_len, HEAD_DIM), lambda b, h: (ctx_blk0 + b, h)),
        out_shape=jax.ShapeDtypeStruct(attn_out.shape, attn_out.dtype),
        input_output_aliases={3: 0},
        compiler_params=_params(("parallel", "parallel")),
        name="ctx_attention",
    )(qkv, qkv, qkv, attn_out)


def _dwconv_kernel(x_ref, w_ref, b_ref, o_ref, xpad, *, seq):
    half = CONV_WIDTH // 2
    lead = 2 * SUBLANES
    ch = x_ref.shape[1]
    xpad[pl.ds(0, lead), :] = jnp.zeros((lead, ch), F32)
    xpad[pl.ds(lead + seq, lead), :] = jnp.zeros((lead, ch), F32)
    xpad[pl.ds(lead, seq), :] = x_ref[...]
    w = w_ref[...]
    bias = b_ref[...]
    win_rows = CONV_TT + 2 * lead

    def body(c, carry):
        t0 = pl.multiple_of(c * CONV_TT, CONV_TT)
        win = xpad[pl.ds(t0, win_rows), :]
        acc = jnp.broadcast_to(bias, (CONV_TT, ch))
        for p in range(SUBLANES):
            shifted = win if p == 0 else pltpu.roll(win, win_rows - p, axis=0)
            for k in range(CONV_WIDTH):
                o = lead - half + k
                if o % SUBLANES == p:
                    acc = acc + shifted[o - p:o - p + CONV_TT, :] * w[k:k + 1, :]
        o_ref[pl.ds(t0, CONV_TT), :] = acc
        return carry

    lax.fori_loop(0, seq // CONV_TT, body, 0)


def _dwconv(x, w_dw, b_dw, n_batch, seq):
    d = x.shape[1]
    nc = d // CONV_CH
    return pl.pallas_call(
        functools.partial(_dwconv_kernel, seq=seq),
        grid=(n_batch, nc),
        in_specs=[pl.BlockSpec((seq, CONV_CH), lambda b, c: (b, c)),
                  pl.BlockSpec((CONV_WIDTH, CONV_CH), lambda b, c: (0, c)),
                  pl.BlockSpec((1, CONV_CH), lambda b, c: (0, c))],
        out_specs=pl.BlockSpec((seq, CONV_CH), lambda b, c: (b, c)),
        out_shape=jax.ShapeDtypeStruct((n_batch * seq, d), F32),
        scratch_shapes=[pltpu.VMEM((seq + 4 * SUBLANES, CONV_CH), F32)],
        compiler_params=_params(("parallel", "parallel")),
        name="dwconv",
    )(x, w_dw, b_dw.reshape(1, d))


def _proj_ln_kernel(*refs, pre_ln, has_bias, n_lat_tiles):
    it = iter(refs)
    a_ref = next(it)
    pre_g = next(it) if pre_ln else None
    pre_b = next(it) if pre_ln else None
    w_ref = next(it)
    bias_ref = next(it) if has_bias else None
    h_ref = next(it)
    hc_ref = next(it) if n_lat_tiles is not None else None
    gate_ref, lng_ref, lnb_ref, scf_ref, shf_ref, wr_ref = (next(it) for _ in range(6))
    hnew_ref, fin_ref, score_ref = (next(it) for _ in range(3))

    tm = a_ref.shape[0]
    half = tm // PROJ_SPLIT

    def project(p):
        a = a_ref[p * half:(p + 1) * half, :]
        if pre_ln:
            a = _silu(_layer_norm(a, pre_g[...], pre_b[...]))
        return jnp.dot(a.astype(BF16), w_ref[...], preferred_element_type=F32)

    def finish(p, m):
        rows = slice(p * half, (p + 1) * half)
        if has_bias:
            m = m + bias_ref[...]
        h_old = h_ref[rows, :]
        if n_lat_tiles is not None:
            h_old = jnp.where(pl.program_id(0) < n_lat_tiles, h_old, hc_ref[rows, :])
        hn = _layer_norm(DEEPNORM_ALPHA * h_old + gate_ref[0] * m, lng_ref[...], lnb_ref[...])
        hnew_ref[rows, :] = hn
        f = hn * (1.0 + scf_ref[0]) + shf_ref[0]
        _store_packed_rows(fin_ref.at[pl.ds(p * half * SUBLANES, half * SUBLANES)], f)
        score_ref[rows, :] = jax.nn.sigmoid(jnp.dot(f.astype(BF16), wr_ref[...], preferred_element_type=F32))

    m_next = project(0)
    for p in range(PROJ_SPLIT):
        m_cur = m_next
        if p + 1 < PROJ_SPLIT:
            m_next = project(p + 1)
        finish(p, m_cur)


def _proj_ln(a, w, bias, h, h_ctx, mods, layer, ln_g, ln_b, w_router, n_rows, seq, pre_ln=None):
    d = D_MODEL
    tm = PROJ_TILE
    row = lambda i: (i, 0)
    const = lambda i: (0, 0)
    n_lat_tiles = None if h_ctx is None else h.shape[0] // tm
    args, specs = [a], [pl.BlockSpec((tm, d), row)]
    if pre_ln is not None:
        args += [pre_ln[0].reshape(1, d), pre_ln[1].reshape(1, d)]
        specs += [pl.BlockSpec((1, d), const)] * 2
    args.append(w)
    specs.append(pl.BlockSpec((d, d), const))
    if bias is not None:
        args.append(bias.reshape(1, d))
        specs.append(pl.BlockSpec((1, d), const))
    if h_ctx is None:
        args.append(h)
        specs.append(pl.BlockSpec((tm, d), row))
    else:
        args += [h, h_ctx]
        specs += [pl.BlockSpec((tm, d), lambda i: (jnp.minimum(i, n_lat_tiles - 1), 0)),
                  pl.BlockSpec((tm, d), lambda i: (jnp.maximum(i - n_lat_tiles, 0), 0))]
    args += [mods, ln_g.reshape(1, d), ln_b.reshape(1, d), mods, mods, w_router]
    specs += [_mod_spec(layer, G_A, tm, seq), pl.BlockSpec((1, d), const), pl.BlockSpec((1, d), const),
              _mod_spec(layer, SC_F, tm, seq), _mod_spec(layer, SH_F, tm, seq),
              pl.BlockSpec((d, N_EXPERTS), const)]
    return pl.pallas_call(
        functools.partial(_proj_ln_kernel, pre_ln=pre_ln is not None, has_bias=bias is not None,
                          n_lat_tiles=n_lat_tiles),
        grid=(n_rows // tm,),
        in_specs=specs,
        out_specs=[pl.BlockSpec((tm, d), row), pl.BlockSpec((tm * SUBLANES, LANES), row),
                   pl.BlockSpec((tm, N_EXPERTS), row)],
        out_shape=[jax.ShapeDtypeStruct((n_rows, d), F32),
                   jax.ShapeDtypeStruct((n_rows * SUBLANES, LANES), jnp.uint32),
                   jax.ShapeDtypeStruct((n_rows, N_EXPERTS), F32)],
        compiler_params=_params(("parallel",)),
        name="proj_ln",
    )(*args)


def _router_kernel(s_ref, b_ref, ek_ref, rk_ref, gk_ref, cnt_ref, carry):
    tm, e = s_ref.shape

    @pl.when(pl.program_id(0) == 0)
    def _():
        carry[...] = jnp.zeros_like(carry)

    s = s_ref[...]
    lane = lax.broadcasted_iota(jnp.int32, (tm, e), 1).astype(F32)
    slot = lax.broadcasted_iota(jnp.int32, (tm, LANES), 1)
    work = s + b_ref[...]
    sel = jnp.zeros((tm, e), F32)
    e_acc = jnp.zeros((tm, LANES), F32)
    g_acc = jnp.zeros((tm, LANES), F32)
    g_sum = jnp.zeros((tm, 1), F32)
    picks = []
    for k in range(TOP_K):
        m = jnp.max(work, axis=-1, keepdims=True)
        idx = jnp.min(jnp.where(work == m, lane, float(e)), axis=-1, keepdims=True)
        onehot = lane == idx
        gk = jnp.sum(jnp.where(onehot, s, 0.0), axis=-1, keepdims=True)
        sel = jnp.where(onehot, 1.0, sel)
        work = jnp.where(onehot, -jnp.inf, work)
        e_acc = jnp.where(slot == k, idx, e_acc)
        g_acc = jnp.where(slot == k, gk, g_acc)
        g_sum = g_sum + gk
        picks.append(idx)

    r_i = lax.broadcasted_iota(jnp.int32, (tm, tm), 0)
    c_i = lax.broadcasted_iota(jnp.int32, (tm, tm), 1)
    tri = jnp.where(c_i < r_i, 1.0, 0.0).astype(BF16)
    rank = jnp.dot(tri, sel.astype(BF16), preferred_element_type=F32) + carry[...]
    r_acc = jnp.zeros((tm, LANES), F32)
    for k in range(TOP_K):
        rk = jnp.sum(jnp.where(lane == picks[k], rank, 0.0), axis=-1, keepdims=True)
        r_acc = jnp.where(slot == k, rk, r_acc)

    carry[...] = carry[...] + jnp.sum(sel, axis=0, keepdims=True)
    ek_ref[...] = e_acc[:, :TOP_K].astype(jnp.int32)
    rk_ref[...] = r_acc[:, :TOP_K].astype(jnp.int32)
    gk_ref[...] = (g_acc / g_sum * ROUTED_SCALE)[:, :TOP_K]
    cnt_ref[...] = carry[...].astype(jnp.int32)


def _router(scores, router_bias, n_rows):
    e = N_EXPERTS
    tm = ROUTER_TILE
    row = lambda i: (i, 0)
    return pl.pallas_call(
        _router_kernel,
        grid=(n_rows // tm,),
        in_specs=[pl.BlockSpec((tm, e), row), pl.BlockSpec((1, e), lambda i: (0, 0))],
        out_specs=[pl.BlockSpec((tm, TOP_K), row), pl.BlockSpec((tm, TOP_K), row), pl.BlockSpec((tm, TOP_K), row),
                   pl.BlockSpec((1, e), lambda i: (0, 0))],
        out_shape=[jax.ShapeDtypeStruct((n_rows, TOP_K), jnp.int32), jax.ShapeDtypeStruct((n_rows, TOP_K), jnp.int32),
                   jax.ShapeDtypeStruct((n_rows, TOP_K), F32), jax.ShapeDtypeStruct((1, e), jnp.int32)],
        scratch_shapes=[pltpu.VMEM((1, e), F32)],
        compiler_params=_params(("arbitrary",)),
        name="router",
    )(scores, router_bias.reshape(1, e).astype(F32))


def _layout(e_k, r_k, counts, tm_e):
    e = N_EXPERTS
    n_assign = e_k.size
    counts = counts.reshape(e)
    padded = (counts + tm_e - 1) // tm_e * tm_e
    pends = jnp.cumsum(padded)
    pstarts = pends - padded
    experts = jnp.arange(e, dtype=jnp.int32)
    start_of = jnp.sum(jnp.where(e_k[..., None] == experts, pstarts, 0), axis=-1)
    dest = (start_of + r_k).reshape(n_assign).astype(jnp.int32)
    n_blocks = (n_assign + e * (tm_e - 1) + tm_e - 1) // tm_e
    blk_row0 = jnp.arange(n_blocks, dtype=jnp.int32) * tm_e
    blk_e = jnp.minimum(jnp.sum(pends[None, :] <= blk_row0[:, None], axis=-1), e - 1).astype(jnp.int32)
    n_valid = (pends[-1] // tm_e).astype(jnp.int32).reshape(1)
    pad_info = jnp.concatenate([pstarts + counts, padded - counts]).astype(jnp.int32)
    later = (experts[None, :] > experts[:, None]) & (counts[None, :] > 0)
    next_e = jnp.min(jnp.where(later, experts[None, :], e), axis=1)
    next_e = jnp.where(next_e == e, -1, next_e)
    order_e = jnp.sum((experts[None, :] < experts[:, None]) & (counts[None, :] > 0), axis=1)
    chain = jnp.concatenate([next_e, order_e]).astype(jnp.int32)
    return dest, blk_e, n_valid, pad_info, chain, n_blocks


def _dispatch_kernel(pad_ref, dest_ref, f_ref, xs_ref, zbuf, sem, zsem):
    tm = f_ref.shape[0] // SUBLANES
    pad_bits = EXPERT_TILE.bit_length() - 1

    def zero_copies(e_idx):
        start = pad_ref[e_idx]
        pad = pad_ref[N_EXPERTS + e_idx]
        return [(((pad >> b) & 1) == 1, pltpu.make_async_copy(
            _row_tile(zbuf, 0, 1 << b), _row_tile(xs_ref, start + (pad & ((1 << b) - 1)), 1 << b), zsem))
            for b in range(pad_bits)]

    @pl.when(pl.program_id(0) == 0)
    def _():
        zbuf[...] = jnp.zeros_like(zbuf)

        def start_e(e_idx, carry):
            for cond, cp in zero_copies(e_idx):
                pl.when(cond)(cp.start)
            return carry

        def wait_e(e_idx, carry):
            for cond, cp in zero_copies(e_idx):
                pl.when(cond)(cp.wait)
            return carry

        lax.fori_loop(0, N_EXPERTS, start_e, 0)
        lax.fori_loop(0, N_EXPERTS, wait_e, 0)

    def tok(t, carry):
        for k in range(TOP_K):
            d = dest_ref[t * TOP_K + k]
            pltpu.make_async_copy(_row_tile(f_ref, t), _row_tile(xs_ref, d), sem).start(priority=k % 2)
        return carry

    lax.fori_loop(0, tm, tok, 0)
    for k in range(TOP_K):
        pltpu.make_async_copy(f_ref, _row_tile(xs_ref, 0, tm), sem).wait()


def _dispatch(f_in, dest, pad_info, n_rows, n_blocks):
    tm = DISPATCH_TILE
    grid_spec = pltpu.PrefetchScalarGridSpec(
        num_scalar_prefetch=1,
        grid=(n_rows // tm,),
        in_specs=[pl.BlockSpec((tm * TOP_K,), lambda i, p: (i,), memory_space=pltpu.SMEM),
                  pl.BlockSpec((tm * SUBLANES, LANES), lambda i, p: (i, 0))],
        out_specs=pl.BlockSpec(memory_space=pl.ANY),
        scratch_shapes=[pltpu.VMEM((EXPERT_TILE // 2 * SUBLANES, LANES), f_in.dtype), pltpu.SemaphoreType.DMA,
                        pltpu.SemaphoreType.DMA],
    )
    return pl.pallas_call(
        _dispatch_kernel,
        grid_spec=grid_spec,
        out_shape=jax.ShapeDtypeStruct((n_blocks * EXPERT_TILE * SUBLANES, LANES), f_in.dtype),
        compiler_params=_params(("arbitrary",)),
        name="dispatch",
    )(pad_info, dest, f_in)


def _swiglu_hidden(x, wg_ref, wu_ref):
    gv = jnp.dot(x, wg_ref[...], preferred_element_type=F32)
    uv = jnp.dot(x, wu_ref[...], preferred_element_type=F32)
    return (_silu(gv) * uv).astype(BF16)


def _expert_kernel(be_ref, nv_ref, chain_ref, x_ref, wg_hbm, wu_hbm, wd_hbm, y_ref,
                   wg32, wu32, wd32, wgb, wub, wdb, sems, *, layer):
    i = pl.program_id(0)
    valid = i < nv_ref[0]
    e_cur = be_ref[i]
    first = jnp.logical_or(i == 0, e_cur != be_ref[jnp.maximum(i - 1, 0)])
    d, de = wgb.shape
    n_chunks = 8

    def weight_copies(e_idx, slot):
        return [pltpu.make_async_copy(w_hbm.at[layer, e_idx], buf.at[slot], sems.at[slot, j])
                for j, (w_hbm, buf) in enumerate(((wg_hbm, wg32), (wu_hbm, wu32), (wd_hbm, wd32)))]

    @pl.when(jnp.logical_and(valid, first))
    def _():
        slot = chain_ref[N_EXPERTS + e_cur] & 1
        e_next = chain_ref[e_cur]

        @pl.when(i == 0)
        def _():
            for cp in weight_copies(e_cur, slot):
                cp.start()

        for cp in weight_copies(e_cur, slot):
            cp.wait()

        @pl.when(e_next >= 0)
        def _():
            for cp in weight_copies(e_next, 1 - slot):
                cp.start()

        def cast_chunk(c, carry):
            r0 = pl.multiple_of(c * (d // n_chunks), d // n_chunks)
            wgb[pl.ds(r0, d // n_chunks), :] = wg32[slot, pl.ds(r0, d // n_chunks), :].astype(BF16)
            wub[pl.ds(r0, d // n_chunks), :] = wu32[slot, pl.ds(r0, d // n_chunks), :].astype(BF16)
            r1 = pl.multiple_of(c * (de // n_chunks), de // n_chunks)
            wdb[pl.ds(r1, de // n_chunks), :] = wd32[slot, pl.ds(r1, de // n_chunks), :].astype(BF16)
            return carry

        lax.fori_loop(0, n_chunks, cast_chunk, 0)

    @pl.when(valid)
    def _():
        act = _swiglu_hidden(_load_packed_rows_bf16(x_ref), wgb, wub)
        _store_packed_rows(y_ref, jnp.dot(act, wdb[...], preferred_element_type=F32))


def _expert_ffn(xs, blk_e, n_valid, chain, w_gate, w_up, w_down, layer):
    n_rows = xs.shape[0] // SUBLANES
    d, de = w_gate.shape[2], w_gate.shape[3]
    tm = EXPERT_TILE
    n_blocks = n_rows // tm

    def xmap(i, be, nv, ch):
        return (jnp.minimum(i, nv[0] - 1), 0)

    grid_spec = pltpu.PrefetchScalarGridSpec(
        num_scalar_prefetch=3,
        grid=(n_blocks,),
        in_specs=[pl.BlockSpec((tm * SUBLANES, LANES), xmap),
                  pl.BlockSpec(memory_space=pl.ANY),
                  pl.BlockSpec(memory_space=pl.ANY),
                  pl.BlockSpec(memory_space=pl.ANY)],
        out_specs=pl.BlockSpec((tm * SUBLANES, LANES), xmap),
        scratch_shapes=[pltpu.VMEM((2, d, de), F32), pltpu.VMEM((2, d, de), F32), pltpu.VMEM((2, de, d), F32),
                        pltpu.VMEM((d, de), BF16), pltpu.VMEM((d, de), BF16), pltpu.VMEM((de, d), BF16),
                        pltpu.SemaphoreType.DMA((2, 3))],
    )
    return pl.pallas_call(
        functools.partial(_expert_kernel, layer=layer),
        grid_spec=grid_spec,
        out_shape=jax.ShapeDtypeStruct((n_rows * SUBLANES, LANES), jnp.uint32),
        compiler_params=_params(("arbitrary",)),
        name="expert_ffn",
    )(blk_e, n_valid, chain, xs, w_gate, w_up, w_down)


def _ffn_out_kernel(dcur_ref, dnext_ref, f_ref, gk_ref, wg_ref, wu_ref, wd_ref, h_ref, gate_ref, lng_ref, lnb_ref,
                    y_ref, o_ref, ybuf_a, ybuf_b, sems):
    i = pl.program_id(0)
    n_steps = pl.num_programs(0)
    tm = f_ref.shape[0] // SUBLANES
    slot = lax.rem(i, 2)
    ybufs = (ybuf_a, ybuf_b)

    def gather_token(dref, s, t):
        for k in range(TOP_K):
            row = dref[t * TOP_K + k]
            pltpu.make_async_copy(_row_tile(y_ref, row), _row_tile(ybufs[s].at[k], t),
                                  sems.at[s]).start(priority=k % 2)

    def wait_gathers(s):
        for k in range(TOP_K):
            pltpu.make_async_copy(_row_tile(y_ref, 0, tm), ybufs[s].at[k], sems.at[s]).wait()

    @pl.when(i == 0)
    def _():
        def tok(t, carry):
            gather_token(dcur_ref, 0, t)
            return carry

        lax.fori_loop(0, tm, tok, 0)

    def step(cur):
        n_pieces = 2 + SUBLANES
        piece = [(p * tm) // n_pieces for p in range(n_pieces + 1)]

        def issue_piece(p):
            for t in range(piece[p], piece[p + 1]):
                gather_token(dnext_ref, 1 - cur, t)

        issue_piece(0)
        act = _swiglu_hidden(_load_packed_rows_bf16(f_ref), wg_ref, wu_ref)
        issue_piece(1)
        f = jnp.dot(act, wd_ref[...], preferred_element_type=F32)

        gk = gk_ref[...]
        wait_gathers(cur)
        gcols = [jnp.broadcast_to(gk[:, k:k + 1], (tm, LANES)) for k in range(TOP_K)]
        parts = []
        for s_ in range(SUBLANES):
            issue_piece(2 + s_)
            r_lo = jnp.zeros((tm, LANES), F32)
            r_hi = jnp.zeros((tm, LANES), F32)
            for k in range(TOP_K):
                y_lo, y_hi = _load_packed_group(ybufs[cur].at[k], s_)
                r_lo = r_lo + gcols[k] * y_lo
                r_hi = r_hi + gcols[k] * y_hi
            parts += [r_lo, r_hi]
        f = f + jnp.concatenate(parts, axis=1)
        o_ref[...] = _layer_norm(DEEPNORM_ALPHA * h_ref[...] + gate_ref[0] * f, lng_ref[...], lnb_ref[...])

        @pl.when(i == n_steps - 1)
        def _():
            wait_gathers(1 - cur)

    for cur in range(2):
        pl.when(slot == cur)(functools.partial(step, cur))


def _ffn_out(f_in, y, dest, gates, ws_gate, ws_up, ws_down, h, mods, layer, ln_g, ln_b, n_rows, seq):
    d = D_MODEL
    ds_ = ws_gate.shape[1]
    tm = COMBINE_TILE
    n_steps = n_rows // tm
    row = lambda i: (i, 0)
    const = lambda i: (0, 0)
    return pl.pallas_call(
        _ffn_out_kernel,
        grid=(n_steps,),
        in_specs=[pl.BlockSpec((tm * TOP_K,), lambda i: (i,), memory_space=pltpu.SMEM),
                  pl.BlockSpec((tm * TOP_K,), lambda i: (jnp.minimum(i + 1, n_steps - 1),), memory_space=pltpu.SMEM),
                  pl.BlockSpec((tm * SUBLANES, LANES), row), pl.BlockSpec((tm, TOP_K), row),
                  pl.BlockSpec((d, ds_), const), pl.BlockSpec((d, ds_), const), pl.BlockSpec((ds_, d), const),
                  pl.BlockSpec((tm, d), row), _mod_spec(layer, G_F, tm, seq),
                  pl.BlockSpec((1, d), const), pl.BlockSpec((1, d), const),
                  pl.BlockSpec(memory_space=pl.ANY)],
        out_specs=pl.BlockSpec((tm, d), row),
        out_shape=jax.ShapeDtypeStruct((n_rows, d), F32),
        scratch_shapes=[pltpu.VMEM((TOP_K, tm * SUBLANES, LANES), jnp.uint32),
                        pltpu.VMEM((TOP_K, tm * SUBLANES, LANES), jnp.uint32), pltpu.SemaphoreType.DMA((2,))],
        compiler_params=_params(("arbitrary",)),
        name="ffn_out",
    )(dest, dest, f_in, gates, ws_gate, ws_up, ws_down, h, mods, ln_g.reshape(1, d), ln_b.reshape(1, d), y)


def _moe_and_norm(h, f_in, scores, mods, router_bias, we_gate, we_up, we_down, ws_gate, ws_up, ws_down,
                  ln_g, ln_b, n_rows, seq, layer):
    e_k, r_k, gates, counts = _router(scores, router_bias, n_rows)
    dest, blk_e, n_valid, pad_info, chain, n_blocks = _layout(e_k, r_k, counts, EXPERT_TILE)
    xs = _dispatch(f_in, dest, pad_info, n_rows, n_blocks)
    y = _expert_ffn(xs, blk_e, n_valid, chain, we_gate, we_up, we_down, layer)
    return _ffn_out(f_in, y, dest, gates, ws_gate.astype(BF16), ws_up.astype(BF16), ws_down.astype(BF16),
                    h, mods, layer, ln_g, ln_b, n_rows, seq)


def kernel(x, c, ctx, c_ctx, mod_w, mod_b, ln_mix_g, ln_mix_b, ln_ffn_g, ln_ffn_b, na_w_qkv, na_rpb, na_w_o,
           cv_w_pw1, cv_b_pw1, cv_w_dw, cv_b_dw, cv_ln_g, cv_ln_b, cv_w_pw2, cv_b_pw2, moe_w_router,
           moe_router_bias, moe_w_gate, moe_w_up, moe_w_down, sh_w_gate, sh_w_up, sh_w_down):
    n_batch, seq, d = x.shape
    ctx_len = ctx.shape[1]
    n_lat = n_batch * seq
    n_tok = n_lat + n_batch * ctx_len

    cvec = jnp.zeros((MOD_ROWS, d), F32).at[:n_batch].set(c).at[n_batch].set(c_ctx)
    mods = _mod_matmul(cvec, mod_w, mod_b).reshape(DEPTH * MOD_ROWS * MOD_PIECES, 1, d)
    x_lat = x.reshape(n_lat, d)
    x_ctx = ctx.reshape(n_batch * ctx_len, d)

    qkv = _mod_matmul_qkv(x_lat, x_ctx, mods, 0, na_w_qkv[0].astype(BF16), seq)
    attn = _na_attention(qkv, _na_bias_table(na_rpb[0]), n_batch, seq, ctx_len, n_tok)
    attn = _ctx_attention(qkv, attn, n_batch, seq, ctx_len)
    h, f_in, scores = _proj_ln(attn, na_w_o[0].astype(BF16), None, x_lat, x_ctx, mods, 0, ln_mix_g[0], ln_mix_b[0],
                               moe_w_router[0].astype(BF16), n_tok, seq)
    h = _moe_and_norm(h, f_in, scores, mods, moe_router_bias[0], moe_w_gate, moe_w_up, moe_w_down,
                      sh_w_gate[0], sh_w_up[0], sh_w_down[0], ln_ffn_g[0], ln_ffn_b[0], n_tok, seq, 0)

    glu = _mod_matmul_glu(h, mods, 1, cv_w_pw1[0].astype(BF16), cv_b_pw1[0], n_lat, seq)
    conv = _dwconv(glu, cv_w_dw[0], cv_b_dw[0], n_batch, seq)
    h, f_in, scores = _proj_ln(conv, cv_w_pw2[0].astype(BF16), cv_b_pw2[0], h, None, mods, 1, ln_mix_g[1],
                               ln_mix_b[1], moe_w_router[1].astype(BF16), n_lat, seq,
                               pre_ln=(cv_ln_g[0], cv_ln_b[0]))
    h = _moe_and_norm(h, f_in, scores, mods, moe_router_bias[1], moe_w_gate, moe_w_up, moe_w_down,
                      sh_w_gate[1], sh_w_up[1], sh_w_down[1], ln_ffn_g[1], ln_ffn_b[1], n_lat, seq, 1)
    return h.reshape(n_batch, seq, d)
```

```python
import functools

import jax
import jax.numpy as jnp
from jax import lax
from jax.experimental import pallas as pl
from jax.experimental.pallas import tpu as pltpu

D_MODEL = 2048
GRID_W = 64
N_HEADS = 16
HEAD_DIM = D_MODEL // N_HEADS
WIN_ROWS = 8
WIN_COLS = 16
CONV_WIDTH = 31
N_EXPERTS = 64
TOP_K = 8
ROUTED_SCALE = 2.5
LN_EPS = 1e-6
SUBLANES = 8
NEG_INF = -1e30
DEPTH = 2
DEEPNORM_ALPHA = (2 * DEPTH) ** 0.25

F32 = jnp.float32
BF16 = jnp.bfloat16

VMEM_LIMIT_BYTES = 56 * 1024 * 1024
DISPATCH_TILE = 512
PROJ_TILE = 512
PROJ_SPLIT = 2
ROUTER_TILE = 512
MM_ROW_TILE = 512
MOD_COL_TILE = 2048
QKV_COL_TILE = 3072
GLU_COL_TILE = 1024
EXPERT_TILE = 512
COMBINE_TILE = 128
ROWS_PER_STEP = 64
NA_BLOCK_ROWS = 2
NA_KEY_ROWS = WIN_ROWS + NA_BLOCK_ROWS
CONV_CH = 256
CONV_TT = 128


def _params(sem):
    return pltpu.CompilerParams(dimension_semantics=sem, vmem_limit_bytes=VMEM_LIMIT_BYTES)


def _layer_norm(z, g, b):
    mu = jnp.mean(z, axis=-1, keepdims=True)
    zc = z - mu
    var = jnp.mean(zc * zc, axis=-1, keepdims=True)
    return zc * lax.rsqrt(var + LN_EPS) * g + b


def _silu(x):
    return x * jax.nn.sigmoid(x)


LANES = 128
ROW_WORDS = D_MODEL // 2
assert ROW_WORDS == SUBLANES * LANES


def _store_packed_rows(ref, x):
    rows = x.shape[0]
    for s_ in range(SUBLANES):
        base = 2 * LANES * s_
        lo = pltpu.bitcast(x[:, base:base + LANES].astype(BF16).astype(F32), jnp.uint32)
        hi = pltpu.bitcast(x[:, base + LANES:base + 2 * LANES].astype(BF16).astype(F32), jnp.uint32)
        ref[pl.ds(s_, rows, stride=SUBLANES), :] = (lo >> 16) | (hi & jnp.uint32(0xFFFF0000))


def _load_packed_group(ref, s_):
    w = ref[pl.ds(s_, ref.shape[0] // SUBLANES, stride=SUBLANES), :]
    return pltpu.bitcast(w << 16, F32), pltpu.bitcast(w & jnp.uint32(0xFFFF0000), F32)


def _row_tile(ref, t, n=1):
    return ref.at[pl.ds(pl.multiple_of(t * SUBLANES, SUBLANES), n * SUBLANES)]


def _load_packed_rows_bf16(ref):
    parts = []
    for s_ in range(SUBLANES):
        lo, hi = _load_packed_group(ref, s_)
        parts += [lo.astype(BF16), hi.astype(BF16)]
    return jnp.concatenate(parts, axis=1)


def _mod_kernel(c_ref, w_ref, b_ref, o_ref):
    a = _silu(c_ref[...]).astype(BF16)
    o_ref[0] = jnp.dot(a, w_ref[0].astype(BF16), preferred_element_type=F32) + b_ref[0]


def _mod_matmul(cvec, w, b):
    m, k = cvec.shape
    layers, _, n = w.shape
    tn = MOD_COL_TILE
    return pl.pallas_call(
        _mod_kernel,
        grid=(layers, n // tn),
        in_specs=[pl.BlockSpec((m, k), lambda l, j: (0, 0)),
                  pl.BlockSpec((1, k, tn), lambda l, j: (l, 0, j)),
                  pl.BlockSpec((1, 1, tn), lambda l, j: (l, 0, j))],
        out_specs=pl.BlockSpec((1, m, tn), lambda l, j: (l, 0, j)),
        out_shape=jax.ShapeDtypeStruct((layers, m, n), F32),
        compiler_params=_params(("parallel", "arbitrary")),
        name="mod_matmul",
    )(cvec, w, b.reshape(layers, 1, n))


def _modmm_kernel(x_ref, c_ref, sc_ref, sh_ref, w_ref, o_ref, a_scr, *, n_lat_tiles):
    @pl.when(pl.program_id(1) == 0)
    def _():
        h = jnp.where(pl.program_id(0) < n_lat_tiles, x_ref[...], c_ref[...])
        a_scr[...] = (h * (1.0 + sc_ref[0]) + sh_ref[0]).astype(BF16)

    o_ref[...] = jnp.dot(a_scr[...], w_ref[...], preferred_element_type=F32).astype(o_ref.dtype)


def _modglu_kernel(h_ref, sc_ref, sh_ref, wa_ref, wg_ref, ba_ref, bg_ref, o_ref, a_scr):
    @pl.when(pl.program_id(1) == 0)
    def _():
        a_scr[...] = (h_ref[...] * (1.0 + sc_ref[0]) + sh_ref[0]).astype(BF16)

    a = a_scr[...]
    va = jnp.dot(a, wa_ref[...], preferred_element_type=F32) + ba_ref[...]
    vg = jnp.dot(a, wg_ref[...], preferred_element_type=F32) + bg_ref[...]
    o_ref[...] = va * jax.nn.sigmoid(vg)


MOD_ROWS = 8
MOD_PIECES = 6
SH_A, SC_A, G_A, SH_F, SC_F, G_F = range(MOD_PIECES)


def _mod_spec(layer, piece, tile_rows, seq):
    def index(i, *_):
        return ((layer * MOD_ROWS + (i * tile_rows) // seq) * MOD_PIECES + piece, 0, 0)

    return pl.BlockSpec((1, 1, D_MODEL), index)


def _mod_matmul_qkv(x_lat, x_ctx, mods, layer, w, seq):
    n_lat, k = x_lat.shape
    n_rows = n_lat + x_ctx.shape[0]
    n = w.shape[1]
    tm, tn = MM_ROW_TILE, QKV_COL_TILE
    n_lat_tiles = n_lat // tm
    return pl.pallas_call(
        functools.partial(_modmm_kernel, n_lat_tiles=n_lat_tiles),
        grid=(n_rows // tm, n // tn),
        in_specs=[pl.BlockSpec((tm, k), lambda i, j: (jnp.minimum(i, n_lat_tiles - 1), 0)),
                  pl.BlockSpec((tm, k), lambda i, j: (jnp.maximum(i - n_lat_tiles, 0), 0)),
                  _mod_spec(layer, SC_A, tm, seq), _mod_spec(layer, SH_A, tm, seq),
                  pl.BlockSpec((k, tn), lambda i, j: (0, j))],
        out_specs=pl.BlockSpec((tm, tn), lambda i, j: (i, j)),
        out_shape=jax.ShapeDtypeStruct((n_rows, n), BF16),
        scratch_shapes=[pltpu.VMEM((tm, k), BF16)],
        compiler_params=_params(("parallel", "arbitrary")),
        name="mod_qkv",
    )(x_lat, x_ctx, mods, mods, w)


def _mod_matmul_glu(h, mods, layer, w, b, n_rows, seq):
    k = h.shape[1]
    n = w.shape[1] // 2
    tm, tn = MM_ROW_TILE, GLU_COL_TILE
    nj = n // tn
    b2 = b.reshape(1, 2 * n)
    return pl.pallas_call(
        _modglu_kernel,
        grid=(n_rows // tm, nj),
        in_specs=[pl.BlockSpec((tm, k), lambda i, j: (i, 0)),
                  _mod_spec(layer, SC_A, tm, seq), _mod_spec(layer, SH_A, tm, seq),
                  pl.BlockSpec((k, tn), lambda i, j: (0, j)),
                  pl.BlockSpec((k, tn), lambda i, j: (0, j + nj)),
                  pl.BlockSpec((1, tn), lambda i, j: (0, j)),
                  pl.BlockSpec((1, tn), lambda i, j: (0, j + nj))],
        out_specs=pl.BlockSpec((tm, tn), lambda i, j: (i, j)),
        out_shape=jax.ShapeDtypeStruct((n_rows, n), F32),
        scratch_shapes=[pltpu.VMEM((tm, k), BF16)],
        compiler_params=_params(("parallel", "arbitrary")),
        name="mod_pw1_glu",
    )(h, mods, mods, w, w, b2, b2)


def _na_block_window(r0, rows):
    return jnp.clip(r0 - WIN_ROWS // 2, 0, rows - NA_KEY_ROWS)


def _na_block_kinds(rows):
    def geometry(r0):
        k0 = min(max(r0 - WIN_ROWS // 2, 0), rows - NA_KEY_ROWS)
        out = []
        for r in range(r0, r0 + NA_BLOCK_ROWS):
            rs = min(max(r - WIN_ROWS // 2, 0), rows - WIN_ROWS)
            assert 0 <= rs - k0 <= NA_KEY_ROWS - WIN_ROWS
            out.append((rs - k0, r - rs))
        return tuple(out)

    kinds, kind_of = [], {}
    for r0 in range(0, rows, NA_BLOCK_ROWS):
        geo = geometry(r0)
        if geo not in kinds:
            kinds.append(geo)
        kind_of[r0] = kinds.index(geo)
    return kinds, kind_of


def _na_kernel(q_ref, k_ref, v_ref, kc_ref, vc_ref, bias_ref, o_ref, *, rows):
    scale = HEAD_DIM ** -0.5
    contract_last = (((1,), (1,)), ((), ()))
    row0 = pl.program_id(2) * ROWS_PER_STEP
    n_keys = NA_KEY_ROWS * GRID_W
    bq = NA_BLOCK_ROWS * GRID_W
    n_blk = ROWS_PER_STEP // NA_BLOCK_ROWS
    _, kind_of = _na_block_kinds(rows)
    kc = kc_ref[...]
    vc = vc_ref[...]

    def block_kind(r0):
        common = max(set(kind_of.values()), key=list(kind_of.values()).count)
        kind = jnp.int32(common)
        for r_static, kd in kind_of.items():
            if kd != common:
                kind = jnp.where(r0 == r_static, kd, kind)
        return kind

    def key_start(b):
        return pl.multiple_of(_na_block_window(row0 + b * NA_BLOCK_ROWS, rows) * GRID_W, GRID_W)

    def scores(b):
        q = q_ref[b * bq:(b + 1) * bq, :]
        kw = k_ref[pl.ds(key_start(b), n_keys), :]
        bias = bias_ref[0, block_kind(row0 + b * NA_BLOCK_ROWS)]
        s_loc = lax.dot_general(q, kw, contract_last, preferred_element_type=F32) * scale + bias
        s_ctx = lax.dot_general(q, kc, contract_last, preferred_element_type=F32) * scale
        return s_loc, s_ctx

    def finish(b, s_loc, s_ctx):
        vw = v_ref[pl.ds(key_start(b), n_keys), :]
        m = jnp.maximum(jnp.max(s_loc, axis=-1, keepdims=True), jnp.max(s_ctx, axis=-1, keepdims=True))
        p_loc = jnp.exp(s_loc - m)
        p_ctx = jnp.exp(s_ctx - m)
        denom = jnp.sum(p_loc, axis=-1, keepdims=True) + jnp.sum(p_ctx, axis=-1, keepdims=True)
        o = (jnp.dot(p_loc.astype(BF16), vw, preferred_element_type=F32)
             + jnp.dot(p_ctx.astype(BF16), vc, preferred_element_type=F32))
        o_ref[b * bq:(b + 1) * bq, :] = (o / denom).astype(o_ref.dtype)

    s_next = scores(0)
    for b in range(n_blk):
        s_cur = s_next
        if b + 1 < n_blk:
            s_next = scores(b + 1)
        finish(b, *s_cur)


def _na_block_bias(row_table, rows):
    n_heads = row_table.shape[0]
    kinds, _ = _na_block_kinds(rows)
    per_kind = []
    for geo in kinds:
        per_row = []
        for rel, off in geo:
            before = jnp.full((n_heads, GRID_W, rel * GRID_W), NEG_INF, F32)
            after = jnp.full((n_heads, GRID_W, (NA_KEY_ROWS - WIN_ROWS - rel) * GRID_W), NEG_INF, F32)
            per_row.append(jnp.concatenate([before, row_table[:, off], after], axis=-1))
        per_kind.append(jnp.concatenate(per_row, axis=1))
    return jnp.stack(per_kind, axis=1)


def _na_bias_table(rpb):
    col = jnp.arange(GRID_W)
    cstart = jnp.clip(col - WIN_COLS // 2, 0, GRID_W - WIN_COLS)
    col_ok = (col[None, :] >= cstart[:, None]) & (col[None, :] < cstart[:, None] + WIN_COLS)
    cidx = jnp.clip(col[None, :] - col[:, None] + WIN_COLS - 1, 0, 2 * WIN_COLS - 2)
    onehot = (cidx[:, :, None] == jnp.arange(2 * WIN_COLS - 1)).astype(F32)
    t = jnp.einsum('hrc,qkc->hrqk', rpb.astype(F32), onehot, precision=lax.Precision.HIGHEST)
    t = jnp.where(col_ok[None, None], t, NEG_INF)
    t = jnp.stack([t[:, WIN_ROWS - 1 - off:2 * WIN_ROWS - 1 - off] for off in range(WIN_ROWS)], axis=1)
    t = t.transpose(0, 1, 3, 2, 4)
    return t.reshape(rpb.shape[0], WIN_ROWS, GRID_W, WIN_ROWS * GRID_W)


def _na_attention(qkv, bias_table, n_batch, seq, ctx_len, n_rows_out):
    rows = seq // GRID_W
    groups = rows // ROWS_PER_STEP
    tq = ROWS_PER_STEP * GRID_W
    ctx_blk0 = (n_batch * seq) // ctx_len
    h_ = N_HEADS
    block_bias = _na_block_bias(bias_table, rows)
    n_kinds = block_bias.shape[1]
    return pl.pallas_call(
        functools.partial(_na_kernel, rows=rows),
        grid=(n_batch, h_, groups),
        in_specs=[pl.BlockSpec((tq, HEAD_DIM), lambda b, h, g: (b * groups + g, h)),
                  pl.BlockSpec((seq, HEAD_DIM), lambda b, h, g: (b, h_ + h)),
                  pl.BlockSpec((seq, HEAD_DIM), lambda b, h, g: (b, 2 * h_ + h)),
                  pl.BlockSpec((ctx_len, HEAD_DIM), lambda b, h, g: (ctx_blk0 + b, h_ + h)),
                  pl.BlockSpec((ctx_len, HEAD_DIM), lambda b, h, g: (ctx_blk0 + b, 2 * h_ + h)),
                  pl.BlockSpec((1, n_kinds, NA_BLOCK_ROWS * GRID_W, NA_KEY_ROWS * GRID_W),
                               lambda b, h, g: (h, 0, 0, 0))],
        out_specs=pl.BlockSpec((tq, HEAD_DIM), lambda b, h, g: (b * groups + g, h)),
        out_shape=jax.ShapeDtypeStruct((n_rows_out, D_MODEL), BF16),
        compiler_params=_params(("parallel", "parallel", "arbitrary")),
        name="na_attention",
    )(qkv, qkv, qkv, qkv, qkv, block_bias)


def _ctx_attn_kernel(q_ref, k_ref, v_ref, prev_ref, o_ref):
    del prev_ref
    scale = HEAD_DIM ** -0.5
    for h in range(N_HEADS):
        cols = slice(h * HEAD_DIM, (h + 1) * HEAD_DIM)
        s = lax.dot_general(q_ref[:, cols], k_ref[:, cols], (((1,), (1,)), ((), ())),
                            preferred_element_type=F32) * scale
        m = jnp.max(s, axis=-1, keepdims=True)
        p = jnp.exp(s - m)
        denom = jnp.sum(p, axis=-1, keepdims=True)
        o = jnp.dot(p.astype(BF16), v_ref[:, cols], preferred_element_type=F32)
        o_ref[:, cols] = (o / denom).astype(o_ref.dtype)


def _ctx_attention(qkv, attn_out, n_batch, seq, ctx_len):
    ctx_blk0 = (n_batch * seq) // ctx_len
    d = D_MODEL
    return pl.pallas_call(
        _ctx_attn_kernel,
        grid=(n_batch,),
        in_specs=[pl.BlockSpec((ctx_len, d), lambda b: (ctx_blk0 + b, 0)),
                  pl.BlockSpec((ctx_len, d), lambda b: (ctx_blk0 + b, 1)),
                  pl.BlockSpec((ctx_len, d), lambda b: (ctx_blk0 + b, 2)),
                  pl.BlockSpec(memory_space=pl.ANY)],
        out_specs=pl.BlockSpec((ctx_len, d), lambda b: (ctx_blk0 + b, 0)),
        out_shape=jax.ShapeDtypeStruct(attn_out.shape, attn_out.dtype),
        input_output_aliases={3: 0},
        compiler_params=_params(("parallel",)),
        name="ctx_attention",
    )(qkv, qkv, qkv, attn_out)


def _dwconv_kernel(x_ref, w_ref, b_ref, o_ref, xpad, *, seq):
    half = CONV_WIDTH // 2
    lead = 2 * SUBLANES
    ch = x_ref.shape[1]
    xpad[pl.ds(0, lead), :] = jnp.zeros((lead, ch), F32)
    xpad[pl.ds(lead + seq, lead), :] = jnp.zeros((lead, ch), F32)
    xpad[pl.ds(lead, seq), :] = x_ref[...]
    w = w_ref[...]
    bias = b_ref[...]
    win_rows = CONV_TT + 2 * lead

    def body(c, carry):
        t0 = pl.multiple_of(c * CONV_TT, CONV_TT)
        win = xpad[pl.ds(t0, win_rows), :]
        acc = jnp.broadcast_to(bias, (CONV_TT, ch))
        for p in range(SUBLANES):
            shifted = win if p == 0 else pltpu.roll(win, win_rows - p, axis=0)
            for k in range(CONV_WIDTH):
                o = lead - half + k
                if o % SUBLANES == p:
                    acc = acc + shifted[o - p:o - p + CONV_TT, :] * w[k:k + 1, :]
        o_ref[pl.ds(t0, CONV_TT), :] = acc
        return carry

    lax.fori_loop(0, seq // CONV_TT, body, 0)


def _dwconv(x, w_dw, b_dw, n_batch, seq):
    d = x.shape[1]
    nc = d // CONV_CH
    return pl.pallas_call(
        functools.partial(_dwconv_kernel, seq=seq),
        grid=(n_batch, nc),
        in_specs=[pl.BlockSpec((seq, CONV_CH), lambda b, c: (b, c)),
                  pl.BlockSpec((CONV_WIDTH, CONV_CH), lambda b, c: (0, c)),
                  pl.BlockSpec((1, CONV_CH), lambda b, c: (0, c))],
        out_specs=pl.BlockSpec((seq, CONV_CH), lambda b, c: (b, c)),
        out_shape=jax.ShapeDtypeStruct((n_batch * seq, d), F32),
        scratch_shapes=[pltpu.VMEM((seq + 4 * SUBLANES, CONV_CH), F32)],
        compiler_params=_params(("parallel", "parallel")),
        name="dwconv",
    )(x, w_dw, b_dw.reshape(1, d))


def _proj_ln_kernel(*refs, pre_ln, has_bias, n_lat_tiles):
    it = iter(refs)
    a_ref = next(it)
    pre_g = next(it) if pre_ln else None
    pre_b = next(it) if pre_ln else None
    w_ref = next(it)
    bias_ref = next(it) if has_bias else None
    h_ref = next(it)
    hc_ref = next(it) if n_lat_tiles is not None else None
    gate_ref, lng_ref, lnb_ref, scf_ref, shf_ref, wr_ref = (next(it) for _ in range(6))
    hnew_ref, fin_ref, score_ref = (next(it) for _ in range(3))

    tm = a_ref.shape[0]
    half = tm // PROJ_SPLIT

    def project(p):
        a = a_ref[p * half:(p + 1) * half, :]
        if pre_ln:
            a = _silu(_layer_norm(a, pre_g[...], pre_b[...]))
        return jnp.dot(a.astype(BF16), w_ref[...], preferred_element_type=F32)

    def finish(p, m):
        rows = slice(p * half, (p + 1) * half)
        if has_bias:
            m = m + bias_ref[...]
        h_old = h_ref[rows, :]
        if n_lat_tiles is not None:
            h_old = jnp.where(pl.program_id(0) < n_lat_tiles, h_old, hc_ref[rows, :])
        hn = _layer_norm(DEEPNORM_ALPHA * h_old + gate_ref[0] * m, lng_ref[...], lnb_ref[...])
        hnew_ref[rows, :] = hn
        f = hn * (1.0 + scf_ref[0]) + shf_ref[0]
        _store_packed_rows(fin_ref.at[pl.ds(p * half * SUBLANES, half * SUBLANES)], f)
        score_ref[rows, :] = jax.nn.sigmoid(jnp.dot(f.astype(BF16), wr_ref[...], preferred_element_type=F32))

    m_next = project(0)
    for p in range(PROJ_SPLIT):
        m_cur = m_next
        if p + 1 < PROJ_SPLIT:
            m_next = project(p + 1)
        finish(p, m_cur)


def _proj_ln(a, w, bias, h, h_ctx, mods, layer, ln_g, ln_b, w_router, n_rows, seq, pre_ln=None):
    d = D_MODEL
    tm = PROJ_TILE
    row = lambda i: (i, 0)
    const = lambda i: (0, 0)
    n_lat_tiles = None if h_ctx is None else h.shape[0] // tm
    args, specs = [a], [pl.BlockSpec((tm, d), row)]
    if pre_ln is not None:
        args += [pre_ln[0].reshape(1, d), pre_ln[1].reshape(1, d)]
        specs += [pl.BlockSpec((1, d), const)] * 2
    args.append(w)
    specs.append(pl.BlockSpec((d, d), const))
    if bias is not None:
        args.append(bias.reshape(1, d))
        specs.append(pl.BlockSpec((1, d), const))
    if h_ctx is None:
        args.append(h)
        specs.append(pl.BlockSpec((tm, d), row))
    else:
        args += [h, h_ctx]
        specs += [pl.BlockSpec((tm, d), lambda i: (jnp.minimum(i, n_lat_tiles - 1), 0)),
                  pl.BlockSpec((tm, d), lambda i: (jnp.maximum(i - n_lat_tiles, 0), 0))]
    args += [mods, ln_g.reshape(1, d), ln_b.reshape(1, d), mods, mods, w_router]
    specs += [_mod_spec(layer, G_A, tm, seq), pl.BlockSpec((1, d), const), pl.BlockSpec((1, d), const),
              _mod_spec(layer, SC_F, tm, seq), _mod_spec(layer, SH_F, tm, seq),
              pl.BlockSpec((d, N_EXPERTS), const)]
    return pl.pallas_call(
        functools.partial(_proj_ln_kernel, pre_ln=pre_ln is not None, has_bias=bias is not None,
                          n_lat_tiles=n_lat_tiles),
        grid=(n_rows // tm,),
        in_specs=specs,
        out_specs=[pl.BlockSpec((tm, d), row), pl.BlockSpec((tm * SUBLANES, LANES), row),
                   pl.BlockSpec((tm, N_EXPERTS), row)],
        out_shape=[jax.ShapeDtypeStruct((n_rows, d), F32),
                   jax.ShapeDtypeStruct((n_rows * SUBLANES, LANES), jnp.uint32),
                   jax.ShapeDtypeStruct((n_rows, N_EXPERTS), F32)],
        compiler_params=_params(("parallel",)),
        name="proj_ln",
    )(*args)


def _router_kernel(s_ref, b_ref, ek_ref, rk_ref, gk_ref, cnt_ref, carry):
    tm, e = s_ref.shape

    @pl.when(pl.program_id(0) == 0)
    def _():
        carry[...] = jnp.zeros_like(carry)

    s = s_ref[...]
    lane = lax.broadcasted_iota(jnp.int32, (tm, e), 1).astype(F32)
    slot = lax.broadcasted_iota(jnp.int32, (tm, LANES), 1)
    work = s + b_ref[...]
    sel = jnp.zeros((tm, e), F32)
    e_acc = jnp.zeros((tm, LANES), F32)
    g_acc = jnp.zeros((tm, LANES), F32)
    g_sum = jnp.zeros((tm, 1), F32)
    picks = []
    for k in range(TOP_K):
        m = jnp.max(work, axis=-1, keepdims=True)
        idx = jnp.min(jnp.where(work == m, lane, float(e)), axis=-1, keepdims=True)
        onehot = lane == idx
        gk = jnp.sum(jnp.where(onehot, s, 0.0), axis=-1, keepdims=True)
        sel = jnp.where(onehot, 1.0, sel)
        work = jnp.where(onehot, -jnp.inf, work)
        e_acc = jnp.where(slot == k, idx, e_acc)
        g_acc = jnp.where(slot == k, gk, g_acc)
        g_sum = g_sum + gk
        picks.append(idx)

    r_i = lax.broadcasted_iota(jnp.int32, (tm, tm), 0)
    c_i = lax.broadcasted_iota(jnp.int32, (tm, tm), 1)
    tri = jnp.where(c_i < r_i, 1.0, 0.0).astype(BF16)
    rank = jnp.dot(tri, sel.astype(BF16), preferred_element_type=F32) + carry[...]
    r_acc = jnp.zeros((tm, LANES), F32)
    for k in range(TOP_K):
        rk = jnp.sum(jnp.where(lane == picks[k], rank, 0.0), axis=-1, keepdims=True)
        r_acc = jnp.where(slot == k, rk, r_acc)

    carry[...] = carry[...] + jnp.sum(sel, axis=0, keepdims=True)
    ek_ref[...] = e_acc[:, :TOP_K].astype(jnp.int32)
    rk_ref[...] = r_acc[:, :TOP_K].astype(jnp.int32)
    gk_ref[...] = (g_acc / g_sum * ROUTED_SCALE)[:, :TOP_K]
    cnt_ref[...] = carry[...].astype(jnp.int32)


def _router(scores, router_bias, n_rows):
    e = N_EXPERTS
    tm = ROUTER_TILE
    row = lambda i: (i, 0)
    return pl.pallas_call(
        _router_kernel,
        grid=(n_rows // tm,),
        in_specs=[pl.BlockSpec((tm, e), row), pl.BlockSpec((1, e), lambda i: (0, 0))],
        out_specs=[pl.BlockSpec((tm, TOP_K), row), pl.BlockSpec((tm, TOP_K), row), pl.BlockSpec((tm, TOP_K), row),
                   pl.BlockSpec((1, e), lambda i: (0, 0))],
        out_shape=[jax.ShapeDtypeStruct((n_rows, TOP_K), jnp.int32), jax.ShapeDtypeStruct((n_rows, TOP_K), jnp.int32),
                   jax.ShapeDtypeStruct((n_rows, TOP_K), F32), jax.ShapeDtypeStruct((1, e), jnp.int32)],
        scratch_shapes=[pltpu.VMEM((1, e), F32)],
        compiler_params=_params(("arbitrary",)),
        name="router",
    )(scores, router_bias.reshape(1, e).astype(F32))


def _layout(e_k, r_k, counts, tm_e):
    e = N_EXPERTS
    n_assign = e_k.size
    counts = counts.reshape(e)
    padded = (counts + tm_e - 1) // tm_e * tm_e
    pends = jnp.cumsum(padded)
    pstarts = pends - padded
    experts = jnp.arange(e, dtype=jnp.int32)
    start_of = jnp.sum(jnp.where(e_k[..., None] == experts, pstarts, 0), axis=-1)
    dest = (start_of + r_k).reshape(n_assign).astype(jnp.int32)
    n_blocks = (n_assign + e * (tm_e - 1) + tm_e - 1) // tm_e
    blk_row0 = jnp.arange(n_blocks, dtype=jnp.int32) * tm_e
    blk_e = jnp.minimum(jnp.sum(pends[None, :] <= blk_row0[:, None], axis=-1), e - 1).astype(jnp.int32)
    n_valid = (pends[-1] // tm_e).astype(jnp.int32).reshape(1)
    pad_info = jnp.concatenate([pstarts + counts, padded - counts]).astype(jnp.int32)
    later = (experts[None, :] > experts[:, None]) & (counts[None, :] > 0)
    next_e = jnp.min(jnp.where(later, experts[None, :], e), axis=1)
    next_e = jnp.where(next_e == e, -1, next_e)
    order_e = jnp.sum((experts[None, :] < experts[:, None]) & (counts[None, :] > 0), axis=1)
    chain = jnp.concatenate([next_e, order_e]).astype(jnp.int32)
    return dest, blk_e, n_valid, pad_info, chain, n_blocks


def _dispatch_kernel(pad_ref, dest_ref, f_ref, xs_ref, zbuf, sem, zsem):
    tm = f_ref.shape[0] // SUBLANES
    pad_bits = EXPERT_TILE.bit_length() - 1

    def zero_copies(e_idx):
        start = pad_ref[e_idx]
        pad = pad_ref[N_EXPERTS + e_idx]
        return [(((pad >> b) & 1) == 1, pltpu.make_async_copy(
            _row_tile(zbuf, 0, 1 << b), _row_tile(xs_ref, start + (pad & ((1 << b) - 1)), 1 << b), zsem))
            for b in range(pad_bits)]

    @pl.when(pl.program_id(0) == 0)
    def _():
        zbuf[...] = jnp.zeros_like(zbuf)

        def start_e(e_idx, carry):
            for cond, cp in zero_copies(e_idx):
                pl.when(cond)(cp.start)
            return carry

        def wait_e(e_idx, carry):
            for cond, cp in zero_copies(e_idx):
                pl.when(cond)(cp.wait)
            return carry

        lax.fori_loop(0, N_EXPERTS, start_e, 0)
        lax.fori_loop(0, N_EXPERTS, wait_e, 0)

    def tok(t, carry):
        for k in range(TOP_K):
            d = dest_ref[t * TOP_K + k]
            pltpu.make_async_copy(_row_tile(f_ref, t), _row_tile(xs_ref, d), sem).start(priority=k % 2)
        return carry

    lax.fori_loop(0, tm, tok, 0)
    for k in range(TOP_K):
        pltpu.make_async_copy(f_ref, _row_tile(xs_ref, 0, tm), sem).wait()


def _dispatch(f_in, dest, pad_info, n_rows, n_blocks):
    tm = DISPATCH_TILE
    grid_spec = pltpu.PrefetchScalarGridSpec(
        num_scalar_prefetch=1,
        grid=(n_rows // tm,),
        in_specs=[pl.BlockSpec((tm * TOP_K,), lambda i, p: (i,), memory_space=pltpu.SMEM),
                  pl.BlockSpec((tm * SUBLANES, LANES), lambda i, p: (i, 0))],
        out_specs=pl.BlockSpec(memory_space=pl.ANY),
        scratch_shapes=[pltpu.VMEM((EXPERT_TILE // 2 * SUBLANES, LANES), f_in.dtype), pltpu.SemaphoreType.DMA,
                        pltpu.SemaphoreType.DMA],
    )
    return pl.pallas_call(
        _dispatch_kernel,
        grid_spec=grid_spec,
        out_shape=jax.ShapeDtypeStruct((n_blocks * EXPERT_TILE * SUBLANES, LANES), f_in.dtype),
        compiler_params=_params(("arbitrary",)),
        name="dispatch",
    )(pad_info, dest, f_in)


def _swiglu_hidden(x, wg_ref, wu_ref):
    gv = jnp.dot(x, wg_ref[...], preferred_element_type=F32)
    uv = jnp.dot(x, wu_ref[...], preferred_element_type=F32)
    return (_silu(gv) * uv).astype(BF16)


def _expert_kernel(be_ref, nv_ref, chain_ref, x_ref, wg_hbm, wu_hbm, wd_hbm, y_ref,
                   wg32, wu32, wd32, wgb, wub, wdb, sems, *, layer):
    i = pl.program_id(0)
    valid = i < nv_ref[0]
    e_cur = be_ref[i]
    first = jnp.logical_or(i == 0, e_cur != be_ref[jnp.maximum(i - 1, 0)])
    d, de = wgb.shape
    n_chunks = 8

    def weight_copies(e_idx, slot):
        return [pltpu.make_async_copy(w_hbm.at[layer, e_idx], buf.at[slot], sems.at[slot, j])
                for j, (w_hbm, buf) in enumerate(((wg_hbm, wg32), (wu_hbm, wu32), (wd_hbm, wd32)))]

    @pl.when(jnp.logical_and(valid, first))
    def _():
        slot = chain_ref[N_EXPERTS + e_cur] & 1
        e_next = chain_ref[e_cur]

        @pl.when(i == 0)
        def _():
            for cp in weight_copies(e_cur, slot):
                cp.start()

        for cp in weight_copies(e_cur, slot):
            cp.wait()

        @pl.when(e_next >= 0)
        def _():
            for cp in weight_copies(e_next, 1 - slot):
                cp.start()

        def cast_chunk(c, carry):
            r0 = pl.multiple_of(c * (d // n_chunks), d // n_chunks)
            wgb[pl.ds(r0, d // n_chunks), :] = wg32[slot, pl.ds(r0, d // n_chunks), :].astype(BF16)
            wub[pl.ds(r0, d // n_chunks), :] = wu32[slot, pl.ds(r0, d // n_chunks), :].astype(BF16)
            r1 = pl.multiple_of(c * (de // n_chunks), de // n_chunks)
            wdb[pl.ds(r1, de // n_chunks), :] = wd32[slot, pl.ds(r1, de // n_chunks), :].astype(BF16)
            return carry

        lax.fori_loop(0, n_chunks, cast_chunk, 0)

    @pl.when(valid)
    def _():
        act = _swiglu_hidden(_load_packed_rows_bf16(x_ref), wgb, wub)
        _store_packed_rows(y_ref, jnp.dot(act, wdb[...], preferred_element_type=F32))


def _expert_ffn(xs, blk_e, n_valid, chain, w_gate, w_up, w_down, layer):
    n_rows = xs.shape[0] // SUBLANES
    d, de = w_gate.shape[2], w_gate.shape[3]
    tm = EXPERT_TILE
    n_blocks = n_rows // tm

    def xmap(i, be, nv, ch):
        return (jnp.minimum(i, nv[0] - 1), 0)

    grid_spec = pltpu.PrefetchScalarGridSpec(
        num_scalar_prefetch=3,
        grid=(n_blocks,),
        in_specs=[pl.BlockSpec((tm * SUBLANES, LANES), xmap),
                  pl.BlockSpec(memory_space=pl.ANY),
                  pl.BlockSpec(memory_space=pl.ANY),
                  pl.BlockSpec(memory_space=pl.ANY)],
        out_specs=pl.BlockSpec((tm * SUBLANES, LANES), xmap),
        scratch_shapes=[pltpu.VMEM((2, d, de), F32), pltpu.VMEM((2, d, de), F32), pltpu.VMEM((2, de, d), F32),
                        pltpu.VMEM((d, de), BF16), pltpu.VMEM((d, de), BF16), pltpu.VMEM((de, d), BF16),
                        pltpu.SemaphoreType.DMA((2, 3))],
    )
    return pl.pallas_call(
        functools.partial(_expert_kernel, layer=layer),
        grid_spec=grid_spec,
        out_shape=jax.ShapeDtypeStruct((n_rows * SUBLANES, LANES), jnp.uint32),
        compiler_params=_params(("arbitrary",)),
        name="expert_ffn",
    )(blk_e, n_valid, chain, xs, w_gate, w_up, w_down)


def _ffn_out_kernel(dcur_ref, dnext_ref, f_ref, gk_ref, wg_ref, wu_ref, wd_ref, h_ref, gate_ref, lng_ref, lnb_ref,
                    y_ref, o_ref, ybuf_a, ybuf_b, sems):
    i = pl.program_id(0)
    n_steps = pl.num_programs(0)
    tm = f_ref.shape[0] // SUBLANES
    slot = lax.rem(i, 2)
    ybufs = (ybuf_a, ybuf_b)

    def gather_token(dref, s, t):
        for k in range(TOP_K):
            row = dref[t * TOP_K + k]
            pltpu.make_async_copy(_row_tile(y_ref, row), _row_tile(ybufs[s].at[k], t),
                                  sems.at[s]).start(priority=k % 2)

    def wait_gathers(s):
        for k in range(TOP_K):
            pltpu.make_async_copy(_row_tile(y_ref, 0, tm), ybufs[s].at[k], sems.at[s]).wait()

    @pl.when(i == 0)
    def _():
        def tok(t, carry):
            gather_token(dcur_ref, 0, t)
            return carry

        lax.fori_loop(0, tm, tok, 0)

    def step(cur):
        n_pieces = 2 + SUBLANES
        piece = [(p * tm) // n_pieces for p in range(n_pieces + 1)]

        def issue_piece(p):
            for t in range(piece[p], piece[p + 1]):
                gather_token(dnext_ref, 1 - cur, t)

        issue_piece(0)
        act = _swiglu_hidden(_load_packed_rows_bf16(f_ref), wg_ref, wu_ref)
        issue_piece(1)
        f = jnp.dot(act, wd_ref[...], preferred_element_type=F32)

        gk = gk_ref[...]
        wait_gathers(cur)
        gcols = [jnp.broadcast_to(gk[:, k:k + 1], (tm, LANES)) for k in range(TOP_K)]
        parts = []
        for s_ in range(SUBLANES):
            issue_piece(2 + s_)
            r_lo = jnp.zeros((tm, LANES), F32)
            r_hi = jnp.zeros((tm, LANES), F32)
            for k in range(TOP_K):
                y_lo, y_hi = _load_packed_group(ybufs[cur].at[k], s_)
                r_lo = r_lo + gcols[k] * y_lo
                r_hi = r_hi + gcols[k] * y_hi
            parts += [r_lo, r_hi]
        f = f + jnp.concatenate(parts, axis=1)
        o_ref[...] = _layer_norm(DEEPNORM_ALPHA * h_ref[...] + gate_ref[0] * f, lng_ref[...], lnb_ref[...])

        @pl.when(i == n_steps - 1)
        def _():
            wait_gathers(1 - cur)

    for cur in range(2):
        pl.when(slot == cur)(functools.partial(step, cur))


def _ffn_out(f_in, y, dest, gates, ws_gate, ws_up, ws_down, h, mods, layer, ln_g, ln_b, n_rows, seq):
    d = D_MODEL
    ds_ = ws_gate.shape[1]
    tm = COMBINE_TILE
    n_steps = n_rows // tm
    row = lambda i: (i, 0)
    const = lambda i: (0, 0)
    return pl.pallas_call(
        _ffn_out_kernel,
        grid=(n_steps,),
        in_specs=[pl.BlockSpec((tm * TOP_K,), lambda i: (i,), memory_space=pltpu.SMEM),
                  pl.BlockSpec((tm * TOP_K,), lambda i: (jnp.minimum(i + 1, n_steps - 1),), memory_space=pltpu.SMEM),
                  pl.BlockSpec((tm * SUBLANES, LANES), row), pl.BlockSpec((tm, TOP_K), row),
                  pl.BlockSpec((d, ds_), const), pl.BlockSpec((d, ds_), const), pl.BlockSpec((ds_, d), const),
                  pl.BlockSpec((tm, d), row), _mod_spec(layer, G_F, tm, seq),
                  pl.BlockSpec((1, d), const), pl.BlockSpec((1, d), const),
                  pl.BlockSpec(memory_space=pl.ANY)],
        out_specs=pl.BlockSpec((tm, d), row),
        out_shape=jax.ShapeDtypeStruct((n_rows, d), F32),
        scratch_shapes=[pltpu.VMEM((TOP_K, tm * SUBLANES, LANES), jnp.uint32),
                        pltpu.VMEM((TOP_K, tm * SUBLANES, LANES), jnp.uint32), pltpu.SemaphoreType.DMA((2,))],
        compiler_params=_params(("arbitrary",)),
        name="ffn_out",
    )(dest, dest, f_in, gates, ws_gate, ws_up, ws_down, h, mods, ln_g.reshape(1, d), ln_b.reshape(1, d), y)


def _moe_and_norm(h, f_in, scores, mods, router_bias, we_gate, we_up, we_down, ws_gate, ws_up, ws_down,
                  ln_g, ln_b, n_rows, seq, layer):
    e_k, r_k, gates, counts = _router(scores, router_bias, n_rows)
    dest, blk_e, n_valid, pad_info, chain, n_blocks = _layout(e_k, r_k, counts, EXPERT_TILE)
    xs = _dispatch(f_in, dest, pad_info, n_rows, n_blocks)
    y = _expert_ffn(xs, blk_e, n_valid, chain, we_gate, we_up, we_down, layer)
    return _ffn_out(f_in, y, dest, gates, ws_gate.astype(BF16), ws_up.astype(BF16), ws_down.astype(BF16),
                    h, mods, layer, ln_g, ln_b, n_rows, seq)


def kernel(x, c, ctx, c_ctx, mod_w, mod_b, ln_mix_g, ln_mix_b, ln_ffn_g, ln_ffn_b, na_w_qkv, na_rpb, na_w_o,
           cv_w_pw1, cv_b_pw1, cv_w_dw, cv_b_dw, cv_ln_g, cv_ln_b, cv_w_pw2, cv_b_pw2, moe_w_router,
           moe_router_bias, moe_w_gate, moe_w_up, moe_w_down, sh_w_gate, sh_w_up, sh_w_down):
    n_batch, seq, d = x.shape
    ctx_len = ctx.shape[1]
    n_lat = n_batch * seq
    n_tok = n_lat + n_batch * ctx_len

    cvec = jnp.zeros((MOD_ROWS, d), F32).at[:n_batch].set(c).at[n_batch].set(c_ctx)
    mods = _mod_matmul(cvec, mod_w, mod_b).reshape(DEPTH * MOD_ROWS * MOD_PIECES, 1, d)
    x_lat = x.reshape(n_lat, d)
    x_ctx = ctx.reshape(n_batch * ctx_len, d)

    qkv = _mod_matmul_qkv(x_lat, x_ctx, mods, 0, na_w_qkv[0].astype(BF16), seq)
    attn = _na_attention(qkv, _na_bias_table(na_rpb[0]), n_batch, seq, ctx_len, n_tok)
    attn = _ctx_attention(qkv, attn, n_batch, seq, ctx_len)
    h, f_in, scores = _proj_ln(attn, na_w_o[0].astype(BF16), None, x_lat, x_ctx, mods, 0, ln_mix_g[0], ln_mix_b[0],
                               moe_w_router[0].astype(BF16), n_tok, seq)
    h = _moe_and_norm(h, f_in, scores, mods, moe_router_bias[0], moe_w_gate, moe_w_up, moe_w_down,
                      sh_w_gate[0], sh_w_up[0], sh_w_down[0], ln_ffn_g[0], ln_ffn_b[0], n_tok, seq, 0)

    glu = _mod_matmul_glu(h, mods, 1, cv_w_pw1[0].astype(BF16), cv_b_pw1[0], n_lat, seq)
    conv = _dwconv(glu, cv_w_dw[0], cv_b_dw[0], n_batch, seq)
    h, f_in, scores = _proj_ln(conv, cv_w_pw2[0].astype(BF16), cv_b_pw2[0], h, None, mods, 1, ln_mix_g[1],
                               ln_mix_b[1], moe_w_router[1].astype(BF16), n_lat, seq,
                               pre_ln=(cv_ln_g[0], cv_ln_b[0]))
    h = _moe_and_norm(h, f_in, scores, mods, moe_router_bias[1], moe_w_gate, moe_w_up, moe_w_down,
                      sh_w_gate[1], sh_w_up[1], sh_w_down[1], ln_ffn_g[1], ln_ffn_b[1], n_lat, seq, 1)
    return h.reshape(n_batch, seq, d)
```

```python
import functools

import jax
import jax.numpy as jnp
from jax import lax
from jax.experimental import pallas as pl
from jax.experimental.pallas import tpu as pltpu

D_MODEL = 2048
GRID_W = 64
N_HEADS = 16
HEAD_DIM = D_MODEL // N_HEADS
WIN_ROWS = 8
WIN_COLS = 16
CONV_WIDTH = 31
N_EXPERTS = 64
TOP_K = 8
ROUTED_SCALE = 2.5
LN_EPS = 1e-6
SUBLANES = 8
NEG_INF = -1e30
DEPTH = 2
DEEPNORM_ALPHA = (2 * DEPTH) ** 0.25

F32 = jnp.float32
BF16 = jnp.bfloat16

VMEM_LIMIT_BYTES = 56 * 1024 * 1024
DISPATCH_TILE = 512
PROJ_TILE = 512
PROJ_SPLIT = 2
ROUTER_TILE = 512
MM_ROW_TILE = 512
MOD_COL_TILE = 2048
QKV_COL_TILE = 3072
GLU_COL_TILE = 2048
EXPERT_TILE = 512
COMBINE_TILE = 256
ROWS_PER_STEP = 64
NA_BLOCK_ROWS = 2
NA_KEY_ROWS = WIN_ROWS + NA_BLOCK_ROWS
CONV_CH = 256
CONV_TT = 128


def _params(sem):
    return pltpu.CompilerParams(dimension_semantics=sem, vmem_limit_bytes=VMEM_LIMIT_BYTES)


def _layer_norm(z, g, b):
    mu = jnp.mean(z, axis=-1, keepdims=True)
    zc = z - mu
    var = jnp.mean(zc * zc, axis=-1, keepdims=True)
    return zc * lax.rsqrt(var + LN_EPS) * g + b


def _silu(x):
    return x * jax.nn.sigmoid(x)


LANES = 128
ROW_WORDS = D_MODEL // 2
assert ROW_WORDS == SUBLANES * LANES


def _store_packed_rows(ref, x):
    rows = x.shape[0]
    for s_ in range(SUBLANES):
        base = 2 * LANES * s_
        lo = pltpu.bitcast(x[:, base:base + LANES].astype(BF16).astype(F32), jnp.uint32)
        hi = pltpu.bitcast(x[:, base + LANES:base + 2 * LANES].astype(BF16).astype(F32), jnp.uint32)
        ref[pl.ds(s_, rows, stride=SUBLANES), :] = (lo >> 16) | (hi & jnp.uint32(0xFFFF0000))


def _load_packed_group(ref, s_):
    w = ref[pl.ds(s_, ref.shape[0] // SUBLANES, stride=SUBLANES), :]
    return pltpu.bitcast(w << 16, F32), pltpu.bitcast(w & jnp.uint32(0xFFFF0000), F32)


def _row_tile(ref, t, n=1):
    return ref.at[pl.ds(pl.multiple_of(t * SUBLANES, SUBLANES), n * SUBLANES)]


def _load_packed_rows_bf16(ref):
    parts = []
    for s_ in range(SUBLANES):
        lo, hi = _load_packed_group(ref, s_)
        parts += [lo.astype(BF16), hi.astype(BF16)]
    return jnp.concatenate(parts, axis=1)


def _mod_kernel(c_ref, w_ref, b_ref, o_ref):
    a = _silu(c_ref[...]).astype(BF16)
    o_ref[0] = jnp.dot(a, w_ref[0].astype(BF16), preferred_element_type=F32) + b_ref[0]


def _mod_matmul(cvec, w, b):
    m, k = cvec.shape
    layers, _, n = w.shape
    tn = MOD_COL_TILE
    return pl.pallas_call(
        _mod_kernel,
        grid=(layers, n // tn),
        in_specs=[pl.BlockSpec((m, k), lambda l, j: (0, 0)),
                  pl.BlockSpec((1, k, tn), lambda l, j: (l, 0, j)),
                  pl.BlockSpec((1, 1, tn), lambda l, j: (l, 0, j))],
        out_specs=pl.BlockSpec((1, m, tn), lambda l, j: (l, 0, j)),
        out_shape=jax.ShapeDtypeStruct((layers, m, n), F32),
        compiler_params=_params(("parallel", "arbitrary")),
        name="mod_matmul",
    )(cvec, w, b.reshape(layers, 1, n))


def _modmm_kernel(x_ref, c_ref, sc_ref, sh_ref, w_ref, o_ref, a_scr, *, n_lat_tiles):
    @pl.when(pl.program_id(1) == 0)
    def _():
        h = jnp.where(pl.program_id(0) < n_lat_tiles, x_ref[...], c_ref[...])
        a_scr[...] = (h * (1.0 + sc_ref[0]) + sh_ref[0]).astype(BF16)

    o_ref[...] = jnp.dot(a_scr[...], w_ref[...], preferred_element_type=F32).astype(o_ref.dtype)


def _modglu_kernel(h_ref, sc_ref, sh_ref, wa_ref, wg_ref, ba_ref, bg_ref, o_ref, a_scr):
    @pl.when(pl.program_id(1) == 0)
    def _():
        a_scr[...] = (h_ref[...] * (1.0 + sc_ref[0]) + sh_ref[0]).astype(BF16)

    a = a_scr[...]
    va = jnp.dot(a, wa_ref[...], preferred_element_type=F32) + ba_ref[...]
    vg = jnp.dot(a, wg_ref[...], preferred_element_type=F32) + bg_ref[...]
    o_ref[...] = va * jax.nn.sigmoid(vg)


MOD_ROWS = 8
MOD_PIECES = 6
SH_A, SC_A, G_A, SH_F, SC_F, G_F = range(MOD_PIECES)


def _mod_spec(layer, piece, tile_rows, seq):
    def index(i, *_):
        return ((layer * MOD_ROWS + (i * tile_rows) // seq) * MOD_PIECES + piece, 0, 0)

    return pl.BlockSpec((1, 1, D_MODEL), index)


def _mod_matmul_qkv(x_lat, x_ctx, mods, layer, w, seq):
    n_lat, k = x_lat.shape
    n_rows = n_lat + x_ctx.shape[0]
    n = w.shape[1]
    tm, tn = MM_ROW_TILE, QKV_COL_TILE
    n_lat_tiles = n_lat // tm
    return pl.pallas_call(
        functools.partial(_modmm_kernel, n_lat_tiles=n_lat_tiles),
        grid=(n_rows // tm, n // tn),
        in_specs=[pl.BlockSpec((tm, k), lambda i, j: (jnp.minimum(i, n_lat_tiles - 1), 0)),
                  pl.BlockSpec((tm, k), lambda i, j: (jnp.maximum(i - n_lat_tiles, 0), 0)),
                  _mod_spec(layer, SC_A, tm, seq), _mod_spec(layer, SH_A, tm, seq),
                  pl.BlockSpec((k, tn), lambda i, j: (0, j))],
        out_specs=pl.BlockSpec((tm, tn), lambda i, j: (i, j)),
        out_shape=jax.ShapeDtypeStruct((n_rows, n), BF16),
        scratch_shapes=[pltpu.VMEM((tm, k), BF16)],
        compiler_params=_params(("parallel", "arbitrary")),
        name="mod_qkv",
    )(x_lat, x_ctx, mods, mods, w)


def _mod_matmul_glu(h, mods, layer, w, b, n_rows, seq):
    k = h.shape[1]
    n = w.shape[1] // 2
    tm, tn = MM_ROW_TILE, GLU_COL_TILE
    nj = n // tn
    b2 = b.reshape(1, 2 * n)
    return pl.pallas_call(
        _modglu_kernel,
        grid=(n_rows // tm, nj),
        in_specs=[pl.BlockSpec((tm, k), lambda i, j: (i, 0)),
                  _mod_spec(layer, SC_A, tm, seq), _mod_spec(layer, SH_A, tm, seq),
                  pl.BlockSpec((k, tn), lambda i, j: (0, j)),
                  pl.BlockSpec((k, tn), lambda i, j: (0, j + nj)),
                  pl.BlockSpec((1, tn), lambda i, j: (0, j)),
                  pl.BlockSpec((1, tn), lambda i, j: (0, j + nj))],
        out_specs=pl.BlockSpec((tm, tn), lambda i, j: (i, j)),
        out_shape=jax.ShapeDtypeStruct((n_rows, n), F32),
        scratch_shapes=[pltpu.VMEM((tm, k), BF16)],
        compiler_params=_params(("parallel", "arbitrary")),
        name="mod_pw1_glu",
    )(h, mods, mods, w, w, b2, b2)


def _na_block_window(r0, rows):
    return jnp.clip(r0 - WIN_ROWS // 2, 0, rows - NA_KEY_ROWS)


def _na_block_kinds(rows):
    def geometry(r0):
        k0 = min(max(r0 - WIN_ROWS // 2, 0), rows - NA_KEY_ROWS)
        out = []
        for r in range(r0, r0 + NA_BLOCK_ROWS):
            rs = min(max(r - WIN_ROWS // 2, 0), rows - WIN_ROWS)
            assert 0 <= rs - k0 <= NA_KEY_ROWS - WIN_ROWS
            out.append((rs - k0, r - rs))
        return tuple(out)

    kinds, kind_of = [], {}
    for r0 in range(0, rows, NA_BLOCK_ROWS):
        geo = geometry(r0)
        if geo not in kinds:
            kinds.append(geo)
        kind_of[r0] = kinds.index(geo)
    return kinds, kind_of


def _na_kernel(q_ref, k_ref, v_ref, kc_ref, vc_ref, bias_ref, o_ref, *, rows):
    scale = HEAD_DIM ** -0.5
    contract_last = (((1,), (1,)), ((), ()))
    row0 = pl.program_id(2) * ROWS_PER_STEP
    n_keys = NA_KEY_ROWS * GRID_W
    bq = NA_BLOCK_ROWS * GRID_W
    n_blk = ROWS_PER_STEP // NA_BLOCK_ROWS
    _, kind_of = _na_block_kinds(rows)
    kc = kc_ref[...]
    vc = vc_ref[...]

    def block_kind(r0):
        common = max(set(kind_of.values()), key=list(kind_of.values()).count)
        kind = jnp.int32(common)
        for r_static, kd in kind_of.items():
            if kd != common:
                kind = jnp.where(r0 == r_static, kd, kind)
        return kind

    def key_start(b):
        return pl.multiple_of(_na_block_window(row0 + b * NA_BLOCK_ROWS, rows) * GRID_W, GRID_W)

    def scores(b):
        q = q_ref[b * bq:(b + 1) * bq, :]
        kw = k_ref[pl.ds(key_start(b), n_keys), :]
        bias = bias_ref[0, block_kind(row0 + b * NA_BLOCK_ROWS)]
        s_loc = lax.dot_general(q, kw, contract_last, preferred_element_type=F32) * scale + bias
        s_ctx = lax.dot_general(q, kc, contract_last, preferred_element_type=F32) * scale
        return s_loc, s_ctx

    def finish(b, s_loc, s_ctx):
        vw = v_ref[pl.ds(key_start(b), n_keys), :]
        m = jnp.maximum(jnp.max(s_loc, axis=-1, keepdims=True), jnp.max(s_ctx, axis=-1, keepdims=True))
        p_loc = jnp.exp(s_loc - m)
        p_ctx = jnp.exp(s_ctx - m)
        denom = jnp.sum(p_loc, axis=-1, keepdims=True) + jnp.sum(p_ctx, axis=-1, keepdims=True)
        o = (jnp.dot(p_loc.astype(BF16), vw, preferred_element_type=F32)
             + jnp.dot(p_ctx.astype(BF16), vc, preferred_element_type=F32))
        o_ref[b * bq:(b + 1) * bq, :] = (o / denom).astype(o_ref.dtype)

    s_next = scores(0)
    for b in range(n_blk):
        s_cur = s_next
        if b + 1 < n_blk:
            s_next = scores(b + 1)
        finish(b, *s_cur)


def _na_block_bias(row_table, rows):
    n_heads = row_table.shape[0]
    kinds, _ = _na_block_kinds(rows)
    per_kind = []
    for geo in kinds:
        per_row = []
        for rel, off in geo:
            before = jnp.full((n_heads, GRID_W, rel * GRID_W), NEG_INF, F32)
            after = jnp.full((n_heads, GRID_W, (NA_KEY_ROWS - WIN_ROWS - rel) * GRID_W), NEG_INF, F32)
            per_row.append(jnp.concatenate([before, row_table[:, off], after], axis=-1))
        per_kind.append(jnp.concatenate(per_row, axis=1))
    return jnp.stack(per_kind, axis=1)


def _na_bias_table(rpb):
    col = jnp.arange(GRID_W)
    cstart = jnp.clip(col - WIN_COLS // 2, 0, GRID_W - WIN_COLS)
    col_ok = (col[None, :] >= cstart[:, None]) & (col[None, :] < cstart[:, None] + WIN_COLS)
    cidx = jnp.clip(col[None, :] - col[:, None] + WIN_COLS - 1, 0, 2 * WIN_COLS - 2)
    onehot = (cidx[:, :, None] == jnp.arange(2 * WIN_COLS - 1)).astype(F32)
    t = jnp.einsum('hrc,qkc->hrqk', rpb.astype(F32), onehot, precision=lax.Precision.HIGHEST)
    t = jnp.where(col_ok[None, None], t, NEG_INF)
    t = jnp.stack([t[:, WIN_ROWS - 1 - off:2 * WIN_ROWS - 1 - off] for off in range(WIN_ROWS)], axis=1)
    t = t.transpose(0, 1, 3, 2, 4)
    return t.reshape(rpb.shape[0], WIN_ROWS, GRID_W, WIN_ROWS * GRID_W)


def _na_attention(qkv, bias_table, n_batch, seq, ctx_len, n_rows_out):
    rows = seq // GRID_W
    groups = rows // ROWS_PER_STEP
    tq = ROWS_PER_STEP * GRID_W
    ctx_blk0 = (n_batch * seq) // ctx_len
    h_ = N_HEADS
    block_bias = _na_block_bias(bias_table, rows)
    n_kinds = block_bias.shape[1]
    return pl.pallas_call(
        functools.partial(_na_kernel, rows=rows),
        grid=(n_batch, h_, groups),
        in_specs=[pl.BlockSpec((tq, HEAD_DIM), lambda b, h, g: (b * groups + g, h)),
                  pl.BlockSpec((seq, HEAD_DIM), lambda b, h, g: (b, h_ + h)),
                  pl.BlockSpec((seq, HEAD_DIM), lambda b, h, g: (b, 2 * h_ + h)),
                  pl.BlockSpec((ctx_len, HEAD_DIM), lambda b, h, g: (ctx_blk0 + b, h_ + h)),
                  pl.BlockSpec((ctx_len, HEAD_DIM), lambda b, h, g: (ctx_blk0 + b, 2 * h_ + h)),
                  pl.BlockSpec((1, n_kinds, NA_BLOCK_ROWS * GRID_W, NA_KEY_ROWS * GRID_W),
                               lambda b, h, g: (h, 0, 0, 0))],
        out_specs=pl.BlockSpec((tq, HEAD_DIM), lambda b, h, g: (b * groups + g, h)),
        out_shape=jax.ShapeDtypeStruct((n_rows_out, D_MODEL), BF16),
        compiler_params=_params(("parallel", "parallel", "arbitrary")),
        name="na_attention",
    )(qkv, qkv, qkv, qkv, qkv, block_bias)


def _ctx_attn_kernel(q_ref, k_ref, v_ref, prev_ref, o_ref):
    del prev_ref
    scale = HEAD_DIM ** -0.5
    for h in range(N_HEADS):
        cols = slice(h * HEAD_DIM, (h + 1) * HEAD_DIM)
        s = lax.dot_general(q_ref[:, cols], k_ref[:, cols], (((1,), (1,)), ((), ())),
                            preferred_element_type=F32) * scale
        m = jnp.max(s, axis=-1, keepdims=True)
        p = jnp.exp(s - m)
        denom = jnp.sum(p, axis=-1, keepdims=True)
        o = jnp.dot(p.astype(BF16), v_ref[:, cols], preferred_element_type=F32)
        o_ref[:, cols] = (o / denom).astype(o_ref.dtype)


def _ctx_attention(qkv, attn_out, n_batch, seq, ctx_len):
    ctx_blk0 = (n_batch * seq) // ctx_len
    d = D_MODEL
    return pl.pallas_call(
        _ctx_attn_kernel,
        grid=(n_batch,),
        in_specs=[pl.BlockSpec((ctx_len, d), lambda b: (ctx_blk0 + b, 0)),
                  pl.BlockSpec((ctx_len, d), lambda b: (ctx_blk0 + b, 1)),
                  pl.BlockSpec((ctx_len, d), lambda b: (ctx_blk0 + b, 2)),
                  pl.BlockSpec(memory_space=pl.ANY)],
        out_specs=pl.BlockSpec((ctx_len, d), lambda b: (ctx_blk0 + b, 0)),
        out_shape=jax.ShapeDtypeStruct(attn_out.shape, attn_out.dtype),
        input_output_aliases={3: 0},
        compiler_params=_params(("parallel",)),
        name="ctx_attention",
    )(qkv, qkv, qkv, attn_out)


def _dwconv_kernel(x_ref, w_ref, b_ref, o_ref, xpad, *, seq):
    half = CONV_WIDTH // 2
    lead = 2 * SUBLANES
    ch = x_ref.shape[1]
    xpad[pl.ds(0, lead), :] = jnp.zeros((lead, ch), F32)
    xpad[pl.ds(lead + seq, lead), :] = jnp.zeros((lead, ch), F32)
    xpad[pl.ds(lead, seq), :] = x_ref[...]
    w = w_ref[...]
    bias = b_ref[...]
    win_rows = CONV_TT + 2 * lead

    def body(c, carry):
        t0 = pl.multiple_of(c * CONV_TT, CONV_TT)
        win = xpad[pl.ds(t0, win_rows), :]
        acc = jnp.broadcast_to(bias, (CONV_TT, ch))
        for p in range(SUBLANES):
            shifted = win if p == 0 else pltpu.roll(win, win_rows - p, axis=0)
            for k in range(CONV_WIDTH):
                o = lead - half + k
                if o % SUBLANES == p:
                    acc = acc + shifted[o - p:o - p + CONV_TT, :] * w[k:k + 1, :]
        o_ref[pl.ds(t0, CONV_TT), :] = acc
        return carry

    lax.fori_loop(0, seq // CONV_TT, body, 0)


def _dwconv(x, w_dw, b_dw, n_batch, seq):
    d = x.shape[1]
    nc = d // CONV_CH
    return pl.pallas_call(
        functools.partial(_dwconv_kernel, seq=seq),
        grid=(n_batch, nc),
        in_specs=[pl.BlockSpec((seq, CONV_CH), lambda b, c: (b, c)),
                  pl.BlockSpec((CONV_WIDTH, CONV_CH), lambda b, c: (0, c)),
                  pl.BlockSpec((1, CONV_CH), lambda b, c: (0, c))],
        out_specs=pl.BlockSpec((seq, CONV_CH), lambda b, c: (b, c)),
        out_shape=jax.ShapeDtypeStruct((n_batch * seq, d), F32),
        scratch_shapes=[pltpu.VMEM((seq + 4 * SUBLANES, CONV_CH), F32)],
        compiler_params=_params(("parallel", "parallel")),
        name="dwconv",
    )(x, w_dw, b_dw.reshape(1, d))


def _proj_ln_kernel(*refs, pre_ln, has_bias, n_lat_tiles):
    it = iter(refs)
    a_ref = next(it)
    pre_g = next(it) if pre_ln else None
    pre_b = next(it) if pre_ln else None
    w_ref = next(it)
    bias_ref = next(it) if has_bias else None
    h_ref = next(it)
    hc_ref = next(it) if n_lat_tiles is not None else None
    gate_ref, lng_ref, lnb_ref, scf_ref, shf_ref, wr_ref = (next(it) for _ in range(6))
    hnew_ref, fin_ref, score_ref = (next(it) for _ in range(3))

    tm = a_ref.shape[0]
    half = tm // PROJ_SPLIT

    def project(p):
        a = a_ref[p * half:(p + 1) * half, :]
        if pre_ln:
            a = _silu(_layer_norm(a, pre_g[...], pre_b[...]))
        return jnp.dot(a.astype(BF16), w_ref[...], preferred_element_type=F32)

    def finish(p, m):
        rows = slice(p * half, (p + 1) * half)
        if has_bias:
            m = m + bias_ref[...]
        h_old = h_ref[rows, :]
        if n_lat_tiles is not None:
            h_old = jnp.where(pl.program_id(0) < n_lat_tiles, h_old, hc_ref[rows, :])
        hn = _layer_norm(DEEPNORM_ALPHA * h_old + gate_ref[0] * m, lng_ref[...], lnb_ref[...])
        hnew_ref[rows, :] = hn
        f = hn * (1.0 + scf_ref[0]) + shf_ref[0]
        _store_packed_rows(fin_ref.at[pl.ds(p * half * SUBLANES, half * SUBLANES)], f)
        score_ref[rows, :] = jax.nn.sigmoid(jnp.dot(f.astype(BF16), wr_ref[...], preferred_element_type=F32))

    m_next = project(0)
    for p in range(PROJ_SPLIT):
        m_cur = m_next
        if p + 1 < PROJ_SPLIT:
            m_next = project(p + 1)
        finish(p, m_cur)


def _proj_ln(a, w, bias, h, h_ctx, mods, layer, ln_g, ln_b, w_router, n_rows, seq, pre_ln=None):
    d = D_MODEL
    tm = PROJ_TILE
    row = lambda i: (i, 0)
    const = lambda i: (0, 0)
    n_lat_tiles = None if h_ctx is None else h.shape[0] // tm
    args, specs = [a], [pl.BlockSpec((tm, d), row)]
    if pre_ln is not None:
        args += [pre_ln[0].reshape(1, d), pre_ln[1].reshape(1, d)]
        specs += [pl.BlockSpec((1, d), const)] * 2
    args.append(w)
    specs.append(pl.BlockSpec((d, d), const))
    if bias is not None:
        args.append(bias.reshape(1, d))
        specs.append(pl.BlockSpec((1, d), const))
    if h_ctx is None:
        args.append(h)
        specs.append(pl.BlockSpec((tm, d), row))
    else:
        args += [h, h_ctx]
        specs += [pl.BlockSpec((tm, d), lambda i: (jnp.minimum(i, n_lat_tiles - 1), 0)),
                  pl.BlockSpec((tm, d), lambda i: (jnp.maximum(i - n_lat_tiles, 0), 0))]
    args += [mods, ln_g.reshape(1, d), ln_b.reshape(1, d), mods, mods, w_router]
    specs += [_mod_spec(layer, G_A, tm, seq), pl.BlockSpec((1, d), const), pl.BlockSpec((1, d), const),
              _mod_spec(layer, SC_F, tm, seq), _mod_spec(layer, SH_F, tm, seq),
              pl.BlockSpec((d, N_EXPERTS), const)]
    return pl.pallas_call(
        functools.partial(_proj_ln_kernel, pre_ln=pre_ln is not None, has_bias=bias is not None,
                          n_lat_tiles=n_lat_tiles),
        grid=(n_rows // tm,),
        in_specs=specs,
        out_specs=[pl.BlockSpec((tm, d), row), pl.BlockSpec((tm * SUBLANES, LANES), row),
                   pl.BlockSpec((tm, N_EXPERTS), row)],
        out_shape=[jax.ShapeDtypeStruct((n_rows, d), F32),
                   jax.ShapeDtypeStruct((n_rows * SUBLANES, LANES), jnp.uint32),
                   jax.ShapeDtypeStruct((n_rows, N_EXPERTS), F32)],
        compiler_params=_params(("parallel",)),
        name="proj_ln",
    )(*args)


def _router_kernel(s_ref, b_ref, ek_ref, rk_ref, gk_ref, cnt_ref, carry):
    tm, e = s_ref.shape

    @pl.when(pl.program_id(0) == 0)
    def _():
        carry[...] = jnp.zeros_like(carry)

    s = s_ref[...]
    lane = lax.broadcasted_iota(jnp.int32, (tm, e), 1).astype(F32)
    slot = lax.broadcasted_iota(jnp.int32, (tm, LANES), 1)
    work = s + b_ref[...]
    sel = jnp.zeros((tm, e), F32)
    e_acc = jnp.zeros((tm, LANES), F32)
    g_acc = jnp.zeros((tm, LANES), F32)
    g_sum = jnp.zeros((tm, 1), F32)
    picks = []
    for k in range(TOP_K):
        m = jnp.max(work, axis=-1, keepdims=True)
        idx = jnp.min(jnp.where(work == m, lane, float(e)), axis=-1, keepdims=True)
        onehot = lane == idx
        gk = jnp.sum(jnp.where(onehot, s, 0.0), axis=-1, keepdims=True)
        sel = jnp.where(onehot, 1.0, sel)
        work = jnp.where(onehot, -jnp.inf, work)
        e_acc = jnp.where(slot == k, idx, e_acc)
        g_acc = jnp.where(slot == k, gk, g_acc)
        g_sum = g_sum + gk
        picks.append(idx)

    r_i = lax.broadcasted_iota(jnp.int32, (tm, tm), 0)
    c_i = lax.broadcasted_iota(jnp.int32, (tm, tm), 1)
    tri = jnp.where(c_i < r_i, 1.0, 0.0).astype(BF16)
    rank = jnp.dot(tri, sel.astype(BF16), preferred_element_type=F32) + carry[...]
    r_acc = jnp.zeros((tm, LANES), F32)
    for k in range(TOP_K):
        rk = jnp.sum(jnp.where(lane == picks[k], rank, 0.0), axis=-1, keepdims=True)
        r_acc = jnp.where(slot == k, rk, r_acc)

    carry[...] = carry[...] + jnp.sum(sel, axis=0, keepdims=True)
    ek_ref[...] = e_acc[:, :TOP_K].astype(jnp.int32)
    rk_ref[...] = r_acc[:, :TOP_K].astype(jnp.int32)
    gk_ref[...] = (g_acc / g_sum * ROUTED_SCALE)[:, :TOP_K]
    cnt_ref[...] = carry[...].astype(jnp.int32)


def _router(scores, router_bias, n_rows):
    e = N_EXPERTS
    tm = ROUTER_TILE
    row = lambda i: (i, 0)
    return pl.pallas_call(
        _router_kernel,
        grid=(n_rows // tm,),
        in_specs=[pl.BlockSpec((tm, e), row), pl.BlockSpec((1, e), lambda i: (0, 0))],
        out_specs=[pl.BlockSpec((tm, TOP_K), row), pl.BlockSpec((tm, TOP_K), row), pl.BlockSpec((tm, TOP_K), row),
                   pl.BlockSpec((1, e), lambda i: (0, 0))],
        out_shape=[jax.ShapeDtypeStruct((n_rows, TOP_K), jnp.int32), jax.ShapeDtypeStruct((n_rows, TOP_K), jnp.int32),
                   jax.ShapeDtypeStruct((n_rows, TOP_K), F32), jax.ShapeDtypeStruct((1, e), jnp.int32)],
        scratch_shapes=[pltpu.VMEM((1, e), F32)],
        compiler_params=_params(("arbitrary",)),
        name="router",
    )(scores, router_bias.reshape(1, e).astype(F32))


def _layout(e_k, r_k, counts, tm_e):
    e = N_EXPERTS
    n_assign = e_k.size
    counts = counts.reshape(e)
    padded = (counts + tm_e - 1) // tm_e * tm_e
    pends = jnp.cumsum(padded)
    pstarts = pends - padded
    experts = jnp.arange(e, dtype=jnp.int32)
    start_of = jnp.sum(jnp.where(e_k[..., None] == experts, pstarts, 0), axis=-1)
    dest = (start_of + r_k).reshape(n_assign).astype(jnp.int32)
    n_blocks = (n_assign + e * (tm_e - 1) + tm_e - 1) // tm_e
    blk_row0 = jnp.arange(n_blocks, dtype=jnp.int32) * tm_e
    blk_e = jnp.minimum(jnp.sum(pends[None, :] <= blk_row0[:, None], axis=-1), e - 1).astype(jnp.int32)
    n_valid = (pends[-1] // tm_e).astype(jnp.int32).reshape(1)
    pad_info = jnp.concatenate([pstarts + counts, padded - counts]).astype(jnp.int32)
    later = (experts[None, :] > experts[:, None]) & (counts[None, :] > 0)
    next_e = jnp.min(jnp.where(later, experts[None, :], e), axis=1)
    next_e = jnp.where(next_e == e, -1, next_e)
    order_e = jnp.sum((experts[None, :] < experts[:, None]) & (counts[None, :] > 0), axis=1)
    chain = jnp.concatenate([next_e, order_e]).astype(jnp.int32)
    return dest, blk_e, n_valid, pad_info, chain, n_blocks


def _dispatch_kernel(pad_ref, dest_ref, f_ref, xs_ref, zbuf, sem, zsem):
    tm = f_ref.shape[0] // SUBLANES
    pad_bits = EXPERT_TILE.bit_length() - 1

    def zero_copies(e_idx):
        start = pad_ref[e_idx]
        pad = pad_ref[N_EXPERTS + e_idx]
        return [(((pad >> b) & 1) == 1, pltpu.make_async_copy(
            _row_tile(zbuf, 0, 1 << b), _row_tile(xs_ref, start + (pad & ((1 << b) - 1)), 1 << b), zsem))
            for b in range(pad_bits)]

    @pl.when(pl.program_id(0) == 0)
    def _():
        zbuf[...] = jnp.zeros_like(zbuf)

        def start_e(e_idx, carry):
            for cond, cp in zero_copies(e_idx):
                pl.when(cond)(cp.start)
            return carry

        def wait_e(e_idx, carry):
            for cond, cp in zero_copies(e_idx):
                pl.when(cond)(cp.wait)
            return carry

        lax.fori_loop(0, N_EXPERTS, start_e, 0)
        lax.fori_loop(0, N_EXPERTS, wait_e, 0)

    def tok(t, carry):
        for k in range(TOP_K):
            d = dest_ref[t * TOP_K + k]
            pltpu.make_async_copy(_row_tile(f_ref, t), _row_tile(xs_ref, d), sem).start(priority=k % 2)
        return carry

    lax.fori_loop(0, tm, tok, 0)
    for k in range(TOP_K):
        pltpu.make_async_copy(f_ref, _row_tile(xs_ref, 0, tm), sem).wait()


def _dispatch(f_in, dest, pad_info, n_rows, n_blocks):
    tm = DISPATCH_TILE
    grid_spec = pltpu.PrefetchScalarGridSpec(
        num_scalar_prefetch=1,
        grid=(n_rows // tm,),
        in_specs=[pl.BlockSpec((tm * TOP_K,), lambda i, p: (i,), memory_space=pltpu.SMEM),
                  pl.BlockSpec((tm * SUBLANES, LANES), lambda i, p: (i, 0))],
        out_specs=pl.BlockSpec(memory_space=pl.ANY),
        scratch_shapes=[pltpu.VMEM((EXPERT_TILE // 2 * SUBLANES, LANES), f_in.dtype), pltpu.SemaphoreType.DMA,
                        pltpu.SemaphoreType.DMA],
    )
    return pl.pallas_call(
        _dispatch_kernel,
        grid_spec=grid_spec,
        out_shape=jax.ShapeDtypeStruct((n_blocks * EXPERT_TILE * SUBLANES, LANES), f_in.dtype),
        compiler_params=_params(("arbitrary",)),
        name="dispatch",
    )(pad_info, dest, f_in)


def _swiglu_hidden(x, wg_ref, wu_ref):
    gv = jnp.dot(x, wg_ref[...], preferred_element_type=F32)
    uv = jnp.dot(x, wu_ref[...], preferred_element_type=F32)
    return (_silu(gv) * uv).astype(BF16)


def _expert_kernel(be_ref, nv_ref, chain_ref, x_ref, wg_hbm, wu_hbm, wd_hbm, y_ref,
                   wg32, wu32, wd32, wgb, wub, wdb, sems, *, layer):
    i = pl.program_id(0)
    valid = i < nv_ref[0]
    e_cur = be_ref[i]
    first = jnp.logical_or(i == 0, e_cur != be_ref[jnp.maximum(i - 1, 0)])
    d, de = wgb.shape
    n_chunks = 8

    def weight_copies(e_idx, slot):
        return [pltpu.make_async_copy(w_hbm.at[layer, e_idx], buf.at[slot], sems.at[slot, j])
                for j, (w_hbm, buf) in enumerate(((wg_hbm, wg32), (wu_hbm, wu32), (wd_hbm, wd32)))]

    @pl.when(jnp.logical_and(valid, first))
    def _():
        slot = chain_ref[N_EXPERTS + e_cur] & 1
        e_next = chain_ref[e_cur]

        @pl.when(i == 0)
        def _():
            for cp in weight_copies(e_cur, slot):
                cp.start()

        for cp in weight_copies(e_cur, slot):
            cp.wait()

        @pl.when(e_next >= 0)
        def _():
            for cp in weight_copies(e_next, 1 - slot):
                cp.start()

        def cast_chunk(c, carry):
            r0 = pl.multiple_of(c * (d // n_chunks), d // n_chunks)
            wgb[pl.ds(r0, d // n_chunks), :] = wg32[slot, pl.ds(r0, d // n_chunks), :].astype(BF16)
            wub[pl.ds(r0, d // n_chunks), :] = wu32[slot, pl.ds(r0, d // n_chunks), :].astype(BF16)
            r1 = pl.multiple_of(c * (de // n_chunks), de // n_chunks)
            wdb[pl.ds(r1, de // n_chunks), :] = wd32[slot, pl.ds(r1, de // n_chunks), :].astype(BF16)
            return carry

        lax.fori_loop(0, n_chunks, cast_chunk, 0)

    @pl.when(valid)
    def _():
        act = _swiglu_hidden(_load_packed_rows_bf16(x_ref), wgb, wub)
        _store_packed_rows(y_ref, jnp.dot(act, wdb[...], preferred_element_type=F32))


def _expert_ffn(xs, blk_e, n_valid, chain, w_gate, w_up, w_down, layer):
    n_rows = xs.shape[0] // SUBLANES
    d, de = w_gate.shape[2], w_gate.shape[3]
    tm = EXPERT_TILE
    n_blocks = n_rows // tm

    def xmap(i, be, nv, ch):
        return (jnp.minimum(i, nv[0] - 1), 0)

    grid_spec = pltpu.PrefetchScalarGridSpec(
        num_scalar_prefetch=3,
        grid=(n_blocks,),
        in_specs=[pl.BlockSpec((tm * SUBLANES, LANES), xmap),
                  pl.BlockSpec(memory_space=pl.ANY),
                  pl.BlockSpec(memory_space=pl.ANY),
                  pl.BlockSpec(memory_space=pl.ANY)],
        out_specs=pl.BlockSpec((tm * SUBLANES, LANES), xmap),
        scratch_shapes=[pltpu.VMEM((2, d, de), F32), pltpu.VMEM((2, d, de), F32), pltpu.VMEM((2, de, d), F32),
                        pltpu.VMEM((d, de), BF16), pltpu.VMEM((d, de), BF16), pltpu.VMEM((de, d), BF16),
                        pltpu.SemaphoreType.DMA((2, 3))],
    )
    return pl.pallas_call(
        functools.partial(_expert_kernel, layer=layer),
        grid_spec=grid_spec,
        out_shape=jax.ShapeDtypeStruct((n_rows * SUBLANES, LANES), jnp.uint32),
        compiler_params=_params(("arbitrary",)),
        name="expert_ffn",
    )(blk_e, n_valid, chain, xs, w_gate, w_up, w_down)


def _ffn_out_kernel(dcur_ref, dnext_ref, f_ref, gk_ref, wg_ref, wu_ref, wd_ref, h_ref, gate_ref, lng_ref, lnb_ref,
                    y_ref, o_ref, ybuf_a, ybuf_b, sems):
    i = pl.program_id(0)
    n_steps = pl.num_programs(0)
    tm = f_ref.shape[0] // SUBLANES
    slot = lax.rem(i, 2)
    ybufs = (ybuf_a, ybuf_b)

    def gather_token(dref, s, t):
        for k in range(TOP_K):
            row = dref[t * TOP_K + k]
            pltpu.make_async_copy(_row_tile(y_ref, row), _row_tile(ybufs[s].at[k], t),
                                  sems.at[s]).start(priority=k % 2)

    def wait_gathers(s):
        for k in range(TOP_K):
            pltpu.make_async_copy(_row_tile(y_ref, 0, tm), ybufs[s].at[k], sems.at[s]).wait()

    @pl.when(i == 0)
    def _():
        def tok(t, carry):
            gather_token(dcur_ref, 0, t)
            return carry

        lax.fori_loop(0, tm, tok, 0)

    def step(cur):
        n_pieces = 2 + SUBLANES
        piece = [(p * tm) // n_pieces for p in range(n_pieces + 1)]

        def issue_piece(p):
            for t in range(piece[p], piece[p + 1]):
                gather_token(dnext_ref, 1 - cur, t)

        issue_piece(0)
        act = _swiglu_hidden(_load_packed_rows_bf16(f_ref), wg_ref, wu_ref)
        issue_piece(1)
        f = jnp.dot(act, wd_ref[...], preferred_element_type=F32)

        gk = gk_ref[...]
        wait_gathers(cur)
        gcols = [jnp.broadcast_to(gk[:, k:k + 1], (tm, LANES)) for k in range(TOP_K)]
        parts = []
        for s_ in range(SUBLANES):
            issue_piece(2 + s_)
            r_lo = jnp.zeros((tm, LANES), F32)
            r_hi = jnp.zeros((tm, LANES), F32)
            for k in range(TOP_K):
                y_lo, y_hi = _load_packed_group(ybufs[cur].at[k], s_)
                r_lo = r_lo + gcols[k] * y_lo
                r_hi = r_hi + gcols[k] * y_hi
            parts += [r_lo, r_hi]
        f = f + jnp.concatenate(parts, axis=1)
        o_ref[...] = _layer_norm(DEEPNORM_ALPHA * h_ref[...] + gate_ref[0] * f, lng_ref[...], lnb_ref[...])

        @pl.when(i == n_steps - 1)
        def _():
            wait_gathers(1 - cur)

    for cur in range(2):
        pl.when(slot == cur)(functools.partial(step, cur))


def _ffn_out(f_in, y, dest, gates, ws_gate, ws_up, ws_down, h, mods, layer, ln_g, ln_b, n_rows, seq):
    d = D_MODEL
    ds_ = ws_gate.shape[1]
    tm = COMBINE_TILE
    n_steps = n_rows // tm
    row = lambda i: (i, 0)
    const = lambda i: (0, 0)
    return pl.pallas_call(
        _ffn_out_kernel,
        grid=(n_steps,),
        in_specs=[pl.BlockSpec((tm * TOP_K,), lambda i: (i,), memory_space=pltpu.SMEM),
                  pl.BlockSpec((tm * TOP_K,), lambda i: (jnp.minimum(i + 1, n_steps - 1),), memory_space=pltpu.SMEM),
                  pl.BlockSpec((tm * SUBLANES, LANES), row), pl.BlockSpec((tm, TOP_K), row),
                  pl.BlockSpec((d, ds_), const), pl.BlockSpec((d, ds_), const), pl.BlockSpec((ds_, d), const),
                  pl.BlockSpec((tm, d), row), _mod_spec(layer, G_F, tm, seq),
                  pl.BlockSpec((1, d), const), pl.BlockSpec((1, d), const),
                  pl.BlockSpec(memory_space=pl.ANY)],
        out_specs=pl.BlockSpec((tm, d), row),
        out_shape=jax.ShapeDtypeStruct((n_rows, d), F32),
        scratch_shapes=[pltpu.VMEM((TOP_K, tm * SUBLANES, LANES), jnp.uint32),
                        pltpu.VMEM((TOP_K, tm * SUBLANES, LANES), jnp.uint32), pltpu.SemaphoreType.DMA((2,))],
        compiler_params=_params(("arbitrary",)),
        name="ffn_out",
    )(dest, dest, f_in, gates, ws_gate, ws_up, ws_down, h, mods, ln_g.reshape(1, d), ln_b.reshape(1, d), y)


def _moe_and_norm(h, f_in, scores, mods, router_bias, we_gate, we_up, we_down, ws_gate, ws_up, ws_down,
                  ln_g, ln_b, n_rows, seq, layer):
    e_k, r_k, gates, counts = _router(scores, router_bias, n_rows)
    dest, blk_e, n_valid, pad_info, chain, n_blocks = _layout(e_k, r_k, counts, EXPERT_TILE)
    xs = _dispatch(f_in, dest, pad_info, n_rows, n_blocks)
    y = _expert_ffn(xs, blk_e, n_valid, chain, we_gate, we_up, we_down, layer)
    return _ffn_out(f_in, y, dest, gates, ws_gate.astype(BF16), ws_up.astype(BF16), ws_down.astype(BF16),
                    h, mods, layer, ln_g, ln_b, n_rows, seq)


def kernel(x, c, ctx, c_ctx, mod_w, mod_b, ln_mix_g, ln_mix_b, ln_ffn_g, ln_ffn_b, na_w_qkv, na_rpb, na_w_o,
           cv_w_pw1, cv_b_pw1, cv_w_dw, cv_b_dw, cv_ln_g, cv_ln_b, cv_w_pw2, cv_b_pw2, moe_w_router,
           moe_router_bias, moe_w_gate, moe_w_up, moe_w_down, sh_w_gate, sh_w_up, sh_w_down):
    n_batch, seq, d = x.shape
    ctx_len = ctx.shape[1]
    n_lat = n_batch * seq
    n_tok = n_lat + n_batch * ctx_len

    cvec = jnp.zeros((MOD_ROWS, d), F32).at[:n_batch].set(c).at[n_batch].set(c_ctx)
    mods = _mod_matmul(cvec, mod_w, mod_b).reshape(DEPTH * MOD_ROWS * MOD_PIECES, 1, d)
    x_lat = x.reshape(n_lat, d)
    x_ctx = ctx.reshape(n_batch * ctx_len, d)

    qkv = _mod_matmul_qkv(x_lat, x_ctx, mods, 0, na_w_qkv[0].astype(BF16), seq)
    attn = _na_attention(qkv, _na_bias_table(na_rpb[0]), n_batch, seq, ctx_len, n_tok)
    attn = _ctx_attention(qkv, attn, n_batch, seq, ctx_len)
    h, f_in, scores = _proj_ln(attn, na_w_o[0].astype(BF16), None, x_lat, x_ctx, mods, 0, ln_mix_g[0], ln_mix_b[0],
                               moe_w_router[0].astype(BF16), n_tok, seq)
    h = _moe_and_norm(h, f_in, scores, mods, moe_router_bias[0], moe_w_gate, moe_w_up, moe_w_down,
                      sh_w_gate[0], sh_w_up[0], sh_w_down[0], ln_ffn_g[0], ln_ffn_b[0], n_tok, seq, 0)

    glu = _mod_matmul_glu(h, mods, 1, cv_w_pw1[0].astype(BF16), cv_b_pw1[0], n_lat, seq)
    conv = _dwconv(glu, cv_w_dw[0], cv_b_dw[0], n_batch, seq)
    h, f_in, scores = _proj_ln(conv, cv_w_pw2[0].astype(BF16), cv_b_pw2[0], h, None, mods, 1, ln_mix_g[1],
                               ln_mix_b[1], moe_w_router[1].astype(BF16), n_lat, seq,
                               pre_ln=(cv_ln_g[0], cv_ln_b[0]))
    h = _moe_and_norm(h, f_in, scores, mods, moe_router_bias[1], moe_w_gate, moe_w_up, moe_w_down,
                      sh_w_gate[1], sh_w_up[1], sh_w_down[1], ln_ffn_g[1], ln_ffn_b[1], n_lat, seq, 1)
    return h.reshape(n_batch, seq, d)
```
